```python
import math
import jax, jax.numpy as jnp
from jax import lax
import numpy as np

D_MODEL = 1024
BATCH = 2
SEQ = 16384
DEPTH = 4

GRID_W = 64
CTX_LEN = 256
N_MIXERS = 3
N_LAYERS_A = (DEPTH + 2) // 3
N_LAYERS_B = (DEPTH + 1) // 3
N_LAYERS_C = DEPTH // 3
EPS = 1e-6
N_MOD = 6
ROPE_BASE = 10000.0

DA_HEAD_DIM = 64
DA_V_DIM = 2 * DA_HEAD_DIM
DA_HEADS = D_MODEL // DA_V_DIM
Q_BLOCK = 128

RET_QK_DIM = 256
RET_HEADS = D_MODEL // RET_QK_DIM
RET_V_DIM = 2 * RET_QK_DIM
RET_CHUNK = 128

HY_ORDER = 2
HY_EMB = 33
HY_BANDS = (HY_EMB - 1) // 2
HY_FFN = 64
HY_TARGET = 1e-2
HY_FAST_DECAY_PCT = 0.3
HY_SLOW_DECAY_PCT = 1.5

N_EXPERTS = 16
EC_CAPACITY_FACTOR = 2
D_EXPERT = D_MODEL

kernel_name = "hybrid_diffattn_retention_hyena_ec_moe"

F32 = jnp.float32


def rms_norm(x, g):
    xf = x.astype(F32)
    y = xf * lax.rsqrt(jnp.mean(xf * xf, axis=-1, keepdims=True) + EPS)
    return (y * g.astype(F32)).astype(x.dtype)


def modulate(x, g, shift, scale):
    return rms_norm(x, g) * (1.0 + scale) + shift


def rope(x, cos, sin):
    n, A, F = cos.shape
    xr = x.astype(F32).reshape(x.shape[:-1] + (A, 2, F))
    x1, x2 = xr[..., 0, :], xr[..., 1, :]
    out = jnp.stack([x1 * cos - x2 * sin, x2 * cos + x1 * sin], axis=-2)
    return out.reshape(x.shape).astype(x.dtype)


def axial_rope_tables(rows, dim):
    f = dim // 4
    inv = 1.0 / (ROPE_BASE ** (jnp.arange(f, dtype=F32) / f))
    row = jnp.repeat(jnp.arange(rows), GRID_W).astype(F32)
    col = jnp.tile(jnp.arange(GRID_W), rows).astype(F32)
    ang = jnp.stack([row[:, None] * inv, col[:, None] * inv], axis=1)
    return jnp.cos(ang), jnp.sin(ang)


def retention_rope_tables(n):
    f = RET_QK_DIM // 2
    inv = 1.0 / (ROPE_BASE ** jnp.linspace(0.0, 1.0, f, dtype=F32))
    ang = jnp.arange(n, dtype=F32)[:, None] * inv
    return jnp.cos(ang)[:, None, :], jnp.sin(ang)[:, None, :]


def diff_attention(hc, hl, w_in, q_g, k_g, lam, sub_g, w_out, lambda_init, cos, sin, need_ctx):
    B, L, _ = hl.shape
    H, dh, dv = DA_HEADS, DA_HEAD_DIM, DA_V_DIM

    def project(h, rotate):
        n = h.shape[1]
        q, k, v = jnp.split(h @ w_in, [H * 2 * dh, 2 * H * 2 * dh], axis=-1)
        q = rms_norm(q.reshape(B, n, H, 2, dh), q_g).transpose(0, 2, 3, 1, 4)
        k = rms_norm(k.reshape(B, n, H, 2, dh), k_g).transpose(0, 2, 3, 1, 4)
        v = v.reshape(B, n, H, dv).transpose(0, 2, 1, 3)
        if rotate:
            q, k = rope(q, cos, sin), rope(k, cos, sin)
        return q, k, v

    q_c, k_c, v_c = project(hc, False)
    q_l, k_l, v_l = project(hl, True)
    lamf = lam.astype(F32)
    lam_val = jnp.exp(jnp.sum(lamf[0] * lamf[1])) - jnp.exp(jnp.sum(lamf[2] * lamf[3])) + lambda_init
    scale = dh ** -0.5

    def attend(q, k, v):
        s = jnp.einsum('bhcqd,bhckd->bhcqk', q, k).astype(F32) * scale
        p = jax.nn.softmax(s, axis=-1)
        a = p[:, :, 0] - lam_val * p[:, :, 1]
        return jnp.einsum('bhqk,bhkd->bhqd', a.astype(v.dtype), v)

    k_all = jnp.concatenate([k_c, k_l], axis=3)
    v_all = jnp.concatenate([v_c, v_l], axis=2)
    nb = L // Q_BLOCK
    qb = q_l.reshape(B, H, 2, nb, Q_BLOCK, dh).transpose(3, 0, 1, 2, 4, 5)
    ob = lax.map(lambda qblk: attend(qblk, k_all, v_all), qb)
    o_l = ob.transpose(1, 2, 0, 3, 4).reshape(B, H, L, dv)

    def finish(o):
        n = o.shape[2]
        o = rms_norm(o, sub_g) * (1.0 - lambda_init)
        return o.transpose(0, 2, 1, 3).reshape(B, n, H * dv) @ w_out

    out_c = finish(attend(q_c, k_c, v_c)) if need_ctx else None
    return out_c, finish(o_l)


def retention_chunks(q, k, v, log_g, state0, strict):
    B, H, N, dk = q.shape
    dv = v.shape[-1]
    T = RET_CHUNK
    nc = N // T
    i = jnp.arange(T, dtype=F32)
    diff = i[:, None] - i[None, :]
    mask = (diff > 0) if strict else (diff >= 0)
    dmask = jnp.where(mask[None], jnp.exp(jnp.where(mask, diff, 0.0)[None] * log_g[:, None, None]), 0.0)
    xi = jnp.exp((i + 1.0)[None] * log_g[:, None])[..., None]
    zeta = jnp.exp((T - 1.0 - i)[None] * log_g[:, None])[..., None]
    chunk_decay = jnp.exp(T * log_g)[:, None, None]

    def to_chunks(a):
        return a.reshape(B, H, nc, T, a.shape[-1]).transpose(2, 0, 1, 3, 4)

    def step(state, qkv):
        qc, kc, vc = qkv
        inner = jnp.einsum('bhid,bhjd->bhij', qc, kc) * dmask
        o = jnp.einsum('bhij,bhjd->bhid', inner, vc) + jnp.einsum('bhid,bhde->bhie', qc, state) * xi
        state = state * chunk_decay + jnp.einsum('bhjd,bhje->bhde', kc * zeta, vc)
        return state, o

    _, o = lax.scan(step, state0, (to_chunks(q), to_chunks(k), to_chunks(v)))
    return o.transpose(1, 2, 0, 3, 4).reshape(B, H, N, dv)


def retention_final_state(k, v, log_g):
    n = k.shape[2]
    w = jnp.exp((n - 1.0 - jnp.arange(n, dtype=F32))[None] * log_g[:, None])[..., None]
    return jnp.einsum('bhnd,bhne->bhde', k * w, v)


def retention(hc, hl, w_in, log_alpha, gn_g, w_out, cos, sin, need_ctx):
    B, L, _ = hl.shape
    H, dk, dv = RET_HEADS, RET_QK_DIM, RET_V_DIM
    log_g = jnp.log1p(-jnp.exp(log_alpha.astype(F32)))

    def project(h, rotate):
        n = h.shape[1]
        q, k, v, g = jnp.split(h @ w_in, [H * dk, 2 * H * dk, 2 * H * dk + H * dv], axis=-1)
        q = q.reshape(B, n, H, dk).transpose(0, 2, 1, 3).astype(F32)
        k = k.reshape(B, n, H, dk).transpose(0, 2, 1, 3).astype(F32) * (dk ** -0.5)
        v = v.reshape(B, n, H, dv).transpose(0, 2, 1, 3).astype(F32)
        if rotate:
            q, k = rope(q, cos, sin), rope(k, cos, sin)
        return q, k, v, g

    def flip(a):
        return jnp.flip(a, axis=2)

    q_c, k_c, v_c, g_c = project(hc, False)
    q_l, k_l, v_l, g_l = project(hl, True)
    s_f = retention_final_state(k_c, v_c, log_g[0])
    s_b = retention_final_state(flip(k_c), flip(v_c), log_g[1])

    def both_dirs(q, k, v, s0f, s0b):
        of = retention_chunks(q, k, v, log_g[0], s0f, False)
        ob = flip(retention_chunks(flip(q), flip(k), flip(v), log_g[1], s0b, True))
        return of + ob

    def finish(y, g):
        n = y.shape[2]
        y = rms_norm(y, gn_g).transpose(0, 2, 1, 3).reshape(B, n, H * dv)
        return (jax.nn.silu(g) * y.astype(g.dtype)) @ w_out

    out_l = finish(both_dirs(q_l, k_l, v_l, s_f, s_b), g_l)
    if need_ctx:
        zero = jnp.zeros_like(s_f)
        out_c = finish(both_dirs(q_c, k_c, v_c, zero, zero), g_c)
    else:
        out_c = None
    return out_c, out_l


def short_conv(u, w, b):
    n = u.shape[1]
    up = jnp.pad(u, ((0, 0), (1, 1), (0, 0)))
    return up[:, :n] * w[0] + up[:, 1:n + 1] * w[1] + up[:, 2:] * w[2] + b


def hyena_filter_spectra(n, w1, b1, w2, b2, freq, w3):
    t = jnp.linspace(0.0, 1.0, n, dtype=F32)[:, None]
    w = 2.0 * math.pi * jnp.arange(n, dtype=F32)[:, None] / n
    f = jnp.linspace(1e-4, HY_BANDS - 1, HY_BANDS, dtype=F32)[None]
    z = jnp.concatenate([t, jnp.cos(f * w), -jnp.sin(f * w)], axis=-1)
    a = jnp.sin(freq[0].astype(F32) * (z @ w1.astype(F32) + b1.astype(F32)))
    a = jnp.sin(freq[1].astype(F32) * (a @ w2.astype(F32) + b2.astype(F32)))
    C = w3.shape[-1] // (2 * HY_ORDER)
    h = (a @ w3.astype(F32)).reshape(n, HY_ORDER, 2, C)
    max_decay = math.log(HY_TARGET) / HY_FAST_DECAY_PCT
    min_decay = math.log(HY_TARGET) / HY_SLOW_DECAY_PCT
    deltas = jnp.abs(jnp.linspace(min_decay, max_decay, C, dtype=F32))
    h = h * jnp.exp(-t * deltas)[:, None, None, :]
    buf = jnp.concatenate([h[:, :, 0], jnp.zeros((1, HY_ORDER, C), F32),
                           jnp.flip(h[1:, :, 1], axis=0)], axis=0)
    buf = buf / jnp.sum(jnp.abs(buf), axis=0, keepdims=True)
    return jnp.fft.rfft(buf, axis=0)


def fft_long_conv(u, hf, bias):
    n = u.shape[1]
    uf = u.astype(F32)
    y = jnp.fft.irfft(jnp.fft.rfft(uf, n=2 * n, axis=1) * hf[None], n=2 * n, axis=1)[:, :n]
    return (y + uf * bias.astype(F32)).astype(u.dtype)


def hyena(h, w_in, b_in, conv_w, conv_b, w1, b1, w2, b2, freq, w3, fbias, w_out, b_out):
    n = h.shape[1]
    v, x1, x2 = jnp.split(short_conv(h @ w_in + b_in, conv_w, conv_b), 3, axis=-1)
    hf = hyena_filter_spectra(n, w1, b1, w2, b2, freq, w3)
    z = x1 * fft_long_conv(v, hf[:, 0], fbias[0])
    z = x2 * fft_long_conv(z, hf[:, 1], fbias[1])
    return z @ w_out + b_out


def expert_choice_ffn(h, router_w, w1, w3, w2):
    B, n, D = h.shape
    cap = EC_CAPACITY_FACTOR * n // N_EXPERTS
    aff = jax.nn.softmax((h @ router_w).astype(F32), axis=-1)
    gate, idx = lax.top_k(aff.transpose(0, 2, 1), cap)
    xg = jax.vmap(lambda hb, ib: hb[ib])(h, idx)
    hid = jax.nn.silu(jnp.einsum('becd,edf->becf', xg, w1)) * jnp.einsum('becd,edf->becf', xg, w3)
    y = jnp.einsum('becf,efd->becd', hid, w2) * gate[..., None].astype(h.dtype)
    return jax.vmap(lambda ib, yb: jnp.zeros((n, D), h.dtype).at[ib.reshape(-1)].add(yb.reshape(-1, D)))(idx, y)


def setup_inputs(seed: int = 0) -> dict:
    key = jax.random.key(seed)
    ks = iter(jax.random.split(key, 48))
    D, E, F = D_MODEL, N_EXPERTS, D_EXPERT

    def nrm(shape, scale=1.0):
        return jax.random.normal(next(ks), shape, F32) * scale

    def gain(shape):
        return 1.0 + nrm(shape, 0.02)

    ret_base = -(5.0 + jnp.arange(RET_HEADS, dtype=F32)) * math.log(2.0)
    return {
        "x": nrm((BATCH, SEQ, D)),
        "c": nrm((BATCH, D)),
        "ctx": nrm((BATCH, CTX_LEN, D)),
        "c_ctx": nrm((D,)),
        "norm_mix_g": gain((DEPTH, D)),
        "norm_ffn_g": gain((DEPTH, D)),
        "mod_w": nrm((DEPTH, D, N_MOD * D), 0.5 * D ** -0.5),
        "mod_b": nrm((DEPTH, N_MOD * D), 0.02),
        "da_w_in": nrm((N_LAYERS_A, D, 3 * D), D ** -0.5),
        "da_q_norm": gain((N_LAYERS_A, DA_HEAD_DIM)),
        "da_k_norm": gain((N_LAYERS_A, DA_HEAD_DIM)),
        "da_lambda": nrm((N_LAYERS_A, 4, DA_HEAD_DIM), 0.1),
        "da_sub_norm": gain((N_LAYERS_A, DA_V_DIM)),
        "da_w_out": nrm((N_LAYERS_A, D, D), D ** -0.5),
        "ret_w_in": nrm((N_LAYERS_B, D, 2 * RET_HEADS * RET_QK_DIM + 2 * RET_HEADS * RET_V_DIM), D ** -0.5),
        "ret_log_alpha": ret_base + nrm((N_LAYERS_B, 2, RET_HEADS), 0.05),
        "ret_group_norm": gain((N_LAYERS_B, RET_V_DIM)),
        "ret_w_out": nrm((N_LAYERS_B, RET_HEADS * RET_V_DIM, D), (RET_HEADS * RET_V_DIM) ** -0.5),
        "hy_w_in": nrm((N_LAYERS_C, D, 3 * D), D ** -0.5),
        "hy_b_in": nrm((N_LAYERS_C, 3 * D), 0.02),
        "hy_conv_w": nrm((N_LAYERS_C, 3, 3 * D), 3 ** -0.5),
        "hy_conv_b": nrm((N_LAYERS_C, 3 * D), 0.02),
        "hy_ffn_w1": nrm((N_LAYERS_C, HY_EMB, HY_FFN), HY_EMB ** -0.5),
        "hy_ffn_b1": nrm((N_LAYERS_C, HY_FFN), 0.02),
        "hy_ffn_w2": nrm((N_LAYERS_C, HY_FFN, HY_FFN), HY_FFN ** -0.5),
        "hy_ffn_b2": nrm((N_LAYERS_C, HY_FFN), 0.02),
        "hy_sin_freq": 1.0 + nrm((N_LAYERS_C, 2, HY_FFN), 0.1),
        "hy_ffn_w3": nrm((N_LAYERS_C, HY_FFN, 2 * HY_ORDER * D), HY_FFN ** -0.5),
        "hy_filter_bias": nrm((N_LAYERS_C, HY_ORDER, D)),
        "hy_w_out": nrm((N_LAYERS_C, D, D), D ** -0.5),
        "hy_b_out": nrm((N_LAYERS_C, D), 0.02),
        "router_w": nrm((DEPTH, D, E), D ** -0.5),
        "exp_w1": nrm((DEPTH, E, D, F), D ** -0.5),
        "exp_w3": nrm((DEPTH, E, D, F), D ** -0.5),
        "exp_w2": nrm((DEPTH, E, F, D), F ** -0.5),
    }


def reference(x, c, ctx, c_ctx, norm_mix_g, norm_ffn_g, mod_w, mod_b,
              da_w_in, da_q_norm, da_k_norm, da_lambda, da_sub_norm, da_w_out,
              ret_w_in, ret_log_alpha, ret_group_norm, ret_w_out,
              hy_w_in, hy_b_in, hy_conv_w, hy_conv_b, hy_ffn_w1, hy_ffn_b1, hy_ffn_w2, hy_ffn_b2,
              hy_sin_freq, hy_ffn_w3, hy_filter_bias, hy_w_out, hy_b_out,
              router_w, exp_w1, exp_w3, exp_w2):
    B, L, D = x.shape
    ROWS = L // GRID_W
    a_cos, a_sin = axial_rope_tables(ROWS, DA_HEAD_DIM)
    r_cos, r_sin = retention_rope_tables(L)
    silu_c = jax.nn.silu(c)
    silu_cc = jax.nn.silu(c_ctx)
    for i in range(DEPTH):
        last = i == DEPTH - 1
        m, s = i % N_MIXERS, i // N_MIXERS
        sh_l, sc_l, g_l, sh2_l, sc2_l, g2_l = jnp.split((silu_c @ mod_w[i] + mod_b[i])[:, None, :], N_MOD, axis=-1)
        sh_c, sc_c, g_c, sh2_c, sc2_c, g2_c = jnp.split(silu_cc @ mod_w[i] + mod_b[i], N_MOD, axis=-1)
        hl = modulate(x, norm_mix_g[i], sh_l, sc_l)
        if m == 0:
            hc = modulate(ctx, norm_mix_g[i], sh_c, sc_c)
            lambda_init = 0.8 - 0.6 * math.exp(-0.3 * i)
            oc, ol = diff_attention(hc, hl, da_w_in[s], da_q_norm[s], da_k_norm[s], da_lambda[s],
                                    da_sub_norm[s], da_w_out[s], lambda_init, a_cos, a_sin, not last)
        elif m == 1:
            hc = modulate(ctx, norm_mix_g[i], sh_c, sc_c)
            oc, ol = retention(hc, hl, ret_w_in[s], ret_log_alpha[s], ret_group_norm[s], ret_w_out[s],
                               r_cos, r_sin, not last)
        else:
            hy_p = (hy_w_in[s], hy_b_in[s], hy_conv_w[s], hy_conv_b[s], hy_ffn_w1[s], hy_ffn_b1[s],
                    hy_ffn_w2[s], hy_ffn_b2[s], hy_sin_freq[s], hy_ffn_w3[s], hy_filter_bias[s],
                    hy_w_out[s], hy_b_out[s])
            ol = hyena(hl, *hy_p)
            oc = None if last else hyena(modulate(ctx, norm_mix_g[i], sh_c, sc_c), *hy_p)
        x = x + g_l * ol
        x = x + g2_l * expert_choice_ffn(modulate(x, norm_ffn_g[i], sh2_l, sc2_l),
                                         router_w[i], exp_w1[i], exp_w3[i], exp_w2[i])
        if not last:
            ctx = ctx + g_c * oc
            ctx = ctx + g2_c * expert_choice_ffn(modulate(ctx, norm_ffn_g[i], sh2_c, sc2_c),
                                                 router_w[i], exp_w1[i], exp_w3[i], exp_w2[i])
    return x
```

```python
import functools
import math

import jax
import jax.numpy as jnp
import numpy as np
from jax import lax
from jax.experimental import pallas as pl
from jax.experimental.pallas import tpu as pltpu

F32 = jnp.float32
BF16 = jnp.bfloat16
I32 = jnp.int32

EPS = 1e-6
N_MOD = 6
ROPE_BASE = 10000.0
GRID_W = 64
N_MIXERS = 3

DA_HEAD_DIM = 64
DA_V_DIM = 128
RET_QK_DIM = 256
RET_V_DIM = 512
RET_CHUNK = 128

HY_ORDER = 2
HY_EMB = 33
HY_BANDS = (HY_EMB - 1) // 2
HY_TARGET = 1e-2
HY_FAST_DECAY_PCT = 0.3
HY_SLOW_DECAY_PCT = 1.5

N_EXPERTS = 16
EC_CAPACITY_FACTOR = 2

LANES = 128
VMEM_LIMIT_BYTES = 56 * 1024 * 1024


def _cparams(*sem):
    return pltpu.CompilerParams(dimension_semantics=sem, vmem_limit_bytes=VMEM_LIMIT_BYTES)


def _split_bf16(a):
    hi = a.astype(BF16)
    lo = (a - hi.astype(F32)).astype(BF16)
    return hi, lo


def _dot(a, b):
    return jnp.dot(a, b, preferred_element_type=F32)


def _dot3(a_hi, a_lo, b_hi, b_lo):
    return _dot(a_hi, b_hi) + _dot(a_lo, b_hi) + _dot(a_hi, b_lo)


def _modulate(x, g, sh, sc):
    ms = jnp.mean(x * x, axis=-1, keepdims=True)
    y = x * lax.rsqrt(ms + EPS) * g
    return y * (1.0 + sc) + sh


def _modvec_kernel(a_ref, w_ref, b_ref, o_ref):
    a = a_ref[...]
    a_hi, a_lo = _split_bf16(a * jax.nn.sigmoid(a))
    w_hi, w_lo = _split_bf16(w_ref[0])
    o_ref[0] = _dot3(a_hi, a_lo, w_hi, w_lo) + b_ref[0]


def mod_vectors(a, mod_w, mod_b):
    depth, d, n = mod_w.shape
    tn = 1024
    return pl.pallas_call(
        _modvec_kernel,
        grid=(depth, n // tn),
        in_specs=[pl.BlockSpec((8, d), lambda i, j: (0, 0)),
                  pl.BlockSpec((1, d, tn), lambda i, j: (i, 0, j)),
                  pl.BlockSpec((1, 1, tn), lambda i, j: (i, 0, j))],
        out_specs=pl.BlockSpec((1, 8, tn), lambda i, j: (i, 0, j)),
        out_shape=jax.ShapeDtypeStruct((depth, 8, n), F32),
        compiler_params=_cparams("parallel", "parallel"),
        name="mod_vectors",
    )(a, mod_w, mod_b.reshape(depth, 1, n))


def _mod_all(c, c_ctx, mod_w, mod_b, batch):
    d = c.shape[-1]
    a = jnp.zeros((8, d), F32).at[:batch].set(c).at[batch].set(c_ctx)
    mv = mod_vectors(a, mod_w, mod_b)
    names = ("sh", "sc", "g", "sh2", "sc2", "g2")
    layers = []
    for i in range(mod_w.shape[0]):
        parts = mv[i].reshape(8, N_MOD, d)
        layer = {}
        for k, nm in enumerate(names):
            layer[nm + "_l"] = parts[:batch, k][:, None, :]
            layer[nm + "_c"] = jnp.broadcast_to(parts[batch, k][None, None, :], (batch, 1, d))
        layers.append(layer)
    return layers


def _proj_kernel(x_ref, g_ref, sh_ref, sc_ref, w_ref, b_ref, *rest, epilogue, n_extra):
    extra, o_ref, h_scr = rest[:n_extra], rest[n_extra], rest[n_extra + 1]
    j = pl.program_id(2)

    @pl.when(j == 0)
    def _():
        h_scr[...] = _modulate(x_ref[0], g_ref[...], sh_ref[0], sc_ref[0]).astype(BF16)

    acc = _dot(h_scr[...], w_ref[...]) + b_ref[...]
    epilogue(acc, j, o_ref, *extra)


def _plain_epilogue(acc, j, o_ref):
    o_ref[0] = acc.astype(o_ref.dtype)


def mod_proj(x, g, sh, sc, w, bias, *, epilogue=_plain_epilogue, extra=(), extra_specs=(),
             out_dtype=BF16, tn=512, name="mod_proj"):
    B, n, K = x.shape
    N = w.shape[1]
    tm = min(512, n)
    tn = min(tn, N)
    kern = functools.partial(_proj_kernel, epilogue=epilogue, n_extra=len(extra))
    return pl.pallas_call(
        kern,
        grid=(B, n // tm, N // tn),
        in_specs=[pl.BlockSpec((1, tm, K), lambda b, i, j: (b, i, 0)),
                  pl.BlockSpec((1, K), lambda b, i, j: (0, 0)),
                  pl.BlockSpec((1, 1, K), lambda b, i, j: (b, 0, 0)),
                  pl.BlockSpec((1, 1, K), lambda b, i, j: (b, 0, 0)),
                  pl.BlockSpec((K, tn), lambda b, i, j: (0, j)),
                  pl.BlockSpec((1, tn), lambda b, i, j: (0, j)),
                  *extra_specs],
        out_specs=pl.BlockSpec((1, tm, tn), lambda b, i, j: (b, i, j)),
        out_shape=jax.ShapeDtypeStruct((B, n, N), out_dtype),
        scratch_shapes=[pltpu.VMEM((tm, K), BF16)],
        compiler_params=_cparams("parallel", "parallel", "arbitrary"),
        name=name,
    )(x, g.reshape(1, K), sh, sc, w, bias.reshape(1, N), *extra)


def _resmm_kernel(a_ref, w_ref, b_ref, r_ref, gt_ref, o_ref):
    acc = _dot(a_ref[0], w_ref[...]) + b_ref[...]
    o_ref[0] = r_ref[0] + gt_ref[0] * acc


def res_matmul(a, w, bias, res, gate):
    B, n, K = a.shape
    N = w.shape[1]
    tm = min(512, n)
    tn = min(512, N)
    return pl.pallas_call(
        _resmm_kernel,
        grid=(B, n // tm, N // tn),
        in_specs=[pl.BlockSpec((1, tm, K), lambda b, i, j: (b, i, 0)),
                  pl.BlockSpec((K, tn), lambda b, i, j: (0, j)),
                  pl.BlockSpec((1, tn), lambda b, i, j: (0, j)),
                  pl.BlockSpec((1, tm, tn), lambda b, i, j: (b, i, j)),
                  pl.BlockSpec((1, 1, tn), lambda b, i, j: (b, 0, j))],
        out_specs=pl.BlockSpec((1, tm, tn), lambda b, i, j: (b, i, j)),
        out_shape=jax.ShapeDtypeStruct((B, n, N), F32),
        compiler_params=_cparams("parallel", "parallel", "parallel"),
        name="res_matmul",
    )(a, w, bias.reshape(1, N), res, gate)


def _da_epilogue(acc, j, o_ref, cos_ref, sin_ref, qg_ref, kg_ref, gm_ref, *, tn, n_qk_tiles):
    @pl.when(j >= n_qk_tiles)
    def _():
        o_ref[0] = acc.astype(o_ref.dtype)

    @pl.when(j < n_qk_tiles)
    def _():
        gain = jnp.where(j < n_qk_tiles // 2, qg_ref[...], kg_ref[...])
        cos, sin = cos_ref[...], sin_ref[...]
        lane = lax.broadcasted_iota(I32, (1, LANES), 1)
        first_half = (lane % 32) < 16
        for s in range(tn // LANES):
            y = acc[:, s * LANES:(s + 1) * LANES]
            hi, lo = _split_bf16(y * y)
            gsum = _dot(hi, gm_ref[...]) + _dot(lo, gm_ref[...])
            yn = y * lax.rsqrt(gsum * (1.0 / DA_HEAD_DIM) + EPS) * gain
            partner = jnp.where(first_half, pltpu.roll(yn, LANES - 16, 1), pltpu.roll(yn, 16, 1))
            o_ref[0, :, s * LANES:(s + 1) * LANES] = (yn * cos + partner * sin).astype(o_ref.dtype)


def _attn_kernel(lam_ref, q_ref, kc_ref, vc_ref, *rest, has_lat, tk, n_chunks, out_scale):
    if has_lat:
        kl_ref, vl_ref, sg_ref, o_ref = rest
    else:
        sg_ref, o_ref = rest
    q = q_ref[0]
    tq = q.shape[0]
    lane = lax.broadcasted_iota(I32, (1, LANES), 1)
    zero = jnp.zeros_like(q)
    q2 = jnp.concatenate([jnp.where(lane < DA_HEAD_DIM, q, zero),
                          jnp.where(lane >= DA_HEAD_DIM, q, zero)], axis=0)

    def step(k, v, carry):
        m, l, acc = carry
        s = lax.dot_general(q2, k, (((1,), (1,)), ((), ())), preferred_element_type=F32)
        m_new = jnp.maximum(m, jnp.max(s, axis=-1, keepdims=True))
        alpha = jnp.exp(m - m_new)
        p = jnp.exp(s - m_new)
        l_new = alpha * l + jnp.sum(p, axis=-1, keepdims=True)
        acc_new = alpha * acc + _dot(p.astype(BF16), v)
        return m_new, l_new, acc_new

    carry = (jnp.full((2 * tq, 1), -1e30, F32), jnp.zeros((2 * tq, 1), F32),
             jnp.zeros((2 * tq, LANES), F32))
    carry = step(kc_ref[0], vc_ref[0], carry)
    if has_lat:
        def body(c, carry):
            off = pl.multiple_of(c * tk, tk)
            return step(kl_ref[0, pl.ds(off, tk), :], vl_ref[0, pl.ds(off, tk), :], carry)
        carry = lax.fori_loop(0, n_chunks, body, carry)
    _, l, acc = carry
    o = acc / l
    o = o[:tq] - lam_ref[0] * o[tq:]
    ms = jnp.mean(o * o, axis=-1, keepdims=True)
    o_ref[0] = (o * lax.rsqrt(ms + EPS) * sg_ref[...] * out_scale).astype(o_ref.dtype)


def diff_attention_core(lam, qkv_q, qkv_c, qkv_l, sub_g, out_scale):
    B, nq, D3 = qkv_q.shape
    D = D3 // 3
    H = D // DA_V_DIM
    nc = qkv_c.shape[1]
    tq = min(256, nq)
    has_lat = qkv_l is not None
    in_specs = [pl.BlockSpec(memory_space=pltpu.SMEM),
                pl.BlockSpec((1, tq, LANES), lambda b, h, i: (b, i, h)),
                pl.BlockSpec((1, nc, LANES), lambda b, h, i: (b, 0, H + h)),
                pl.BlockSpec((1, nc, LANES), lambda b, h, i: (b, 0, 2 * H + h))]
    args = [lam, qkv_q, qkv_c, qkv_c]
    tk, n_chunks = 0, 0
    if has_lat:
        nl = qkv_l.shape[1]
        tk = min(512, nl)
        n_chunks = nl // tk
        in_specs += [pl.BlockSpec((1, nl, LANES), lambda b, h, i: (b, 0, H + h)),
                     pl.BlockSpec((1, nl, LANES), lambda b, h, i: (b, 0, 2 * H + h))]
        args += [qkv_l, qkv_l]
    in_specs.append(pl.BlockSpec((1, LANES), lambda b, h, i: (0, 0)))
    args.append(sub_g.reshape(1, LANES))
    kern = functools.partial(_attn_kernel, has_lat=has_lat, tk=tk, n_chunks=n_chunks,
                             out_scale=out_scale)
    return pl.pallas_call(
        kern,
        grid=(B, H, nq // tq),
        in_specs=in_specs,
        out_specs=pl.BlockSpec((1, tq, LANES), lambda b, h, i: (b, i, h)),
        out_shape=jax.ShapeDtypeStruct((B, nq, D), BF16),
        compiler_params=_cparams("parallel", "parallel", "arbitrary"),
        name="diff_attention",
    )(*args)


def _axial_rope_tables(n):
    f = DA_HEAD_DIM // 4
    inv = 1.0 / (ROPE_BASE ** (jnp.arange(f, dtype=F32) / f))
    rows = n // GRID_W
    row = jnp.repeat(jnp.arange(rows), GRID_W).astype(F32)
    col = jnp.tile(jnp.arange(GRID_W), rows).astype(F32)
    ang = jnp.stack([row[:, None] * inv, col[:, None] * inv], axis=1)
    cos, sin = jnp.cos(ang), jnp.sin(ang)
    cos64 = jnp.concatenate([cos, cos], axis=-1).reshape(n, DA_HEAD_DIM)
    sin64 = jnp.concatenate([-sin, sin], axis=-1).reshape(n, DA_HEAD_DIM)
    return jnp.tile(cos64, (1, 2)), jnp.tile(sin64, (1, 2))


def da_project(h_in, g, sh, sc, w_in_bf, q_g, k_g, cos, sin):
    B, n, D = h_in.shape
    tn = 512
    tm = min(512, n)
    gm = (np.arange(LANES)[:, None] // DA_HEAD_DIM == np.arange(LANES)[None, :] // DA_HEAD_DIM)
    gm = jnp.asarray(gm, BF16)
    qg = (jnp.tile(q_g.astype(F32), 2) * (DA_HEAD_DIM ** -0.5)).reshape(1, LANES)
    kg = jnp.tile(k_g.astype(F32), 2).reshape(1, LANES)
    extra = (cos, sin, qg, kg, gm)
    extra_specs = (pl.BlockSpec((tm, LANES), lambda b, i, j: (i, 0)),
                   pl.BlockSpec((tm, LANES), lambda b, i, j: (i, 0)),
                   pl.BlockSpec((1, LANES), lambda b, i, j: (0, 0)),
                   pl.BlockSpec((1, LANES), lambda b, i, j: (0, 0)),
                   pl.BlockSpec((LANES, LANES), lambda b, i, j: (0, 0)))
    epi = functools.partial(_da_epilogue, tn=tn, n_qk_tiles=2 * D // tn)
    return mod_proj(h_in, g, sh, sc, w_in_bf, jnp.zeros((3 * D,), F32), epilogue=epi, extra=extra,
                    extra_specs=extra_specs, tn=tn, name="da_project")


def diff_attention_layer(x, ctx, mv, norm_g, w_in, q_g, k_g, lam, sub_g, w_out, lambda_init, need_ctx):
    B, L, D = x.shape
    nc = ctx.shape[1]
    w_in_bf = w_in.astype(BF16)
    w_out_bf = w_out.astype(BF16)
    cos_l, sin_l = _axial_rope_tables(L)
    cos_c, sin_c = jnp.ones((nc, LANES), F32), jnp.zeros((nc, LANES), F32)
    qkv_l = da_project(x, norm_g, mv["sh_l"], mv["sc_l"], w_in_bf, q_g, k_g, cos_l, sin_l)
    qkv_c = da_project(ctx, norm_g, mv["sh_c"], mv["sc_c"], w_in_bf, q_g, k_g, cos_c, sin_c)
    lamf = lam.astype(F32)
    lam_val = (jnp.exp(jnp.sum(lamf[0] * lamf[1])) - jnp.exp(jnp.sum(lamf[2] * lamf[3]))
               + lambda_init).reshape(1)
    out_scale = 1.0 - lambda_init
    o_l = diff_attention_core(lam_val, qkv_l, qkv_c, qkv_l, sub_g, out_scale)
    zb = jnp.zeros((D,), F32)
    x = res_matmul(o_l, w_out_bf, zb, x, mv["g_l"])
    if need_ctx:
        o_c = diff_attention_core(lam_val, qkv_c, qkv_c, None, sub_g, out_scale)
        ctx = res_matmul(o_c, w_out_bf, zb, ctx, mv["g_c"])
    return x, ctx


def _router_kernel(x_ref, g_ref, sh_ref, sc_ref, whi_ref, wlo_ref, o_ref):
    h = _modulate(x_ref[0], g_ref[...], sh_ref[0], sc_ref[0])
    o_ref[0] = _router_affinity(h, whi_ref[...], wlo_ref[...])


def _router_affinity(h, w_hi, w_lo):
    h_hi, h_lo = _split_bf16(h)
    logits = _dot3(h_hi, h_lo, w_hi, w_lo)
    lane = lax.broadcasted_iota(I32, (1, LANES), 1)
    logits = jnp.where(lane < N_EXPERTS, logits, -1e30)
    z = jnp.exp(logits - jnp.max(logits, axis=-1, keepdims=True))
    return z / jnp.sum(z, axis=-1, keepdims=True)


def router_affinity(x, g, sh, sc, w_hi, w_lo):
    B, n, D = x.shape
    tm = min(512, n)
    return pl.pallas_call(
        _router_kernel,
        grid=(B, n // tm),
        in_specs=[pl.BlockSpec((1, tm, D), lambda b, i: (b, i, 0)),
                  pl.BlockSpec((1, D), lambda b, i: (0, 0)),
                  pl.BlockSpec((1, 1, D), lambda b, i: (b, 0, 0)),
                  pl.BlockSpec((1, 1, D), lambda b, i: (b, 0, 0)),
                  pl.BlockSpec((D, LANES), lambda b, i: (0, 0)),
                  pl.BlockSpec((D, LANES), lambda b, i: (0, 0))],
        out_specs=pl.BlockSpec((1, tm, LANES), lambda b, i: (b, i, 0)),
        out_shape=jax.ShapeDtypeStruct((B, n, LANES), F32),
        compiler_params=_cparams("parallel", "parallel"),
        name="router",
    )(x, g.reshape(1, D), sh, sc, w_hi, w_lo)


def _ffn_kernel(idx_ref, x_hbm, acc_in, g_ref, sh_ref, sc_ref, g2_ref, rhi_ref, rlo_ref,
                w1_ref, w3_ref, w2_ref, o_hbm, xbuf, abuf, sem, *, n, cap, tm):
    del acc_in
    b, e, j = pl.program_id(0), pl.program_id(1), pl.program_id(2)
    n_exp = pl.num_programs(1)
    base = (b * n_exp + e) * cap + j * tm
    row0 = b * n

    def gather(k, c):
        row = row0 + idx_ref[base + k]
        pltpu.make_async_copy(x_hbm.at[pl.ds(row, 1)], xbuf.at[pl.ds(k, 1)], sem.at[0]).start()
        pltpu.make_async_copy(o_hbm.at[pl.ds(row, 1)], abuf.at[pl.ds(k, 1)], sem.at[1]).start()
        return c

    lax.fori_loop(0, tm, gather, 0)
    pltpu.make_async_copy(x_hbm.at[pl.ds(0, tm)], xbuf, sem.at[0]).wait()

    h = _modulate(xbuf[...], g_ref[...], sh_ref[0], sc_ref[0])
    aff = _router_affinity(h, rhi_ref[...], rlo_ref[...])
    lane = lax.broadcasted_iota(I32, (1, LANES), 1)
    gate = jnp.sum(jnp.where(lane == e, aff, 0.0), axis=-1, keepdims=True)
    hb = h.astype(BF16)
    a1 = _dot(hb, w1_ref[0])
    a3 = _dot(hb, w3_ref[0])
    hid = (a1 * jax.nn.sigmoid(a1) * a3).astype(BF16)
    y = _dot(hid, w2_ref[0])
    upd = g2_ref[0] * (gate * y)

    pltpu.make_async_copy(o_hbm.at[pl.ds(0, tm)], abuf, sem.at[1]).wait()
    abuf[...] = abuf[...] + upd

    def scatter(k, c):
        row = row0 + idx_ref[base + k]
        pltpu.make_async_copy(abuf.at[pl.ds(k, 1)], o_hbm.at[pl.ds(row, 1)], sem.at[2]).start()
        return c

    lax.fori_loop(0, tm, scatter, 0)
    pltpu.make_async_copy(abuf, o_hbm.at[pl.ds(0, tm)], sem.at[2]).wait()


def expert_ffn(idx, x, g, sh, sc, g2, r_hi, r_lo, w1, w3, w2):
    B, n, D = x.shape
    E, cap = idx.shape[1], idx.shape[2]
    F = w1.shape[-1]
    tm = min(256, cap)
    x2 = x.reshape(B * n, D)
    kern = functools.partial(_ffn_kernel, n=n, cap=cap, tm=tm)
    vec = lambda: pl.BlockSpec((1, 1, D), lambda b, e, j, idx: (b, 0, 0))
    grid_spec = pltpu.PrefetchScalarGridSpec(
        num_scalar_prefetch=1,
        grid=(B, E, cap // tm),
        in_specs=[pl.BlockSpec(memory_space=pl.ANY),
                  pl.BlockSpec(memory_space=pl.ANY),
                  pl.BlockSpec((1, D), lambda b, e, j, idx: (0, 0)),
                  vec(), vec(), vec(),
                  pl.BlockSpec((D, LANES), lambda b, e, j, idx: (0, 0)),
                  pl.BlockSpec((D, LANES), lambda b, e, j, idx: (0, 0)),
                  pl.BlockSpec((1, D, F), lambda b, e, j, idx: (e, 0, 0)),
                  pl.BlockSpec((1, D, F), lambda b, e, j, idx: (e, 0, 0)),
                  pl.BlockSpec((1, F, D), lambda b, e, j, idx: (e, 0, 0))],
        out_specs=pl.BlockSpec(memory_space=pl.ANY),
        scratch_shapes=[pltpu.VMEM((tm, D), F32), pltpu.VMEM((tm, D), F32),
                        pltpu.SemaphoreType.DMA((3,))])
    out = pl.pallas_call(
        kern,
        grid_spec=grid_spec,
        out_shape=jax.ShapeDtypeStruct((B * n, D), F32),
        input_output_aliases={2: 0},
        compiler_params=_cparams("arbitrary", "arbitrary", "arbitrary"),
        name="expert_ffn",
    )(idx.reshape(-1), x2, x2, g.reshape(1, D), sh, sc, g2, r_hi, r_lo, w1, w3, w2)
    return out.reshape(B, n, D)


def moe_layer(x, g, sh, sc, g2, router_w, w1, w3, w2):
    B, n, D = x.shape
    E = router_w.shape[1]
    cap = EC_CAPACITY_FACTOR * n // E
    rw = jnp.pad(router_w.astype(F32), ((0, 0), (0, LANES - E)))
    r_hi, r_lo = _split_bf16(rw)
    aff = router_affinity(x, g, sh, sc, r_hi, r_lo)
    _, idx = lax.top_k(aff[:, :, :E].transpose(0, 2, 1), cap)
    idx = jnp.sort(idx, axis=-1).astype(I32)
    return expert_ffn(idx, x, g, sh, sc, g2, r_hi, r_lo, w1, w3, w2)


def _rope_half(x, cos, sin):
    f = cos.shape[-1]
    x1, x2 = x[..., :f], x[..., f:]
    return jnp.concatenate([x1 * cos - x2 * sin, x2 * cos + x1 * sin], axis=-1)


def _retention_chunks(q, k, v, log_g, state0, strict):
    B, H, N, dk = q.shape
    T = RET_CHUNK
    nc = N // T
    i = jnp.arange(T, dtype=F32)
    diff = i[:, None] - i[None, :]
    mask = (diff > 0) if strict else (diff >= 0)
    dmask = jnp.where(mask[None], jnp.exp(jnp.where(mask, diff, 0.0)[None] * log_g[:, None, None]), 0.0)
    xi = jnp.exp((i + 1.0)[None] * log_g[:, None])[..., None]
    zeta = jnp.exp((T - 1.0 - i)[None] * log_g[:, None])[..., None]
    chunk_decay = jnp.exp(T * log_g)[:, None, None]

    def to_chunks(a):
        return a.reshape(B, H, nc, T, a.shape[-1]).transpose(2, 0, 1, 3, 4)

    def step(state, qkv):
        qc, kc, vc = qkv
        inner = jnp.einsum('bhid,bhjd->bhij', qc, kc) * dmask
        o = jnp.einsum('bhij,bhjd->bhid', inner, vc) + jnp.einsum('bhid,bhde->bhie', qc, state) * xi
        state = state * chunk_decay + jnp.einsum('bhjd,bhje->bhde', kc * zeta, vc)
        return state, o

    state, o = lax.scan(step, state0, (to_chunks(q), to_chunks(k), to_chunks(v)))
    return o.transpose(1, 2, 0, 3, 4).reshape(B, H, N, v.shape[-1]), state


def retention_layer(x, ctx, mv, norm_g, w_in, log_alpha, gn_g, w_out, need_ctx):
    B, L, D = x.shape
    dk, dv = RET_QK_DIM, RET_V_DIM
    H = D // dk
    w_in_bf = w_in.astype(BF16)
    w_out_bf = w_out.astype(BF16)
    zb = jnp.zeros((w_in.shape[1],), F32)
    log_g = jnp.log1p(-jnp.exp(log_alpha.astype(F32)))
    f = dk // 2
    inv = 1.0 / (ROPE_BASE ** jnp.linspace(0.0, 1.0, f, dtype=F32))
    ang = jnp.arange(L, dtype=F32)[:, None] * inv
    cos, sin = jnp.cos(ang), jnp.sin(ang)

    def project(h_in, sh, sc, rotate):
        n = h_in.shape[1]
        p = mod_proj(h_in, norm_g, sh, sc, w_in_bf, zb, out_dtype=F32, name="ret_project")
        q, k, v, g = jnp.split(p, [H * dk, 2 * H * dk, 2 * H * dk + H * dv], axis=-1)
        q = q.reshape(B, n, H, dk).transpose(0, 2, 1, 3)
        k = k.reshape(B, n, H, dk).transpose(0, 2, 1, 3) * (dk ** -0.5)
        v = v.reshape(B, n, H, dv).transpose(0, 2, 1, 3)
        if rotate:
            q, k = _rope_half(q, cos, sin), _rope_half(k, cos, sin)
        return q, k, v, g

    def flip(a):
        return jnp.flip(a, axis=2)

    def both_dirs(q, k, v, s0f, s0b):
        of, sf = _retention_chunks(q, k, v, log_g[0], s0f, False)
        ob, sb = _retention_chunks(flip(q), flip(k), flip(v), log_g[1], s0b, True)
        return of + flip(ob), sf, sb

    def finish(y, g, res, gate):
        n = y.shape[2]
        ms = jnp.mean(y * y, axis=-1, keepdims=True)
        y = (y * lax.rsqrt(ms + EPS) * gn_g.astype(F32)).transpose(0, 2, 1, 3).reshape(B, n, H * dv)
        z = (jax.nn.silu(g) * y).astype(BF16)
        return res_matmul(z, w_out_bf, jnp.zeros((D,), F32), res, gate)

    q_c, k_c, v_c, g_c = project(ctx, mv["sh_c"], mv["sc_c"], False)
    q_l, k_l, v_l, g_l = project(x, mv["sh_l"], mv["sc_l"], True)
    zero = jnp.zeros((B, H, dk, dv), F32)
    y_c, s_f, s_b = both_dirs(q_c, k_c, v_c, zero, zero)
    y_l, _, _ = both_dirs(q_l, k_l, v_l, s_f, s_b)
    x = finish(y_l, g_l, x, mv["g_l"])
    if need_ctx:
        ctx = finish(y_c, g_c, ctx, mv["g_c"])
    return x, ctx


def _hyena_filter_spectra(n, w1, b1, w2, b2, freq, w3):
    t = jnp.linspace(0.0, 1.0, n, dtype=F32)[:, None]
    w = 2.0 * math.pi * jnp.arange(n, dtype=F32)[:, None] / n
    f = jnp.linspace(1e-4, HY_BANDS - 1, HY_BANDS, dtype=F32)[None]
    z = jnp.concatenate([t, jnp.cos(f * w), -jnp.sin(f * w)], axis=-1)
    hp = lax.Precision.HIGHEST
    a = jnp.sin(freq[0].astype(F32) * (jnp.dot(z, w1.astype(F32), precision=hp) + b1.astype(F32)))
    a = jnp.sin(freq[1].astype(F32) * (jnp.dot(a, w2.astype(F32), precision=hp) + b2.astype(F32)))
    C = w3.shape[-1] // (2 * HY_ORDER)
    h = jnp.dot(a, w3.astype(F32), precision=hp).reshape(n, HY_ORDER, 2, C)
    max_decay = math.log(HY_TARGET) / HY_FAST_DECAY_PCT
    min_decay = math.log(HY_TARGET) / HY_SLOW_DECAY_PCT
    deltas = jnp.abs(jnp.linspace(min_decay, max_decay, C, dtype=F32))
    h = h * jnp.exp(-t * deltas)[:, None, None, :]
    buf = jnp.concatenate([h[:, :, 0], jnp.zeros((1, HY_ORDER, C), F32),
                           jnp.flip(h[1:, :, 1], axis=0)], axis=0)
    buf = buf / jnp.sum(jnp.abs(buf), axis=0, keepdims=True)
    return jnp.fft.rfft(buf, axis=0)


def _fft_long_conv(u, hf, bias):
    n = u.shape[1]
    y = jnp.fft.irfft(jnp.fft.rfft(u, n=2 * n, axis=1) * hf[None], n=2 * n, axis=1)[:, :n]
    return y + u * bias.astype(F32)


def hyena_layer(x, ctx, mv, norm_g, hy_p, need_ctx):
    (w_in, b_in, conv_w, conv_b, w1, b1, w2, b2, freq, w3, fbias, w_out, b_out) = hy_p
    w_in_bf = w_in.astype(BF16)
    w_out_bf = w_out.astype(BF16)

    def mix(h_in, sh, sc, res, gate):
        n = h_in.shape[1]
        u = mod_proj(h_in, norm_g, sh, sc, w_in_bf, b_in.astype(F32), out_dtype=F32, name="hy_project")
        up = jnp.pad(u, ((0, 0), (1, 1), (0, 0)))
        u = up[:, :n] * conv_w[0] + up[:, 1:n + 1] * conv_w[1] + up[:, 2:] * conv_w[2] + conv_b
        v, x1, x2 = jnp.split(u, 3, axis=-1)
        hf = _hyena_filter_spectra(n, w1, b1, w2, b2, freq, w3)
        z = x1 * _fft_long_conv(v, hf[:, 0], fbias[0])
        z = x2 * _fft_long_conv(z, hf[:, 1], fbias[1])
        return res_matmul(z.astype(BF16), w_out_bf, b_out.astype(F32), res, gate)

    x = mix(x, mv["sh_l"], mv["sc_l"], x, mv["g_l"])
    if need_ctx:
        ctx = mix(ctx, mv["sh_c"], mv["sc_c"], ctx, mv["g_c"])
    return x, ctx


def kernel(x, c, ctx, c_ctx, norm_mix_g, norm_ffn_g, mod_w, mod_b,
           da_w_in, da_q_norm, da_k_norm, da_lambda, da_sub_norm, da_w_out,
           ret_w_in, ret_log_alpha, ret_group_norm, ret_w_out,
           hy_w_in, hy_b_in, hy_conv_w, hy_conv_b, hy_ffn_w1, hy_ffn_b1, hy_ffn_w2, hy_ffn_b2,
           hy_sin_freq, hy_ffn_w3, hy_filter_bias, hy_w_out, hy_b_out,
           router_w, exp_w1, exp_w3, exp_w2):
    B = x.shape[0]
    depth = mod_w.shape[0]
    mods = _mod_all(c, c_ctx, mod_w, mod_b, B)
    for i in range(depth):
        last = i == depth - 1
        m, s = i % N_MIXERS, i // N_MIXERS
        mv = mods[i]
        if m == 0:
            lambda_init = 0.8 - 0.6 * math.exp(-0.3 * i)
            x, ctx = diff_attention_layer(x, ctx, mv, norm_mix_g[i], da_w_in[s], da_q_norm[s], da_k_norm[s],
                                          da_lambda[s], da_sub_norm[s], da_w_out[s], lambda_init, not last)
        elif m == 1:
            x, ctx = retention_layer(x, ctx, mv, norm_mix_g[i], ret_w_in[s], ret_log_alpha[s],
                                     ret_group_norm[s], ret_w_out[s], not last)
        else:
            hy_p = (hy_w_in[s], hy_b_in[s], hy_conv_w[s], hy_conv_b[s], hy_ffn_w1[s], hy_ffn_b1[s],
                    hy_ffn_w2[s], hy_ffn_b2[s], hy_sin_freq[s], hy_ffn_w3[s], hy_filter_bias[s],
                    hy_w_out[s], hy_b_out[s])
            x, ctx = hyena_layer(x, ctx, mv, norm_mix_g[i], hy_p, not last)
        w1, w3, w2 = exp_w1[i].astype(BF16), exp_w3[i].astype(BF16), exp_w2[i].astype(BF16)
        x = moe_layer(x, norm_ffn_g[i], mv["sh2_l"], mv["sc2_l"], mv["g2_l"], router_w[i], w1, w3, w2)
        if not last:
            ctx = moe_layer(ctx, norm_ffn_g[i], mv["sh2_c"], mv["sc2_c"], mv["g2_c"], router_w[i], w1, w3, w2)
    return x
```

```python
import functools
import math

import jax
import jax.numpy as jnp
import numpy as np
from jax import lax
from jax.experimental import pallas as pl
from jax.experimental.pallas import tpu as pltpu

F32 = jnp.float32
BF16 = jnp.bfloat16
I32 = jnp.int32

EPS = 1e-6
N_MOD = 6
ROPE_BASE = 10000.0
GRID_W = 64
N_MIXERS = 3

DA_HEAD_DIM = 64
DA_V_DIM = 128
RET_QK_DIM = 256
RET_V_DIM = 512
RET_CHUNK = 128

HY_ORDER = 2
HY_EMB = 33
HY_BANDS = (HY_EMB - 1) // 2
HY_TARGET = 1e-2
HY_FAST_DECAY_PCT = 0.3
HY_SLOW_DECAY_PCT = 1.5

N_EXPERTS = 16
EC_CAPACITY_FACTOR = 2

LANES = 128
VMEM_LIMIT_BYTES = 56 * 1024 * 1024


def _cparams(*sem):
    return pltpu.CompilerParams(dimension_semantics=sem, vmem_limit_bytes=VMEM_LIMIT_BYTES)


def _split_bf16(a):
    hi = a.astype(BF16)
    lo = (a - hi.astype(F32)).astype(BF16)
    return hi, lo


def _dot(a, b):
    return jnp.dot(a, b, preferred_element_type=F32)


def _dot3(a_hi, a_lo, b_hi, b_lo):
    return _dot(a_hi, b_hi) + _dot(a_lo, b_hi) + _dot(a_hi, b_lo)


def _modulate(x, g, sh, sc):
    ms = jnp.mean(x * x, axis=-1, keepdims=True)
    y = x * lax.rsqrt(ms + EPS) * g
    return y * (1.0 + sc) + sh


def _modvec_kernel(a_ref, w_ref, b_ref, o_ref):
    a = a_ref[...]
    a_hi, a_lo = _split_bf16(a * jax.nn.sigmoid(a))
    w_hi, w_lo = _split_bf16(w_ref[0])
    o_ref[0] = _dot3(a_hi, a_lo, w_hi, w_lo) + b_ref[0]


def mod_vectors(a, mod_w, mod_b):
    depth, d, n = mod_w.shape
    tn = 1024
    return pl.pallas_call(
        _modvec_kernel,
        grid=(depth, n // tn),
        in_specs=[pl.BlockSpec((8, d), lambda i, j: (0, 0)),
                  pl.BlockSpec((1, d, tn), lambda i, j: (i, 0, j)),
                  pl.BlockSpec((1, 1, tn), lambda i, j: (i, 0, j))],
        out_specs=pl.BlockSpec((1, 8, tn), lambda i, j: (i, 0, j)),
        out_shape=jax.ShapeDtypeStruct((depth, 8, n), F32),
        compiler_params=_cparams("parallel", "parallel"),
        name="mod_vectors",
    )(a, mod_w, mod_b.reshape(depth, 1, n))


def _mod_all(c, c_ctx, mod_w, mod_b, batch):
    d = c.shape[-1]
    a = jnp.zeros((8, d), F32).at[:batch].set(c).at[batch].set(c_ctx)
    mv = mod_vectors(a, mod_w, mod_b)
    names = ("sh", "sc", "g", "sh2", "sc2", "g2")
    layers = []
    for i in range(mod_w.shape[0]):
        parts = mv[i].reshape(8, N_MOD, d)
        layer = {}
        for k, nm in enumerate(names):
            layer[nm + "_l"] = parts[:batch, k][:, None, :]
            layer[nm + "_c"] = jnp.broadcast_to(parts[batch, k][None, None, :], (batch, 1, d))
        layers.append(layer)
    return layers


def _proj_kernel(x_ref, g_ref, sh_ref, sc_ref, w_ref, b_ref, *rest, epilogue, n_extra):
    extra, o_ref, h_scr = rest[:n_extra], rest[n_extra], rest[n_extra + 1]
    j = pl.program_id(2)

    @pl.when(j == 0)
    def _():
        h_scr[...] = _modulate(x_ref[0], g_ref[...], sh_ref[0], sc_ref[0]).astype(BF16)

    acc = _dot(h_scr[...], w_ref[...]) + b_ref[...]
    epilogue(acc, j, o_ref, *extra)


def _plain_epilogue(acc, j, o_ref):
    o_ref[0] = acc.astype(o_ref.dtype)


def mod_proj(x, g, sh, sc, w, bias, *, epilogue=_plain_epilogue, extra=(), extra_specs=(),
             out_dtype=BF16, tn=512, name="mod_proj"):
    B, n, K = x.shape
    N = w.shape[1]
    tm = min(512, n)
    tn = min(tn, N)
    kern = functools.partial(_proj_kernel, epilogue=epilogue, n_extra=len(extra))
    return pl.pallas_call(
        kern,
        grid=(B, n // tm, N // tn),
        in_specs=[pl.BlockSpec((1, tm, K), lambda b, i, j: (b, i, 0)),
                  pl.BlockSpec((1, K), lambda b, i, j: (0, 0)),
                  pl.BlockSpec((1, 1, K), lambda b, i, j: (b, 0, 0)),
                  pl.BlockSpec((1, 1, K), lambda b, i, j: (b, 0, 0)),
                  pl.BlockSpec((K, tn), lambda b, i, j: (0, j)),
                  pl.BlockSpec((1, tn), lambda b, i, j: (0, j)),
                  *extra_specs],
        out_specs=pl.BlockSpec((1, tm, tn), lambda b, i, j: (b, i, j)),
        out_shape=jax.ShapeDtypeStruct((B, n, N), out_dtype),
        scratch_shapes=[pltpu.VMEM((tm, K), BF16)],
        compiler_params=_cparams("parallel", "parallel", "arbitrary"),
        name=name,
    )(x, g.reshape(1, K), sh, sc, w, bias.reshape(1, N), *extra)


def _resmm_kernel(a_ref, w_ref, b_ref, r_ref, gt_ref, o_ref):
    acc = _dot(a_ref[0], w_ref[...]) + b_ref[...]
    o_ref[0] = r_ref[0] + gt_ref[0] * acc


def res_matmul(a, w, bias, res, gate):
    B, n, K = a.shape
    N = w.shape[1]
    tm = min(512, n)
    tn = min(512, N)
    return pl.pallas_call(
        _resmm_kernel,
        grid=(B, n // tm, N // tn),
        in_specs=[pl.BlockSpec((1, tm, K), lambda b, i, j: (b, i, 0)),
                  pl.BlockSpec((K, tn), lambda b, i, j: (0, j)),
                  pl.BlockSpec((1, tn), lambda b, i, j: (0, j)),
                  pl.BlockSpec((1, tm, tn), lambda b, i, j: (b, i, j)),
                  pl.BlockSpec((1, 1, tn), lambda b, i, j: (b, 0, j))],
        out_specs=pl.BlockSpec((1, tm, tn), lambda b, i, j: (b, i, j)),
        out_shape=jax.ShapeDtypeStruct((B, n, N), F32),
        compiler_params=_cparams("parallel", "parallel", "parallel"),
        name="res_matmul",
    )(a, w, bias.reshape(1, N), res, gate)


def _da_epilogue(acc, j, o_ref, cos_ref, sin_ref, qg_ref, kg_ref, gm_ref, *, tn, n_qk_tiles):
    @pl.when(j >= n_qk_tiles)
    def _():
        o_ref[0] = acc.astype(o_ref.dtype)

    @pl.when(j < n_qk_tiles)
    def _():
        gain = jnp.where(j < n_qk_tiles // 2, qg_ref[...], kg_ref[...])
        cos, sin = cos_ref[...], sin_ref[...]
        lane = lax.broadcasted_iota(I32, (1, LANES), 1)
        first_half = (lane % 32) < 16
        for s in range(tn // LANES):
            y = acc[:, s * LANES:(s + 1) * LANES]
            hi, lo = _split_bf16(y * y)
            gsum = _dot(hi, gm_ref[...]) + _dot(lo, gm_ref[...])
            yn = y * lax.rsqrt(gsum * (1.0 / DA_HEAD_DIM) + EPS) * gain
            partner = jnp.where(first_half, pltpu.roll(yn, LANES - 16, 1), pltpu.roll(yn, 16, 1))
            o_ref[0, :, s * LANES:(s + 1) * LANES] = (yn * cos + partner * sin).astype(o_ref.dtype)


def _attn_kernel(lam_ref, q_ref, kc_ref, vc_ref, *rest, has_lat, tk, n_chunks, out_scale):
    if has_lat:
        kl_ref, vl_ref, sg_ref, o_ref = rest
    else:
        sg_ref, o_ref = rest
    q = q_ref[0]
    tq = q.shape[0]
    lane = lax.broadcasted_iota(I32, (1, LANES), 1)
    zero = jnp.zeros_like(q)
    q2 = jnp.concatenate([jnp.where(lane < DA_HEAD_DIM, q, zero),
                          jnp.where(lane >= DA_HEAD_DIM, q, zero)], axis=0)

    def step(k, v, carry):
        m, l, acc = carry
        s = lax.dot_general(q2, k, (((1,), (1,)), ((), ())), preferred_element_type=F32)
        m_new = jnp.maximum(m, jnp.max(s, axis=-1, keepdims=True))
        alpha = jnp.exp2(m - m_new)
        p = jnp.exp2(s - m_new)
        l_new = alpha * l + jnp.sum(p, axis=-1, keepdims=True)
        acc_new = alpha * acc + _dot(p.astype(BF16), v)
        return m_new, l_new, acc_new

    carry = (jnp.full((2 * tq, 1), -1e30, F32), jnp.zeros((2 * tq, 1), F32),
             jnp.zeros((2 * tq, LANES), F32))
    carry = step(kc_ref[0], vc_ref[0], carry)
    if has_lat:
        def body(c, carry):
            off = pl.multiple_of(c * tk, tk)
            return step(kl_ref[0, pl.ds(off, tk), :], vl_ref[0, pl.ds(off, tk), :], carry)
        carry = lax.fori_loop(0, n_chunks, body, carry)
    _, l, acc = carry
    o = acc / l
    o = o[:tq] - lam_ref[0] * o[tq:]
    ms = jnp.mean(o * o, axis=-1, keepdims=True)
    o_ref[0] = (o * lax.rsqrt(ms + EPS) * sg_ref[...] * out_scale).astype(o_ref.dtype)


def diff_attention_core(lam, qkv_q, qkv_c, qkv_l, sub_g, out_scale):
    B, nq, D3 = qkv_q.shape
    D = D3 // 3
    H = D // DA_V_DIM
    nc = qkv_c.shape[1]
    tq = min(256, nq)
    has_lat = qkv_l is not None
    in_specs = [pl.BlockSpec(memory_space=pltpu.SMEM),
                pl.BlockSpec((1, tq, LANES), lambda b, h, i: (b, i, h)),
                pl.BlockSpec((1, nc, LANES), lambda b, h, i: (b, 0, H + h)),
                pl.BlockSpec((1, nc, LANES), lambda b, h, i: (b, 0, 2 * H + h))]
    args = [lam, qkv_q, qkv_c, qkv_c]
    tk, n_chunks = 0, 0
    if has_lat:
        nl = qkv_l.shape[1]
        tk = min(512, nl)
        n_chunks = nl // tk
        in_specs += [pl.BlockSpec((1, nl, LANES), lambda b, h, i: (b, 0, H + h)),
                     pl.BlockSpec((1, nl, LANES), lambda b, h, i: (b, 0, 2 * H + h))]
        args += [qkv_l, qkv_l]
    in_specs.append(pl.BlockSpec((1, LANES), lambda b, h, i: (0, 0)))
    args.append(sub_g.reshape(1, LANES))
    kern = functools.partial(_attn_kernel, has_lat=has_lat, tk=tk, n_chunks=n_chunks,
                             out_scale=out_scale)
    return pl.pallas_call(
        kern,
        grid=(B, H, nq // tq),
        in_specs=in_specs,
        out_specs=pl.BlockSpec((1, tq, LANES), lambda b, h, i: (b, i, h)),
        out_shape=jax.ShapeDtypeStruct((B, nq, D), BF16),
        compiler_params=_cparams("parallel", "parallel", "arbitrary"),
        name="diff_attention",
    )(*args)


def _axial_rope_tables(n):
    f = DA_HEAD_DIM // 4
    inv = 1.0 / (ROPE_BASE ** (jnp.arange(f, dtype=F32) / f))
    rows = n // GRID_W
    row = jnp.repeat(jnp.arange(rows), GRID_W).astype(F32)
    col = jnp.tile(jnp.arange(GRID_W), rows).astype(F32)
    ang = jnp.stack([row[:, None] * inv, col[:, None] * inv], axis=1)
    cos, sin = jnp.cos(ang), jnp.sin(ang)
    cos64 = jnp.concatenate([cos, cos], axis=-1).reshape(n, DA_HEAD_DIM)
    sin64 = jnp.concatenate([-sin, sin], axis=-1).reshape(n, DA_HEAD_DIM)
    return jnp.tile(cos64, (1, 2)), jnp.tile(sin64, (1, 2))


def da_project(h_in, g, sh, sc, w_in_bf, q_g, k_g, cos, sin):
    B, n, D = h_in.shape
    tn = 512
    tm = min(512, n)
    gm = (np.arange(LANES)[:, None] // DA_HEAD_DIM == np.arange(LANES)[None, :] // DA_HEAD_DIM)
    gm = jnp.asarray(gm, BF16)
    qg = (jnp.tile(q_g.astype(F32), 2) * (DA_HEAD_DIM ** -0.5 * math.log2(math.e))).reshape(1, LANES)
    kg = jnp.tile(k_g.astype(F32), 2).reshape(1, LANES)
    extra = (cos, sin, qg, kg, gm)
    extra_specs = (pl.BlockSpec((tm, LANES), lambda b, i, j: (i, 0)),
                   pl.BlockSpec((tm, LANES), lambda b, i, j: (i, 0)),
                   pl.BlockSpec((1, LANES), lambda b, i, j: (0, 0)),
                   pl.BlockSpec((1, LANES), lambda b, i, j: (0, 0)),
                   pl.BlockSpec((LANES, LANES), lambda b, i, j: (0, 0)))
    epi = functools.partial(_da_epilogue, tn=tn, n_qk_tiles=2 * D // tn)
    return mod_proj(h_in, g, sh, sc, w_in_bf, jnp.zeros((3 * D,), F32), epilogue=epi, extra=extra,
                    extra_specs=extra_specs, tn=tn, name="da_project")


def diff_attention_layer(x, ctx, mv, norm_g, w_in, q_g, k_g, lam, sub_g, w_out, lambda_init, need_ctx):
    B, L, D = x.shape
    nc = ctx.shape[1]
    w_in_bf = w_in.astype(BF16)
    w_out_bf = w_out.astype(BF16)
    cos_l, sin_l = _axial_rope_tables(L)
    cos_c, sin_c = jnp.ones((nc, LANES), F32), jnp.zeros((nc, LANES), F32)
    qkv_l = da_project(x, norm_g, mv["sh_l"], mv["sc_l"], w_in_bf, q_g, k_g, cos_l, sin_l)
    qkv_c = da_project(ctx, norm_g, mv["sh_c"], mv["sc_c"], w_in_bf, q_g, k_g, cos_c, sin_c)
    lamf = lam.astype(F32)
    lam_val = (jnp.exp(jnp.sum(lamf[0] * lamf[1])) - jnp.exp(jnp.sum(lamf[2] * lamf[3]))
               + lambda_init).reshape(1)
    out_scale = 1.0 - lambda_init
    o_l = diff_attention_core(lam_val, qkv_l, qkv_c, qkv_l, sub_g, out_scale)
    zb = jnp.zeros((D,), F32)
    x = res_matmul(o_l, w_out_bf, zb, x, mv["g_l"])
    if need_ctx:
        o_c = diff_attention_core(lam_val, qkv_c, qkv_c, None, sub_g, out_scale)
        ctx = res_matmul(o_c, w_out_bf, zb, ctx, mv["g_c"])
    return x, ctx


def _router_kernel(x_ref, g_ref, sh_ref, sc_ref, whi_ref, wlo_ref, o_ref):
    h = _modulate(x_ref[0], g_ref[...], sh_ref[0], sc_ref[0])
    o_ref[0] = _router_affinity(h, whi_ref[...], wlo_ref[...])


def _router_affinity(h, w_hi, w_lo):
    h_hi, h_lo = _split_bf16(h)
    logits = _dot3(h_hi, h_lo, w_hi, w_lo)
    lane = lax.broadcasted_iota(I32, (1, LANES), 1)
    logits = jnp.where(lane < N_EXPERTS, logits, -1e30)
    z = jnp.exp(logits - jnp.max(logits, axis=-1, keepdims=True))
    return z / jnp.sum(z, axis=-1, keepdims=True)


def router_affinity(x, g, sh, sc, w_hi, w_lo):
    B, n, D = x.shape
    tm = min(512, n)
    return pl.pallas_call(
        _router_kernel,
        grid=(B, n // tm),
        in_specs=[pl.BlockSpec((1, tm, D), lambda b, i: (b, i, 0)),
                  pl.BlockSpec((1, D), lambda b, i: (0, 0)),
                  pl.BlockSpec((1, 1, D), lambda b, i: (b, 0, 0)),
                  pl.BlockSpec((1, 1, D), lambda b, i: (b, 0, 0)),
                  pl.BlockSpec((D, LANES), lambda b, i: (0, 0)),
                  pl.BlockSpec((D, LANES), lambda b, i: (0, 0))],
        out_specs=pl.BlockSpec((1, tm, LANES), lambda b, i: (b, i, 0)),
        out_shape=jax.ShapeDtypeStruct((B, n, LANES), F32),
        compiler_params=_cparams("parallel", "parallel"),
        name="router",
    )(x, g.reshape(1, D), sh, sc, w_hi, w_lo)


def _ffn_kernel(idx_ref, x_hbm, acc_in, g_ref, sh_ref, sc_ref, g2_ref, rhi_ref, rlo_ref,
                w1_ref, w3_ref, w2_ref, o_hbm, xbuf, abuf, sem, *, n, cap, tm):
    del acc_in
    b, e, j = pl.program_id(0), pl.program_id(1), pl.program_id(2)
    n_exp = pl.num_programs(1)
    base = (b * n_exp + e) * cap + j * tm
    row0 = b * n

    def gather(k, c):
        row = row0 + idx_ref[base + k]
        pltpu.make_async_copy(x_hbm.at[pl.ds(row, 1)], xbuf.at[pl.ds(k, 1)], sem.at[0]).start()
        pltpu.make_async_copy(o_hbm.at[pl.ds(row, 1)], abuf.at[pl.ds(k, 1)], sem.at[1]).start()
        return c

    lax.fori_loop(0, tm, gather, 0)
    pltpu.make_async_copy(x_hbm.at[pl.ds(0, tm)], xbuf, sem.at[0]).wait()

    h = _modulate(xbuf[...], g_ref[...], sh_ref[0], sc_ref[0])
    aff = _router_affinity(h, rhi_ref[...], rlo_ref[...])
    lane = lax.broadcasted_iota(I32, (1, LANES), 1)
    gate = jnp.sum(jnp.where(lane == e, aff, 0.0), axis=-1, keepdims=True)
    hb = h.astype(BF16)
    a1 = _dot(hb, w1_ref[0])
    a3 = _dot(hb, w3_ref[0])
    hid = (a1 * jax.nn.sigmoid(a1) * a3).astype(BF16)
    y = _dot(hid, w2_ref[0])
    upd = g2_ref[0] * (gate * y)

    pltpu.make_async_copy(o_hbm.at[pl.ds(0, tm)], abuf, sem.at[1]).wait()
    abuf[...] = abuf[...] + upd

    def scatter(k, c):
        row = row0 + idx_ref[base + k]
        pltpu.make_async_copy(abuf.at[pl.ds(k, 1)], o_hbm.at[pl.ds(row, 1)], sem.at[2]).start()
        return c

    lax.fori_loop(0, tm, scatter, 0)
    pltpu.make_async_copy(abuf, o_hbm.at[pl.ds(0, tm)], sem.at[2]).wait()


def expert_ffn(idx, x, g, sh, sc, g2, r_hi, r_lo, w1, w3, w2):
    B, n, D = x.shape
    E, cap = idx.shape[1], idx.shape[2]
    F = w1.shape[-1]
    tm = min(256, cap)
    x2 = x.reshape(B * n, D)
    kern = functools.partial(_ffn_kernel, n=n, cap=cap, tm=tm)
    vec = lambda: pl.BlockSpec((1, 1, D), lambda b, e, j, idx: (b, 0, 0))
    grid_spec = pltpu.PrefetchScalarGridSpec(
        num_scalar_prefetch=1,
        grid=(B, E, cap // tm),
        in_specs=[pl.BlockSpec(memory_space=pl.ANY),
                  pl.BlockSpec(memory_space=pl.ANY),
                  pl.BlockSpec((1, D), lambda b, e, j, idx: (0, 0)),
                  vec(), vec(), vec(),
                  pl.BlockSpec((D, LANES), lambda b, e, j, idx: (0, 0)),
                  pl.BlockSpec((D, LANES), lambda b, e, j, idx: (0, 0)),
                  pl.BlockSpec((1, D, F), lambda b, e, j, idx: (e, 0, 0)),
                  pl.BlockSpec((1, D, F), lambda b, e, j, idx: (e, 0, 0)),
                  pl.BlockSpec((1, F, D), lambda b, e, j, idx: (e, 0, 0))],
        out_specs=pl.BlockSpec(memory_space=pl.ANY),
        scratch_shapes=[pltpu.VMEM((tm, D), F32), pltpu.VMEM((tm, D), F32),
                        pltpu.SemaphoreType.DMA((3,))])
    out = pl.pallas_call(
        kern,
        grid_spec=grid_spec,
        out_shape=jax.ShapeDtypeStruct((B * n, D), F32),
        input_output_aliases={2: 0},
        compiler_params=_cparams("arbitrary", "arbitrary", "arbitrary"),
        name="expert_ffn",
    )(idx.reshape(-1), x2, x2, g.reshape(1, D), sh, sc, g2, r_hi, r_lo, w1, w3, w2)
    return out.reshape(B, n, D)


TOPK_ROWS = 128


def _topk_kernel(a_ref, idx_ref, *, k, capp):
    a = a_ref[0, 0]
    bits = lax.bitcast_convert_type(a, I32)
    row_i = lax.broadcasted_iota(I32, (TOPK_ROWS, LANES), 0)
    lane_i = lax.broadcasted_iota(I32, (TOPK_ROWS, LANES), 1)
    tok = row_i * LANES + lane_i

    def count(ind):
        return jnp.sum(jnp.sum(ind, axis=1, keepdims=True), axis=0, keepdims=True)

    thr = jnp.zeros((1, 1), I32)
    for bit in range(30, -1, -1):
        cand = thr | (1 << bit)
        thr = jnp.where(count(jnp.where(bits >= cand, 1, 0)) >= k, cand, thr)
    gt = jnp.where(bits > thr, 1, 0)
    eq = jnp.where(bits == thr, 1, 0)
    need = k - count(gt)
    tcut = jnp.zeros((1, 1), I32)
    for bit in range((TOPK_ROWS * LANES).bit_length() - 2, -1, -1):
        cand = tcut | (1 << bit)
        tcut = jnp.where(count(jnp.where(tok < cand, eq, 0)) < need, cand, tcut)
    sel = gt + jnp.where(tok <= tcut, eq, 0)

    m = sel.astype(F32)
    cnt = jnp.sum(m, axis=1, keepdims=True)
    tri = jnp.where(lax.broadcasted_iota(I32, (TOPK_ROWS, TOPK_ROWS), 0)
                    >= lax.broadcasted_iota(I32, (TOPK_ROWS, TOPK_ROWS), 1), 1.0, 0.0).astype(BF16)
    incl = _dot(tri, jnp.broadcast_to(cnt, (TOPK_ROWS, LANES)).astype(BF16))[:, :1]
    slot = lax.broadcasted_iota(I32, (1, capp), 1).astype(F32)
    before = incl <= slot
    row_of = jnp.sum(jnp.where(before, 1.0, 0.0), axis=0, keepdims=True)
    base = jnp.sum(jnp.where(before, cnt, 0.0), axis=0, keepdims=True)
    rows_col = lax.broadcasted_iota(I32, (TOPK_ROWS, 1), 0).astype(F32)
    onehot = jnp.where(rows_col == row_of, 1.0, 0.0).astype(BF16)
    picked = _dot(m.T.astype(BF16), onehot)
    cum = _dot(tri, picked.astype(BF16))
    local = jnp.sum(jnp.where(cum <= slot - base, 1.0, 0.0), axis=0, keepdims=True)
    idx_ref[0, 0] = (row_of * LANES + local).astype(I32)


def expert_choice_topk(aff, cap):
    B, E, n = aff.shape
    n_pad = TOPK_ROWS * LANES
    assert n <= n_pad and TOPK_ROWS == LANES
    a = jnp.pad(aff, ((0, 0), (0, 0), (0, n_pad - n)), constant_values=-1.0)
    a = a.reshape(B, E, TOPK_ROWS, LANES)
    capp = max(cap, LANES)
    out = pl.pallas_call(
        functools.partial(_topk_kernel, k=cap, capp=capp),
        grid=(B, E),
        in_specs=[pl.BlockSpec((1, 1, TOPK_ROWS, LANES), lambda b, e: (b, e, 0, 0))],
        out_specs=pl.BlockSpec((1, 1, 1, capp), lambda b, e: (b, e, 0, 0)),
        out_shape=jax.ShapeDtypeStruct((B, E, 1, capp), I32),
        compiler_params=_cparams("parallel", "parallel"),
        name="expert_topk",
    )(a)
    return out[:, :, 0, :cap]


def moe_layer(x, g, sh, sc, g2, router_w, w1, w3, w2):
    B, n, D = x.shape
    E = router_w.shape[1]
    cap = EC_CAPACITY_FACTOR * n // E
    rw = jnp.pad(router_w.astype(F32), ((0, 0), (0, LANES - E)))
    r_hi, r_lo = _split_bf16(rw)
    aff = router_affinity(x, g, sh, sc, r_hi, r_lo)
    idx = expert_choice_topk(aff[:, :, :E].transpose(0, 2, 1), cap)
    return expert_ffn(idx, x, g, sh, sc, g2, r_hi, r_lo, w1, w3, w2)


def _ret_epilogue(acc, j, o_ref, cos_ref, sin_ref, *, tn, n_q_tiles):
    @pl.when(j >= 2 * n_q_tiles)
    def _():
        o_ref[0] = acc.astype(o_ref.dtype)

    @pl.when(j < 2 * n_q_tiles)
    def _():
        scale = jnp.where(j < n_q_tiles, 1.0, RET_QK_DIM ** -0.5)
        cos, sin = cos_ref[...] * scale, sin_ref[...] * scale
        f = RET_QK_DIM // 2
        for s in range(tn // RET_QK_DIM):
            lo, mid, hi = s * RET_QK_DIM, s * RET_QK_DIM + f, (s + 1) * RET_QK_DIM
            x1, x2 = acc[:, lo:mid], acc[:, mid:hi]
            o_ref[0, :, lo:mid] = (x1 * cos - x2 * sin).astype(o_ref.dtype)
            o_ref[0, :, mid:hi] = (x2 * cos + x1 * sin).astype(o_ref.dtype)


def _retention_kernel(lg_ref, qf_ref, kf_ref, vf_ref, qb_ref, kb_ref, vb_ref, s0f_ref, s0b_ref,
                      of_ref, ob_ref, sf_ref, sb_ref, stf, stb, *, T):
    h, c = pl.program_id(1), pl.program_id(2)

    @pl.when(c == 0)
    def _():
        stf[...] = s0f_ref[0, 0]
        stb[...] = s0b_ref[0, 0]

    ii = lax.broadcasted_iota(I32, (T, T), 0).astype(F32)
    jj = lax.broadcasted_iota(I32, (T, T), 1).astype(F32)
    row = lax.broadcasted_iota(I32, (T, 1), 0).astype(F32)

    def one(q_ref, k_ref, v_ref, st, o_ref, lg, forward):
        q, k, v = q_ref[0], k_ref[0], v_ref[0]
        d = ii - jj if forward else jj - ii
        mask = d >= 0 if forward else d > 0
        dmask = jnp.where(mask, jnp.exp(jnp.where(mask, d, 0.0) * lg), 0.0)
        xi = jnp.exp(((row + 1.0) if forward else (T - row)) * lg)
        zeta = jnp.exp(((T - 1.0 - row) if forward else row) * lg)
        inner = lax.dot_general(q, k, (((1,), (1,)), ((), ())), preferred_element_type=F32) * dmask
        o = _dot(inner.astype(BF16), v) + _dot(q, st[...].astype(BF16)) * xi
        o_ref[0] = o.astype(o_ref.dtype)
        kz = (k.astype(F32) * zeta).astype(BF16)
        decay = jnp.exp(jnp.full((1, 1), T, F32) * lg)
        st[...] = st[...] * decay + lax.dot_general(kz, v, (((0,), (0,)), ((), ())),
                                                    preferred_element_type=F32)

    one(qf_ref, kf_ref, vf_ref, stf, of_ref, lg_ref[0, h], True)
    one(qb_ref, kb_ref, vb_ref, stb, ob_ref, lg_ref[1, h], False)

    @pl.when(c == pl.num_programs(2) - 1)
    def _():
        sf_ref[0, 0] = stf[...]
        sb_ref[0, 0] = stb[...]


def retention_core(log_g, p, s0f, s0b):
    B, n, _ = p.shape
    H, dk, dv = s0f.shape[1], s0f.shape[2], s0f.shape[3]
    T = min(256, n)
    nc = n // T
    kq, kk, kv = 0, (H * dk) // dk, (2 * H * dk) // dv
    fwd = lambda off: (lambda b, h, c: (b, c, off + h))
    bwd = lambda off: (lambda b, h, c: (b, nc - 1 - c, off + h))
    st_spec = pl.BlockSpec((1, 1, dk, dv), lambda b, h, c: (b, h, 0, 0))
    out_sd = jax.ShapeDtypeStruct((B, n, H * dv), BF16)
    st_sd = jax.ShapeDtypeStruct((B, H, dk, dv), F32)
    return pl.pallas_call(
        functools.partial(_retention_kernel, T=T),
        grid=(B, H, nc),
        in_specs=[pl.BlockSpec(memory_space=pltpu.SMEM),
                  pl.BlockSpec((1, T, dk), fwd(kq)), pl.BlockSpec((1, T, dk), fwd(kk)),
                  pl.BlockSpec((1, T, dv), fwd(kv)),
                  pl.BlockSpec((1, T, dk), bwd(kq)), pl.BlockSpec((1, T, dk), bwd(kk)),
                  pl.BlockSpec((1, T, dv), bwd(kv)),
                  st_spec, st_spec],
        out_specs=[pl.BlockSpec((1, T, dv), fwd(0)), pl.BlockSpec((1, T, dv), bwd(0)), st_spec, st_spec],
        out_shape=[out_sd, out_sd, st_sd, st_sd],
        scratch_shapes=[pltpu.VMEM((dk, dv), F32), pltpu.VMEM((dk, dv), F32)],
        compiler_params=_cparams("parallel", "parallel", "arbitrary"),
        name="retention",
    )(log_g, p, p, p, p, p, p, s0f, s0b)


def _ret_finish_kernel(of_ref, ob_ref, g_ref, gn_ref, z_ref, *, dv):
    for s in range(of_ref.shape[-1] // dv):
        sl = slice(s * dv, (s + 1) * dv)
        y = of_ref[0, :, sl].astype(F32) + ob_ref[0, :, sl].astype(F32)
        ms = jnp.mean(y * y, axis=-1, keepdims=True)
        y = y * lax.rsqrt(ms + EPS) * gn_ref[...]
        g = g_ref[0, :, sl].astype(F32)
        z_ref[0, :, sl] = (g * jax.nn.sigmoid(g) * y).astype(z_ref.dtype)


def retention_finish(of, ob, p, gn_g):
    B, n, N = of.shape
    dv = gn_g.shape[0]
    tm = min(512, n)
    blk = lambda off: pl.BlockSpec((1, tm, N), lambda b, i: (b, i, off))
    return pl.pallas_call(
        functools.partial(_ret_finish_kernel, dv=dv),
        grid=(B, n // tm),
        in_specs=[blk(0), blk(0), blk(p.shape[-1] // N - 1), pl.BlockSpec((1, dv), lambda b, i: (0, 0))],
        out_specs=blk(0),
        out_shape=jax.ShapeDtypeStruct((B, n, N), BF16),
        compiler_params=_cparams("parallel", "parallel"),
        name="retention_finish",
    )(of, ob, p, gn_g.astype(F32).reshape(1, dv))


def retention_layer(x, ctx, mv, norm_g, w_in, log_alpha, gn_g, w_out, need_ctx):
    B, L, D = x.shape
    nc = ctx.shape[1]
    dk, dv = RET_QK_DIM, RET_V_DIM
    H = D // dk
    w_in_bf = w_in.astype(BF16)
    w_out_bf = w_out.astype(BF16)
    zb = jnp.zeros((w_in.shape[1],), F32)
    log_g = jnp.log1p(-jnp.exp(log_alpha.astype(F32)))
    f = dk // 2
    inv = 1.0 / (ROPE_BASE ** jnp.linspace(0.0, 1.0, f, dtype=F32))
    ang = jnp.arange(L, dtype=F32)[:, None] * inv
    tn = 512

    def project(h_in, sh, sc, cos, sin):
        tm = min(512, h_in.shape[1])
        specs = (pl.BlockSpec((tm, f), lambda b, i, j: (i, 0)), pl.BlockSpec((tm, f), lambda b, i, j: (i, 0)))
        epi = functools.partial(_ret_epilogue, tn=tn, n_q_tiles=H * dk // tn)
        return mod_proj(h_in, norm_g, sh, sc, w_in_bf, zb, epilogue=epi, extra=(cos, sin),
                        extra_specs=specs, tn=tn, name="ret_project")

    p_c = project(ctx, mv["sh_c"], mv["sc_c"], jnp.ones((nc, f), F32), jnp.zeros((nc, f), F32))
    p_l = project(x, mv["sh_l"], mv["sc_l"], jnp.cos(ang), jnp.sin(ang))
    zero = jnp.zeros((B, H, dk, dv), F32)
    of_c, ob_c, s_f, s_b = retention_core(log_g, p_c, zero, zero)
    of_l, ob_l, _, _ = retention_core(log_g, p_l, s_f, s_b)
    zd = jnp.zeros((D,), F32)
    x = res_matmul(retention_finish(of_l, ob_l, p_l, gn_g), w_out_bf, zd, x, mv["g_l"])
    if need_ctx:
        ctx = res_matmul(retention_finish(of_c, ob_c, p_c, gn_g), w_out_bf, zd, ctx, mv["g_c"])
    return x, ctx


def _hyena_filter_spectra(n, w1, b1, w2, b2, freq, w3):
    t = jnp.linspace(0.0, 1.0, n, dtype=F32)[:, None]
    w = 2.0 * math.pi * jnp.arange(n, dtype=F32)[:, None] / n
    f = jnp.linspace(1e-4, HY_BANDS - 1, HY_BANDS, dtype=F32)[None]
    z = jnp.concatenate([t, jnp.cos(f * w), -jnp.sin(f * w)], axis=-1)
    hp = lax.Precision.HIGHEST
    a = jnp.sin(freq[0].astype(F32) * (jnp.dot(z, w1.astype(F32), precision=hp) + b1.astype(F32)))
    a = jnp.sin(freq[1].astype(F32) * (jnp.dot(a, w2.astype(F32), precision=hp) + b2.astype(F32)))
    C = w3.shape[-1] // (2 * HY_ORDER)
    h = jnp.dot(a, w3.astype(F32), precision=hp).reshape(n, HY_ORDER, 2, C)
    max_decay = math.log(HY_TARGET) / HY_FAST_DECAY_PCT
    min_decay = math.log(HY_TARGET) / HY_SLOW_DECAY_PCT
    deltas = jnp.abs(jnp.linspace(min_decay, max_decay, C, dtype=F32))
    h = h * jnp.exp(-t * deltas)[:, None, None, :]
    buf = jnp.concatenate([h[:, :, 0], jnp.zeros((1, HY_ORDER, C), F32),
                           jnp.flip(h[1:, :, 1], axis=0)], axis=0)
    buf = buf / jnp.sum(jnp.abs(buf), axis=0, keepdims=True)
    return jnp.fft.rfft(buf, axis=0)


def _fft_long_conv(u, hf, bias):
    n = u.shape[1]
    y = jnp.fft.irfft(jnp.fft.rfft(u, n=2 * n, axis=1) * hf[None], n=2 * n, axis=1)[:, :n]
    return y + u * bias.astype(F32)


def hyena_layer(x, ctx, mv, norm_g, hy_p, need_ctx):
    (w_in, b_in, conv_w, conv_b, w1, b1, w2, b2, freq, w3, fbias, w_out, b_out) = hy_p
    w_in_bf = w_in.astype(BF16)
    w_out_bf = w_out.astype(BF16)

    def mix(h_in, sh, sc, res, gate):
        n = h_in.shape[1]
        u = mod_proj(h_in, norm_g, sh, sc, w_in_bf, b_in.astype(F32), out_dtype=F32, name="hy_project")
        up = jnp.pad(u, ((0, 0), (1, 1), (0, 0)))
        u = up[:, :n] * conv_w[0] + up[:, 1:n + 1] * conv_w[1] + up[:, 2:] * conv_w[2] + conv_b
        v, x1, x2 = jnp.split(u, 3, axis=-1)
        hf = _hyena_filter_spectra(n, w1, b1, w2, b2, freq, w3)
        z = x1 * _fft_long_conv(v, hf[:, 0], fbias[0])
        z = x2 * _fft_long_conv(z, hf[:, 1], fbias[1])
        return res_matmul(z.astype(BF16), w_out_bf, b_out.astype(F32), res, gate)

    x = mix(x, mv["sh_l"], mv["sc_l"], x, mv["g_l"])
    if need_ctx:
        ctx = mix(ctx, mv["sh_c"], mv["sc_c"], ctx, mv["g_c"])
    return x, ctx


def kernel(x, c, ctx, c_ctx, norm_mix_g, norm_ffn_g, mod_w, mod_b,
           da_w_in, da_q_norm, da_k_norm, da_lambda, da_sub_norm, da_w_out,
           ret_w_in, ret_log_alpha, ret_group_norm, ret_w_out,
           hy_w_in, hy_b_in, hy_conv_w, hy_conv_b, hy_ffn_w1, hy_ffn_b1, hy_ffn_w2, hy_ffn_b2,
           hy_sin_freq, hy_ffn_w3, hy_filter_bias, hy_w_out, hy_b_out,
           router_w, exp_w1, exp_w3, exp_w2):
    B = x.shape[0]
    depth = mod_w.shape[0]
    mods = _mod_all(c, c_ctx, mod_w, mod_b, B)
    for i in range(depth):
        last = i == depth - 1
        m, s = i % N_MIXERS, i // N_MIXERS
        mv = mods[i]
        if m == 0:
            lambda_init = 0.8 - 0.6 * math.exp(-0.3 * i)
            x, ctx = diff_attention_layer(x, ctx, mv, norm_mix_g[i], da_w_in[s], da_q_norm[s], da_k_norm[s],
                                          da_lambda[s], da_sub_norm[s], da_w_out[s], lambda_init, not last)
        elif m == 1:
            x, ctx = retention_layer(x, ctx, mv, norm_mix_g[i], ret_w_in[s], ret_log_alpha[s],
                                     ret_group_norm[s], ret_w_out[s], not last)
        else:
            hy_p = (hy_w_in[s], hy_b_in[s], hy_conv_w[s], hy_conv_b[s], hy_ffn_w1[s], hy_ffn_b1[s],
                    hy_ffn_w2[s], hy_ffn_b2[s], hy_sin_freq[s], hy_ffn_w3[s], hy_filter_bias[s],
                    hy_w_out[s], hy_b_out[s])
            x, ctx = hyena_layer(x, ctx, mv, norm_mix_g[i], hy_p, not last)
        w1, w3, w2 = exp_w1[i].astype(BF16), exp_w3[i].astype(BF16), exp_w2[i].astype(BF16)
        x = moe_layer(x, norm_ffn_g[i], mv["sh2_l"], mv["sc2_l"], mv["g2_l"], router_w[i], w1, w3, w2)
        if not last:
            ctx = moe_layer(ctx, norm_ffn_g[i], mv["sh2_c"], mv["sc2_c"], mv["g2_c"], router_w[i], w1, w3, w2)
    return x
```

```python
import functools
import math

import jax
import jax.numpy as jnp
import numpy as np
from jax import lax
from jax.experimental import pallas as pl
from jax.experimental.pallas import tpu as pltpu

F32 = jnp.float32
BF16 = jnp.bfloat16
I32 = jnp.int32

EPS = 1e-6
N_MOD = 6
ROPE_BASE = 10000.0
GRID_W = 64
N_MIXERS = 3

DA_HEAD_DIM = 64
DA_V_DIM = 128
RET_QK_DIM = 256
RET_V_DIM = 512
RET_CHUNK = 128

HY_ORDER = 2
HY_EMB = 33
HY_BANDS = (HY_EMB - 1) // 2
HY_TARGET = 1e-2
HY_FAST_DECAY_PCT = 0.3
HY_SLOW_DECAY_PCT = 1.5

N_EXPERTS = 16
EC_CAPACITY_FACTOR = 2

LANES = 128
VMEM_LIMIT_BYTES = 56 * 1024 * 1024


def _cparams(*sem):
    return pltpu.CompilerParams(dimension_semantics=sem, vmem_limit_bytes=VMEM_LIMIT_BYTES)


def _split_bf16(a):
    hi = a.astype(BF16)
    lo = (a - hi.astype(F32)).astype(BF16)
    return hi, lo


def _dot(a, b):
    return jnp.dot(a, b, preferred_element_type=F32)


def _dot3(a_hi, a_lo, b_hi, b_lo):
    return _dot(a_hi, b_hi) + _dot(a_lo, b_hi) + _dot(a_hi, b_lo)


def _modulate(x, g, sh, sc):
    ms = jnp.mean(x * x, axis=-1, keepdims=True)
    y = x * lax.rsqrt(ms + EPS) * g
    return y * (1.0 + sc) + sh


def _modvec_kernel(a_ref, w_ref, b_ref, o_ref):
    a = a_ref[...]
    a_hi, a_lo = _split_bf16(a * jax.nn.sigmoid(a))
    w_hi, w_lo = _split_bf16(w_ref[0])
    o_ref[0] = _dot3(a_hi, a_lo, w_hi, w_lo) + b_ref[0]


def mod_vectors(a, mod_w, mod_b):
    depth, d, n = mod_w.shape
    tn = 1024
    return pl.pallas_call(
        _modvec_kernel,
        grid=(depth, n // tn),
        in_specs=[pl.BlockSpec((8, d), lambda i, j: (0, 0)),
                  pl.BlockSpec((1, d, tn), lambda i, j: (i, 0, j)),
                  pl.BlockSpec((1, 1, tn), lambda i, j: (i, 0, j))],
        out_specs=pl.BlockSpec((1, 8, tn), lambda i, j: (i, 0, j)),
        out_shape=jax.ShapeDtypeStruct((depth, 8, n), F32),
        compiler_params=_cparams("parallel", "parallel"),
        name="mod_vectors",
    )(a, mod_w, mod_b.reshape(depth, 1, n))


def _mod_all(c, c_ctx, mod_w, mod_b, batch):
    d = c.shape[-1]
    a = jnp.zeros((8, d), F32).at[:batch].set(c).at[batch].set(c_ctx)
    mv = mod_vectors(a, mod_w, mod_b)
    names = ("sh", "sc", "g", "sh2", "sc2", "g2")
    layers = []
    for i in range(mod_w.shape[0]):
        parts = mv[i].reshape(8, N_MOD, d)
        layer = {}
        for k, nm in enumerate(names):
            layer[nm + "_l"] = parts[:batch, k][:, None, :]
            layer[nm + "_c"] = jnp.broadcast_to(parts[batch, k][None, None, :], (batch, 1, d))
        layers.append(layer)
    return layers


def _proj_kernel(x_ref, g_ref, sh_ref, sc_ref, w_ref, b_ref, *rest, epilogue, n_extra):
    extra, o_ref, h_scr = rest[:n_extra], rest[n_extra], rest[n_extra + 1]
    j = pl.program_id(2)

    @pl.when(j == 0)
    def _():
        h_scr[...] = _modulate(x_ref[0], g_ref[...], sh_ref[0], sc_ref[0]).astype(BF16)

    acc = _dot(h_scr[...], w_ref[...]) + b_ref[...]
    epilogue(acc, j, o_ref, *extra)


def _plain_epilogue(acc, j, o_ref):
    o_ref[0] = acc.astype(o_ref.dtype)


def mod_proj(x, g, sh, sc, w, bias, *, epilogue=_plain_epilogue, extra=(), extra_specs=(),
             out_dtype=BF16, tn=512, name="mod_proj"):
    B, n, K = x.shape
    N = w.shape[1]
    tm = min(512, n)
    tn = min(tn, N)
    kern = functools.partial(_proj_kernel, epilogue=epilogue, n_extra=len(extra))
    return pl.pallas_call(
        kern,
        grid=(B, n // tm, N // tn),
        in_specs=[pl.BlockSpec((1, tm, K), lambda b, i, j: (b, i, 0)),
                  pl.BlockSpec((1, K), lambda b, i, j: (0, 0)),
                  pl.BlockSpec((1, 1, K), lambda b, i, j: (b, 0, 0)),
                  pl.BlockSpec((1, 1, K), lambda b, i, j: (b, 0, 0)),
                  pl.BlockSpec((K, tn), lambda b, i, j: (0, j)),
                  pl.BlockSpec((1, tn), lambda b, i, j: (0, j)),
                  *extra_specs],
        out_specs=pl.BlockSpec((1, tm, tn), lambda b, i, j: (b, i, j)),
        out_shape=jax.ShapeDtypeStruct((B, n, N), out_dtype),
        scratch_shapes=[pltpu.VMEM((tm, K), BF16)],
        compiler_params=_cparams("parallel", "parallel", "arbitrary"),
        name=name,
    )(x, g.reshape(1, K), sh, sc, w, bias.reshape(1, N), *extra)


def _resmm_kernel(a_ref, w_ref, b_ref, r_ref, gt_ref, o_ref):
    acc = _dot(a_ref[0], w_ref[...]) + b_ref[...]
    o_ref[0] = r_ref[0] + gt_ref[0] * acc


def res_matmul(a, w, bias, res, gate):
    B, n, K = a.shape
    N = w.shape[1]
    tm = min(512, n)
    tn = min(512, N)
    return pl.pallas_call(
        _resmm_kernel,
        grid=(B, n // tm, N // tn),
        in_specs=[pl.BlockSpec((1, tm, K), lambda b, i, j: (b, i, 0)),
                  pl.BlockSpec((K, tn), lambda b, i, j: (0, j)),
                  pl.BlockSpec((1, tn), lambda b, i, j: (0, j)),
                  pl.BlockSpec((1, tm, tn), lambda b, i, j: (b, i, j)),
                  pl.BlockSpec((1, 1, tn), lambda b, i, j: (b, 0, j))],
        out_specs=pl.BlockSpec((1, tm, tn), lambda b, i, j: (b, i, j)),
        out_shape=jax.ShapeDtypeStruct((B, n, N), F32),
        compiler_params=_cparams("parallel", "parallel", "parallel"),
        name="res_matmul",
    )(a, w, bias.reshape(1, N), res, gate)


def _da_epilogue(acc, j, o_ref, cos_ref, sin_ref, qg_ref, kg_ref, gm_ref, *, tn, n_qk_tiles):
    @pl.when(j >= n_qk_tiles)
    def _():
        o_ref[0] = acc.astype(o_ref.dtype)

    @pl.when(j < n_qk_tiles)
    def _():
        gain = jnp.where(j < n_qk_tiles // 2, qg_ref[...], kg_ref[...])
        cos, sin = cos_ref[...], sin_ref[...]
        lane = lax.broadcasted_iota(I32, (1, LANES), 1)
        first_half = (lane % 32) < 16
        for s in range(tn // LANES):
            y = acc[:, s * LANES:(s + 1) * LANES]
            hi, lo = _split_bf16(y * y)
            gsum = _dot(hi, gm_ref[...]) + _dot(lo, gm_ref[...])
            yn = y * lax.rsqrt(gsum * (1.0 / DA_HEAD_DIM) + EPS) * gain
            partner = jnp.where(first_half, pltpu.roll(yn, LANES - 16, 1), pltpu.roll(yn, 16, 1))
            o_ref[0, :, s * LANES:(s + 1) * LANES] = (yn * cos + partner * sin).astype(o_ref.dtype)


def _attn_kernel(lam_ref, q_ref, kc_ref, vc_ref, *rest, has_lat, tk, n_chunks, out_scale):
    if has_lat:
        kl_ref, vl_ref, sg_ref, o_ref = rest
    else:
        sg_ref, o_ref = rest
    q = q_ref[0]
    tq = q.shape[0]
    lane = lax.broadcasted_iota(I32, (1, LANES), 1)
    zero = jnp.zeros_like(q)
    q2 = jnp.concatenate([jnp.where(lane < DA_HEAD_DIM, q, zero),
                          jnp.where(lane >= DA_HEAD_DIM, q, zero)], axis=0)

    def step(k, v, carry):
        m, l, acc = carry
        s = lax.dot_general(q2, k, (((1,), (1,)), ((), ())), preferred_element_type=F32)
        m_new = jnp.maximum(m, jnp.max(s, axis=-1, keepdims=True))
        alpha = jnp.exp2(m - m_new)
        p = jnp.exp2(s - m_new)
        l_new = alpha * l + jnp.sum(p, axis=-1, keepdims=True)
        acc_new = alpha * acc + _dot(p.astype(BF16), v)
        return m_new, l_new, acc_new

    carry = (jnp.full((2 * tq, 1), -1e30, F32), jnp.zeros((2 * tq, 1), F32),
             jnp.zeros((2 * tq, LANES), F32))
    carry = step(kc_ref[0], vc_ref[0], carry)
    if has_lat:
        def body(c, carry):
            off = pl.multiple_of(c * tk, tk)
            return step(kl_ref[0, pl.ds(off, tk), :], vl_ref[0, pl.ds(off, tk), :], carry)
        carry = lax.fori_loop(0, n_chunks, body, carry)
    _, l, acc = carry
    o = acc / l
    o = o[:tq] - lam_ref[0] * o[tq:]
    ms = jnp.mean(o * o, axis=-1, keepdims=True)
    o_ref[0] = (o * lax.rsqrt(ms + EPS) * sg_ref[...] * out_scale).astype(o_ref.dtype)


def diff_attention_core(lam, qkv_q, qkv_c, qkv_l, sub_g, out_scale):
    B, nq, D3 = qkv_q.shape
    D = D3 // 3
    H = D // DA_V_DIM
    nc = qkv_c.shape[1]
    tq = min(256, nq)
    has_lat = qkv_l is not None
    in_specs = [pl.BlockSpec(memory_space=pltpu.SMEM),
                pl.BlockSpec((1, tq, LANES), lambda b, h, i: (b, i, h)),
                pl.BlockSpec((1, nc, LANES), lambda b, h, i: (b, 0, H + h)),
                pl.BlockSpec((1, nc, LANES), lambda b, h, i: (b, 0, 2 * H + h))]
    args = [lam, qkv_q, qkv_c, qkv_c]
    tk, n_chunks = 0, 0
    if has_lat:
        nl = qkv_l.shape[1]
        tk = min(512, nl)
        n_chunks = nl // tk
        in_specs += [pl.BlockSpec((1, nl, LANES), lambda b, h, i: (b, 0, H + h)),
                     pl.BlockSpec((1, nl, LANES), lambda b, h, i: (b, 0, 2 * H + h))]
        args += [qkv_l, qkv_l]
    in_specs.append(pl.BlockSpec((1, LANES), lambda b, h, i: (0, 0)))
    args.append(sub_g.reshape(1, LANES))
    kern = functools.partial(_attn_kernel, has_lat=has_lat, tk=tk, n_chunks=n_chunks,
                             out_scale=out_scale)
    return pl.pallas_call(
        kern,
        grid=(B, H, nq // tq),
        in_specs=in_specs,
        out_specs=pl.BlockSpec((1, tq, LANES), lambda b, h, i: (b, i, h)),
        out_shape=jax.ShapeDtypeStruct((B, nq, D), BF16),
        compiler_params=_cparams("parallel", "parallel", "arbitrary"),
        name="diff_attention",
    )(*args)


def _axial_rope_tables(n):
    f = DA_HEAD_DIM // 4
    inv = 1.0 / (ROPE_BASE ** (jnp.arange(f, dtype=F32) / f))
    rows = n // GRID_W
    row = jnp.repeat(jnp.arange(rows), GRID_W).astype(F32)
    col = jnp.tile(jnp.arange(GRID_W), rows).astype(F32)
    ang = jnp.stack([row[:, None] * inv, col[:, None] * inv], axis=1)
    cos, sin = jnp.cos(ang), jnp.sin(ang)
    cos64 = jnp.concatenate([cos, cos], axis=-1).reshape(n, DA_HEAD_DIM)
    sin64 = jnp.concatenate([-sin, sin], axis=-1).reshape(n, DA_HEAD_DIM)
    return jnp.tile(cos64, (1, 2)), jnp.tile(sin64, (1, 2))


def da_project(h_in, g, sh, sc, w_in_bf, q_g, k_g, cos, sin):
    B, n, D = h_in.shape
    tn = 512
    tm = min(512, n)
    gm = (np.arange(LANES)[:, None] // DA_HEAD_DIM == np.arange(LANES)[None, :] // DA_HEAD_DIM)
    gm = jnp.asarray(gm, BF16)
    qg = (jnp.tile(q_g.astype(F32), 2) * (DA_HEAD_DIM ** -0.5 * math.log2(math.e))).reshape(1, LANES)
    kg = jnp.tile(k_g.astype(F32), 2).reshape(1, LANES)
    extra = (cos, sin, qg, kg, gm)
    extra_specs = (pl.BlockSpec((tm, LANES), lambda b, i, j: (i, 0)),
                   pl.BlockSpec((tm, LANES), lambda b, i, j: (i, 0)),
                   pl.BlockSpec((1, LANES), lambda b, i, j: (0, 0)),
                   pl.BlockSpec((1, LANES), lambda b, i, j: (0, 0)),
                   pl.BlockSpec((LANES, LANES), lambda b, i, j: (0, 0)))
    epi = functools.partial(_da_epilogue, tn=tn, n_qk_tiles=2 * D // tn)
    return mod_proj(h_in, g, sh, sc, w_in_bf, jnp.zeros((3 * D,), F32), epilogue=epi, extra=extra,
                    extra_specs=extra_specs, tn=tn, name="da_project")


def diff_attention_layer(x, ctx, mv, norm_g, w_in, q_g, k_g, lam, sub_g, w_out, lambda_init, need_ctx):
    B, L, D = x.shape
    nc = ctx.shape[1]
    w_in_bf = w_in.astype(BF16)
    w_out_bf = w_out.astype(BF16)
    cos_l, sin_l = _axial_rope_tables(L)
    cos_c, sin_c = jnp.ones((nc, LANES), F32), jnp.zeros((nc, LANES), F32)
    qkv_l = da_project(x, norm_g, mv["sh_l"], mv["sc_l"], w_in_bf, q_g, k_g, cos_l, sin_l)
    qkv_c = da_project(ctx, norm_g, mv["sh_c"], mv["sc_c"], w_in_bf, q_g, k_g, cos_c, sin_c)
    lamf = lam.astype(F32)
    lam_val = (jnp.exp(jnp.sum(lamf[0] * lamf[1])) - jnp.exp(jnp.sum(lamf[2] * lamf[3]))
               + lambda_init).reshape(1)
    out_scale = 1.0 - lambda_init
    o_l = diff_attention_core(lam_val, qkv_l, qkv_c, qkv_l, sub_g, out_scale)
    zb = jnp.zeros((D,), F32)
    x = res_matmul(o_l, w_out_bf, zb, x, mv["g_l"])
    if need_ctx:
        o_c = diff_attention_core(lam_val, qkv_c, qkv_c, None, sub_g, out_scale)
        ctx = res_matmul(o_c, w_out_bf, zb, ctx, mv["g_c"])
    return x, ctx


def _router_kernel(x_ref, g_ref, sh_ref, sc_ref, whi_ref, wlo_ref, o_ref):
    h = _modulate(x_ref[0], g_ref[...], sh_ref[0], sc_ref[0])
    o_ref[0] = _router_affinity(h, whi_ref[...], wlo_ref[...])


def _router_affinity(h, w_hi, w_lo):
    h_hi, h_lo = _split_bf16(h)
    logits = _dot3(h_hi, h_lo, w_hi, w_lo)
    lane = lax.broadcasted_iota(I32, (1, LANES), 1)
    logits = jnp.where(lane < N_EXPERTS, logits, -1e30)
    z = jnp.exp(logits - jnp.max(logits, axis=-1, keepdims=True))
    return z / jnp.sum(z, axis=-1, keepdims=True)


def router_affinity(x, g, sh, sc, w_hi, w_lo):
    B, n, D = x.shape
    tm = min(512, n)
    return pl.pallas_call(
        _router_kernel,
        grid=(B, n // tm),
        in_specs=[pl.BlockSpec((1, tm, D), lambda b, i: (b, i, 0)),
                  pl.BlockSpec((1, D), lambda b, i: (0, 0)),
                  pl.BlockSpec((1, 1, D), lambda b, i: (b, 0, 0)),
                  pl.BlockSpec((1, 1, D), lambda b, i: (b, 0, 0)),
                  pl.BlockSpec((D, LANES), lambda b, i: (0, 0)),
                  pl.BlockSpec((D, LANES), lambda b, i: (0, 0))],
        out_specs=pl.BlockSpec((1, tm, LANES), lambda b, i: (b, i, 0)),
        out_shape=jax.ShapeDtypeStruct((B, n, LANES), F32),
        compiler_params=_cparams("parallel", "parallel"),
        name="router",
    )(x, g.reshape(1, D), sh, sc, w_hi, w_lo)


def _ffn_kernel(idx_ref, x_hbm, acc_in, g_ref, sh_ref, sc_ref, g2_ref, rhi_ref, rlo_ref,
                w1_ref, w3_ref, w2_ref, o_hbm, xbuf, abuf, sem, *, n, cap, tm):
    del acc_in
    b, e, j = pl.program_id(0), pl.program_id(1), pl.program_id(2)
    n_exp = pl.num_programs(1)
    base = (b * n_exp + e) * cap + j * tm
    row0 = b * n

    def gather(k, c):
        row = row0 + idx_ref[base + k]
        pltpu.make_async_copy(x_hbm.at[pl.ds(row, 1)], xbuf.at[pl.ds(k, 1)], sem.at[0]).start()
        pltpu.make_async_copy(o_hbm.at[pl.ds(row, 1)], abuf.at[pl.ds(k, 1)], sem.at[1]).start()
        return c

    lax.fori_loop(0, tm, gather, 0)
    pltpu.make_async_copy(x_hbm.at[pl.ds(0, tm)], xbuf, sem.at[0]).wait()

    h = _modulate(xbuf[...], g_ref[...], sh_ref[0], sc_ref[0])
    aff = _router_affinity(h, rhi_ref[...], rlo_ref[...])
    lane = lax.broadcasted_iota(I32, (1, LANES), 1)
    gate = jnp.sum(jnp.where(lane == e, aff, 0.0), axis=-1, keepdims=True)
    hb = h.astype(BF16)
    a1 = _dot(hb, w1_ref[0])
    a3 = _dot(hb, w3_ref[0])
    hid = (a1 * jax.nn.sigmoid(a1) * a3).astype(BF16)
    y = _dot(hid, w2_ref[0])
    upd = g2_ref[0] * (gate * y)

    pltpu.make_async_copy(o_hbm.at[pl.ds(0, tm)], abuf, sem.at[1]).wait()
    abuf[...] = abuf[...] + upd

    def scatter(k, c):
        row = row0 + idx_ref[base + k]
        pltpu.make_async_copy(abuf.at[pl.ds(k, 1)], o_hbm.at[pl.ds(row, 1)], sem.at[2]).start()
        return c

    lax.fori_loop(0, tm, scatter, 0)
    pltpu.make_async_copy(abuf, o_hbm.at[pl.ds(0, tm)], sem.at[2]).wait()


def expert_ffn(idx, x, g, sh, sc, g2, r_hi, r_lo, w1, w3, w2):
    B, n, D = x.shape
    E, cap = idx.shape[1], idx.shape[2]
    F = w1.shape[-1]
    tm = min(256, cap)
    x2 = x.reshape(B * n, D)
    kern = functools.partial(_ffn_kernel, n=n, cap=cap, tm=tm)
    vec = lambda: pl.BlockSpec((1, 1, D), lambda b, e, j, idx: (b, 0, 0))
    grid_spec = pltpu.PrefetchScalarGridSpec(
        num_scalar_prefetch=1,
        grid=(B, E, cap // tm),
        in_specs=[pl.BlockSpec(memory_space=pl.ANY),
                  pl.BlockSpec(memory_space=pl.ANY),
                  pl.BlockSpec((1, D), lambda b, e, j, idx: (0, 0)),
                  vec(), vec(), vec(),
                  pl.BlockSpec((D, LANES), lambda b, e, j, idx: (0, 0)),
                  pl.BlockSpec((D, LANES), lambda b, e, j, idx: (0, 0)),
                  pl.BlockSpec((1, D, F), lambda b, e, j, idx: (e, 0, 0)),
                  pl.BlockSpec((1, D, F), lambda b, e, j, idx: (e, 0, 0)),
                  pl.BlockSpec((1, F, D), lambda b, e, j, idx: (e, 0, 0))],
        out_specs=pl.BlockSpec(memory_space=pl.ANY),
        scratch_shapes=[pltpu.VMEM((tm, D), F32), pltpu.VMEM((tm, D), F32),
                        pltpu.SemaphoreType.DMA((3,))])
    out = pl.pallas_call(
        kern,
        grid_spec=grid_spec,
        out_shape=jax.ShapeDtypeStruct((B * n, D), F32),
        input_output_aliases={2: 0},
        compiler_params=_cparams("arbitrary", "arbitrary", "arbitrary"),
        name="expert_ffn",
    )(idx.reshape(-1), x2, x2, g.reshape(1, D), sh, sc, g2, r_hi, r_lo, w1, w3, w2)
    return out.reshape(B, n, D)


TOPK_ROWS = 128


def _topk_kernel(a_ref, idx_ref, *, k, capp):
    a = a_ref[0, 0]
    bits = lax.bitcast_convert_type(a, I32)
    row_i = lax.broadcasted_iota(I32, (TOPK_ROWS, LANES), 0)
    lane_i = lax.broadcasted_iota(I32, (TOPK_ROWS, LANES), 1)
    tok = row_i * LANES + lane_i

    def count(ind):
        return jnp.sum(jnp.sum(ind, axis=1, keepdims=True), axis=0, keepdims=True)

    thr = jnp.zeros((1, 1), I32)
    for bit in range(30, -1, -1):
        cand = thr | (1 << bit)
        thr = jnp.where(count(jnp.where(bits >= cand, 1, 0)) >= k, cand, thr)
    gt = jnp.where(bits > thr, 1, 0)
    eq = jnp.where(bits == thr, 1, 0)
    need = k - count(gt)
    tcut = jnp.zeros((1, 1), I32)
    for bit in range((TOPK_ROWS * LANES).bit_length() - 2, -1, -1):
        cand = tcut | (1 << bit)
        tcut = jnp.where(count(jnp.where(tok < cand, eq, 0)) < need, cand, tcut)
    sel = gt + jnp.where(tok <= tcut, eq, 0)

    m = sel.astype(F32)
    cnt = jnp.sum(m, axis=1, keepdims=True)
    tri = jnp.where(lax.broadcasted_iota(I32, (TOPK_ROWS, TOPK_ROWS), 0)
                    >= lax.broadcasted_iota(I32, (TOPK_ROWS, TOPK_ROWS), 1), 1.0, 0.0).astype(BF16)
    incl = _dot(tri, jnp.broadcast_to(cnt, (TOPK_ROWS, LANES)).astype(BF16))[:, :1]
    slot = lax.broadcasted_iota(I32, (1, capp), 1).astype(F32)
    before = incl <= slot
    row_of = jnp.sum(jnp.where(before, 1.0, 0.0), axis=0, keepdims=True)
    base = jnp.sum(jnp.where(before, cnt, 0.0), axis=0, keepdims=True)
    rows_col = lax.broadcasted_iota(I32, (TOPK_ROWS, 1), 0).astype(F32)
    onehot = jnp.where(rows_col == row_of, 1.0, 0.0).astype(BF16)
    picked = _dot(m.T.astype(BF16), onehot)
    cum = _dot(tri, picked.astype(BF16))
    local = jnp.sum(jnp.where(cum <= slot - base, 1.0, 0.0), axis=0, keepdims=True)
    idx_ref[0, 0] = (row_of * LANES + local).astype(I32)


def expert_choice_topk(aff, cap):
    B, E, n = aff.shape
    n_pad = TOPK_ROWS * LANES
    assert n <= n_pad and TOPK_ROWS == LANES
    a = jnp.pad(aff, ((0, 0), (0, 0), (0, n_pad - n)), constant_values=-1.0)
    a = a.reshape(B, E, TOPK_ROWS, LANES)
    capp = max(cap, LANES)
    out = pl.pallas_call(
        functools.partial(_topk_kernel, k=cap, capp=capp),
        grid=(B, E),
        in_specs=[pl.BlockSpec((1, 1, TOPK_ROWS, LANES), lambda b, e: (b, e, 0, 0))],
        out_specs=pl.BlockSpec((1, 1, 1, capp), lambda b, e: (b, e, 0, 0)),
        out_shape=jax.ShapeDtypeStruct((B, E, 1, capp), I32),
        compiler_params=_cparams("parallel", "parallel"),
        name="expert_topk",
    )(a)
    return out[:, :, 0, :cap]


def moe_layer(x, g, sh, sc, g2, router_w, w1, w3, w2):
    B, n, D = x.shape
    E = router_w.shape[1]
    cap = EC_CAPACITY_FACTOR * n // E
    rw = jnp.pad(router_w.astype(F32), ((0, 0), (0, LANES - E)))
    r_hi, r_lo = _split_bf16(rw)
    aff = router_affinity(x, g, sh, sc, r_hi, r_lo)
    idx = expert_choice_topk(aff[:, :, :E].transpose(0, 2, 1), cap)
    return expert_ffn(idx, x, g, sh, sc, g2, r_hi, r_lo, w1, w3, w2)


def _ret_epilogue(acc, j, o_ref, cos_ref, sin_ref, *, tn, n_q_tiles):
    @pl.when(j >= 2 * n_q_tiles)
    def _():
        o_ref[0] = acc.astype(o_ref.dtype)

    @pl.when(j < 2 * n_q_tiles)
    def _():
        scale = jnp.where(j < n_q_tiles, 1.0, RET_QK_DIM ** -0.5)
        cos, sin = cos_ref[...] * scale, sin_ref[...] * scale
        f = RET_QK_DIM // 2
        for s in range(tn // RET_QK_DIM):
            lo, mid, hi = s * RET_QK_DIM, s * RET_QK_DIM + f, (s + 1) * RET_QK_DIM
            x1, x2 = acc[:, lo:mid], acc[:, mid:hi]
            o_ref[0, :, lo:mid] = (x1 * cos - x2 * sin).astype(o_ref.dtype)
            o_ref[0, :, mid:hi] = (x2 * cos + x1 * sin).astype(o_ref.dtype)


def _retention_kernel(lg_ref, qf_ref, kf_ref, vf_ref, qb_ref, kb_ref, vb_ref, s0f_ref, s0b_ref,
                      of_ref, ob_ref, sf_ref, sb_ref, stf, stb, *, T):
    h, c = pl.program_id(1), pl.program_id(2)

    @pl.when(c == 0)
    def _():
        stf[...] = s0f_ref[0, 0]
        stb[...] = s0b_ref[0, 0]

    ii = lax.broadcasted_iota(I32, (T, T), 0).astype(F32)
    jj = lax.broadcasted_iota(I32, (T, T), 1).astype(F32)
    row = lax.broadcasted_iota(I32, (T, 1), 0).astype(F32)

    def one(q_ref, k_ref, v_ref, st, o_ref, lg, forward):
        q, k, v = q_ref[0], k_ref[0], v_ref[0]
        d = ii - jj if forward else jj - ii
        mask = d >= 0 if forward else d > 0
        dmask = jnp.where(mask, jnp.exp(jnp.where(mask, d, 0.0) * lg), 0.0)
        xi = jnp.exp(((row + 1.0) if forward else (T - row)) * lg)
        zeta = jnp.exp(((T - 1.0 - row) if forward else row) * lg)
        inner = lax.dot_general(q, k, (((1,), (1,)), ((), ())), preferred_element_type=F32) * dmask
        o = _dot(inner.astype(BF16), v) + _dot(q, st[...].astype(BF16)) * xi
        o_ref[0] = o.astype(o_ref.dtype)
        kz = (k.astype(F32) * zeta).astype(BF16)
        decay = jnp.exp(jnp.full((1, 1), T, F32) * lg)
        st[...] = st[...] * decay + lax.dot_general(kz, v, (((0,), (0,)), ((), ())),
                                                    preferred_element_type=F32)

    one(qf_ref, kf_ref, vf_ref, stf, of_ref, lg_ref[0, h], True)
    one(qb_ref, kb_ref, vb_ref, stb, ob_ref, lg_ref[1, h], False)

    @pl.when(c == pl.num_programs(2) - 1)
    def _():
        sf_ref[0, 0] = stf[...]
        sb_ref[0, 0] = stb[...]


def retention_core(log_g, p, s0f, s0b):
    B, n, _ = p.shape
    H, dk, dv = s0f.shape[1], s0f.shape[2], s0f.shape[3]
    T = min(256, n)
    nc = n // T
    kq, kk, kv = 0, (H * dk) // dk, (2 * H * dk) // dv
    fwd = lambda off: (lambda b, h, c: (b, c, off + h))
    bwd = lambda off: (lambda b, h, c: (b, nc - 1 - c, off + h))
    st_spec = pl.BlockSpec((1, 1, dk, dv), lambda b, h, c: (b, h, 0, 0))
    out_sd = jax.ShapeDtypeStruct((B, n, H * dv), BF16)
    st_sd = jax.ShapeDtypeStruct((B, H, dk, dv), F32)
    return pl.pallas_call(
        functools.partial(_retention_kernel, T=T),
        grid=(B, H, nc),
        in_specs=[pl.BlockSpec(memory_space=pltpu.SMEM),
                  pl.BlockSpec((1, T, dk), fwd(kq)), pl.BlockSpec((1, T, dk), fwd(kk)),
                  pl.BlockSpec((1, T, dv), fwd(kv)),
                  pl.BlockSpec((1, T, dk), bwd(kq)), pl.BlockSpec((1, T, dk), bwd(kk)),
                  pl.BlockSpec((1, T, dv), bwd(kv)),
                  st_spec, st_spec],
        out_specs=[pl.BlockSpec((1, T, dv), fwd(0)), pl.BlockSpec((1, T, dv), bwd(0)), st_spec, st_spec],
        out_shape=[out_sd, out_sd, st_sd, st_sd],
        scratch_shapes=[pltpu.VMEM((dk, dv), F32), pltpu.VMEM((dk, dv), F32)],
        compiler_params=_cparams("parallel", "parallel", "arbitrary"),
        name="retention",
    )(log_g, p, p, p, p, p, p, s0f, s0b)


def _ret_finish_kernel(of_ref, ob_ref, g_ref, gn_ref, z_ref, *, dv):
    for s in range(of_ref.shape[-1] // dv):
        sl = slice(s * dv, (s + 1) * dv)
        y = of_ref[0, :, sl].astype(F32) + ob_ref[0, :, sl].astype(F32)
        ms = jnp.mean(y * y, axis=-1, keepdims=True)
        y = y * lax.rsqrt(ms + EPS) * gn_ref[...]
        g = g_ref[0, :, sl].astype(F32)
        z_ref[0, :, sl] = (g * jax.nn.sigmoid(g) * y).astype(z_ref.dtype)


def retention_finish(of, ob, p, gn_g):
    B, n, N = of.shape
    dv = gn_g.shape[0]
    tm = min(512, n)
    blk = lambda off: pl.BlockSpec((1, tm, N), lambda b, i: (b, i, off))
    return pl.pallas_call(
        functools.partial(_ret_finish_kernel, dv=dv),
        grid=(B, n // tm),
        in_specs=[blk(0), blk(0), blk(p.shape[-1] // N - 1), pl.BlockSpec((1, dv), lambda b, i: (0, 0))],
        out_specs=blk(0),
        out_shape=jax.ShapeDtypeStruct((B, n, N), BF16),
        compiler_params=_cparams("parallel", "parallel"),
        name="retention_finish",
    )(of, ob, p, gn_g.astype(F32).reshape(1, dv))


def retention_layer(x, ctx, mv, norm_g, w_in, log_alpha, gn_g, w_out, need_ctx):
    B, L, D = x.shape
    nc = ctx.shape[1]
    dk, dv = RET_QK_DIM, RET_V_DIM
    H = D // dk
    w_in_bf = w_in.astype(BF16)
    w_out_bf = w_out.astype(BF16)
    zb = jnp.zeros((w_in.shape[1],), F32)
    log_g = jnp.log1p(-jnp.exp(log_alpha.astype(F32)))
    f = dk // 2
    inv = 1.0 / (ROPE_BASE ** jnp.linspace(0.0, 1.0, f, dtype=F32))
    ang = jnp.arange(L, dtype=F32)[:, None] * inv
    tn = 512

    def project(h_in, sh, sc, cos, sin):
        tm = min(512, h_in.shape[1])
        specs = (pl.BlockSpec((tm, f), lambda b, i, j: (i, 0)), pl.BlockSpec((tm, f), lambda b, i, j: (i, 0)))
        epi = functools.partial(_ret_epilogue, tn=tn, n_q_tiles=H * dk // tn)
        return mod_proj(h_in, norm_g, sh, sc, w_in_bf, zb, epilogue=epi, extra=(cos, sin),
                        extra_specs=specs, tn=tn, name="ret_project")

    p_c = project(ctx, mv["sh_c"], mv["sc_c"], jnp.ones((nc, f), F32), jnp.zeros((nc, f), F32))
    p_l = project(x, mv["sh_l"], mv["sc_l"], jnp.cos(ang), jnp.sin(ang))
    zero = jnp.zeros((B, H, dk, dv), F32)
    of_c, ob_c, s_f, s_b = retention_core(log_g, p_c, zero, zero)
    of_l, ob_l, _, _ = retention_core(log_g, p_l, s_f, s_b)
    zd = jnp.zeros((D,), F32)
    x = res_matmul(retention_finish(of_l, ob_l, p_l, gn_g), w_out_bf, zd, x, mv["g_l"])
    if need_ctx:
        ctx = res_matmul(retention_finish(of_c, ob_c, p_c, gn_g), w_out_bf, zd, ctx, mv["g_c"])
    return x, ctx


def _hyena_filter_spectra(n, w1, b1, w2, b2, freq, w3):
    t = jnp.linspace(0.0, 1.0, n, dtype=F32)[:, None]
    w = 2.0 * math.pi * jnp.arange(n, dtype=F32)[:, None] / n
    f = jnp.linspace(1e-4, HY_BANDS - 1, HY_BANDS, dtype=F32)[None]
    z = jnp.concatenate([t, jnp.cos(f * w), -jnp.sin(f * w)], axis=-1)
    hp = lax.Precision.HIGHEST
    a = jnp.sin(freq[0].astype(F32) * (jnp.dot(z, w1.astype(F32), precision=hp) + b1.astype(F32)))
    a = jnp.sin(freq[1].astype(F32) * (jnp.dot(a, w2.astype(F32), precision=hp) + b2.astype(F32)))
    C = w3.shape[-1] // (2 * HY_ORDER)
    h = jnp.dot(a, w3.astype(F32), precision=hp).reshape(n, HY_ORDER, 2, C)
    max_decay = math.log(HY_TARGET) / HY_FAST_DECAY_PCT
    min_decay = math.log(HY_TARGET) / HY_SLOW_DECAY_PCT
    deltas = jnp.abs(jnp.linspace(min_decay, max_decay, C, dtype=F32))
    h = h * jnp.exp(-t * deltas)[:, None, None, :]
    buf = jnp.concatenate([h[:, :, 0], jnp.zeros((1, HY_ORDER, C), F32),
                           jnp.flip(h[1:, :, 1], axis=0)], axis=0)
    buf = buf / jnp.sum(jnp.abs(buf), axis=0, keepdims=True)
    return jnp.fft.rfft(buf, axis=0)


def _short_conv_kernel(u_ref, prev_ref, next_ref, w_ref, b_ref, o_ref):
    i, ni = pl.program_id(1), pl.num_programs(1)
    u = u_ref[0]
    tm = u.shape[0]
    row = lax.broadcasted_iota(I32, (tm, 1), 0)
    prev_row = jnp.where(i > 0, prev_ref[0, 7:8, :], 0.0)
    next_row = jnp.where(i < ni - 1, next_ref[0, 0:1, :], 0.0)
    before = jnp.where(row == 0, prev_row, pltpu.roll(u, 1, 0))
    after = jnp.where(row == tm - 1, next_row, pltpu.roll(u, tm - 1, 0))
    w = w_ref[...]
    o_ref[0, 0] = (before * w[0:1] + u * w[1:2] + after * w[2:3] + b_ref[...]).astype(o_ref.dtype)


def short_conv_split(u, conv_w, conv_b):
    B, n, C3 = u.shape
    C = C3 // 3
    tm = min(512, n)
    r8 = tm // 8
    return pl.pallas_call(
        _short_conv_kernel,
        grid=(B, n // tm, 3),
        in_specs=[pl.BlockSpec((1, tm, C), lambda b, i, j: (b, i, j)),
                  pl.BlockSpec((1, 8, C), lambda b, i, j: (b, jnp.maximum(i * r8 - 1, 0), j)),
                  pl.BlockSpec((1, 8, C), lambda b, i, j: (b, jnp.minimum((i + 1) * r8, n // 8 - 1), j)),
                  pl.BlockSpec((8, C), lambda b, i, j: (0, j)),
                  pl.BlockSpec((1, C), lambda b, i, j: (0, j))],
        out_specs=pl.BlockSpec((1, 1, tm, C), lambda b, i, j: (j, b, i, 0)),
        out_shape=jax.ShapeDtypeStruct((3, B, n, C), BF16),
        compiler_params=_cparams("parallel", "parallel", "parallel"),
        name="hy_short_conv",
    )(u, u, u, jnp.pad(conv_w.astype(F32), ((0, 5), (0, 0))), conv_b.astype(F32).reshape(1, C3))


FFT_N1 = 128


def _stack_complex(re, im):
    return jnp.concatenate([jnp.concatenate([re, -im], axis=-1),
                            jnp.concatenate([im, re], axis=-1)], axis=-2)


def _phase(k, period):
    a = (2.0 * math.pi / period) * (k % period).astype(F32)
    return jnp.cos(a), jnp.sin(a)


def _fft_a_kernel(z_ref, m_ref, o_ref):
    z = z_ref[...]
    o_ref[...] = _dot(m_ref[...], z.reshape(2 * z.shape[1], z.shape[2])).astype(o_ref.dtype)


def _fft_c_kernel(a_ref, gf_ref, gi_ref, h_ref, o_ref):
    n2, C = a_ref.shape[2], a_ref.shape[3]
    x = _dot(gf_ref[0], a_ref[:, 0].reshape(2 * n2, C))
    xr, xi = x[:n2], x[n2:]
    hr, hi = h_ref[0, 0], h_ref[0, 1]
    y = jnp.concatenate([xr * hr - xi * hi, xr * hi + xi * hr], axis=0).astype(BF16)
    o_ref[:, 0] = _dot(gi_ref[0], y).astype(o_ref.dtype).reshape(2, n2, C)


def _fft_b_kernel(b_ref, m_ref, u_ref, xg_ref, bias_ref, o_ref):
    s1, tc = u_ref.shape[1], u_ref.shape[2]
    y = _dot(m_ref[...], b_ref[...]).reshape(2, s1, tc)
    o_ref[...] = (xg_ref[...].astype(F32) * (y + u_ref[...].astype(F32) * bias_ref[...])).astype(o_ref.dtype)


def _dense_conv_kernel(u_ref, mf_ref, mi_ref, h_ref, xg_ref, bias_ref, o_ref):
    n, tc = u_ref.shape[1], u_ref.shape[2]
    u = u_ref[...]
    x = _dot(mf_ref[...], u.reshape(2 * n, tc))
    N = x.shape[0] // 2
    xr, xi = x[:N], x[N:]
    hr, hi = h_ref[0], h_ref[1]
    y = jnp.concatenate([xr * hr - xi * hi, xr * hi + xi * hr], axis=0).astype(BF16)
    out = _dot(mi_ref[...], y).reshape(2, n, tc)
    o_ref[...] = (xg_ref[...].astype(F32) * (out + u.astype(F32) * bias_ref[...])).astype(o_ref.dtype)


def long_conv_gate(u, xg, hfull, bias):
    B, n, C = u.shape
    assert B == 2
    N = 2 * n
    hr, hi = jnp.real(hfull).astype(F32), jnp.imag(hfull).astype(F32)
    bias3 = bias.astype(F32).reshape(1, 1, C)
    if n <= 512:
        f, t = jnp.arange(N, dtype=I32)[:, None], jnp.arange(n, dtype=I32)[None, :]
        c, s = _phase(f * t, N)
        mf = _stack_complex(c, -s).astype(BF16)
        mi = _stack_complex(c.T / N, s.T / N).astype(BF16)
        tc = 256
        blk = pl.BlockSpec((2, n, tc), lambda j: (0, 0, j))
        return pl.pallas_call(
            _dense_conv_kernel,
            grid=(C // tc,),
            in_specs=[blk, pl.BlockSpec((2 * N, 2 * n), lambda j: (0, 0)),
                      pl.BlockSpec((2 * n, 2 * N), lambda j: (0, 0)),
                      pl.BlockSpec((2, N, tc), lambda j: (0, 0, j)), blk,
                      pl.BlockSpec((1, 1, tc), lambda j: (0, 0, j))],
            out_specs=blk,
            out_shape=jax.ShapeDtypeStruct((2, n, C), BF16),
            compiler_params=_cparams("parallel"),
            name="hy_dense_conv",
        )(u, mf, mi, jnp.stack([hr, hi]), xg, bias3)

    N1, S1 = FFT_N1, FFT_N1 // 2
    N2 = N // N1
    cols = N2 * C
    tc = 2 * C
    f1 = jnp.arange(N1, dtype=I32)
    c, s = _phase(f1[:, None] * jnp.arange(S1, dtype=I32)[None, :], N1)
    m_a = _stack_complex(c, -s).astype(BF16)
    m_b = _stack_complex(c.T / N, s.T / N).astype(BF16)
    f2 = jnp.arange(N2, dtype=I32)
    c, s = _phase(f2[None, None, :] * (f1[:, None, None] + N1 * f2[None, :, None]), N)
    g_f = _stack_complex(c, -s).astype(BF16)
    ct, st = jnp.swapaxes(c, 1, 2), jnp.swapaxes(s, 1, 2)
    g_i = _stack_complex(ct, st).astype(BF16)
    hperm = jnp.stack([hr.reshape(N2, N1, C).transpose(1, 0, 2), hi.reshape(N2, N1, C).transpose(1, 0, 2)],
                      axis=1)
    zb = pl.BlockSpec((2, S1, tc), lambda j: (0, 0, j))
    a = pl.pallas_call(
        _fft_a_kernel,
        grid=(cols // tc,),
        in_specs=[zb, pl.BlockSpec((2 * N1, 2 * S1), lambda j: (0, 0))],
        out_specs=pl.BlockSpec((2 * N1, tc), lambda j: (0, j)),
        out_shape=jax.ShapeDtypeStruct((2 * N1, cols), BF16),
        compiler_params=_cparams("parallel"),
        name="hy_fft_a",
    )(u.reshape(2, S1, cols), m_a)
    ab = pl.BlockSpec((2, 1, N2, C), lambda i: (0, i, 0, 0))
    gb = pl.BlockSpec((1, 2 * N2, 2 * N2), lambda i: (i, 0, 0))
    bm = pl.pallas_call(
        _fft_c_kernel,
        grid=(N1,),
        in_specs=[ab, gb, gb, pl.BlockSpec((1, 2, N2, C), lambda i: (i, 0, 0, 0))],
        out_specs=ab,
        out_shape=jax.ShapeDtypeStruct((2, N1, N2, C), BF16),
        compiler_params=_cparams("parallel"),
        name="hy_fft_c",
    )(a.reshape(2, N1, N2, C), g_f, g_i, hperm)
    out = pl.pallas_call(
        _fft_b_kernel,
        grid=(cols // tc,),
        in_specs=[pl.BlockSpec((2 * N1, tc), lambda j: (0, j)),
                  pl.BlockSpec((2 * S1, 2 * N1), lambda j: (0, 0)), zb, zb,
                  pl.BlockSpec((1, 1, tc), lambda j: (0, 0, 0))],
        out_specs=zb,
        out_shape=jax.ShapeDtypeStruct((2, S1, cols), BF16),
        compiler_params=_cparams("parallel"),
        name="hy_fft_b",
    )(bm.reshape(2 * N1, cols), m_b, u.reshape(2, S1, cols), xg.reshape(2, S1, cols),
      jnp.tile(bias3, (1, 1, tc // C)))
    return out.reshape(2, n, C)


def hyena_layer(x, ctx, mv, norm_g, hy_p, need_ctx):
    (w_in, b_in, conv_w, conv_b, w1, b1, w2, b2, freq, w3, fbias, w_out, b_out) = hy_p
    w_in_bf = w_in.astype(BF16)
    w_out_bf = w_out.astype(BF16)

    def mix(h_in, sh, sc, res, gate):
        n = h_in.shape[1]
        u = mod_proj(h_in, norm_g, sh, sc, w_in_bf, b_in.astype(F32), out_dtype=F32, name="hy_project")
        v, x1, x2 = short_conv_split(u, conv_w, conv_b)
        hf = _hyena_filter_spectra(n, w1, b1, w2, b2, freq, w3)
        hfull = jnp.concatenate([hf, jnp.conj(jnp.flip(hf[1:n], axis=0))], axis=0)
        z = long_conv_gate(v, x1, hfull[:, 0], fbias[0])
        z = long_conv_gate(z, x2, hfull[:, 1], fbias[1])
        return res_matmul(z, w_out_bf, b_out.astype(F32), res, gate)

    x = mix(x, mv["sh_l"], mv["sc_l"], x, mv["g_l"])
    if need_ctx:
        ctx = mix(ctx, mv["sh_c"], mv["sc_c"], ctx, mv["g_c"])
    return x, ctx


def kernel(x, c, ctx, c_ctx, norm_mix_g, norm_ffn_g, mod_w, mod_b,
           da_w_in, da_q_norm, da_k_norm, da_lambda, da_sub_norm, da_w_out,
           ret_w_in, ret_log_alpha, ret_group_norm, ret_w_out,
           hy_w_in, hy_b_in, hy_conv_w, hy_conv_b, hy_ffn_w1, hy_ffn_b1, hy_ffn_w2, hy_ffn_b2,
           hy_sin_freq, hy_ffn_w3, hy_filter_bias, hy_w_out, hy_b_out,
           router_w, exp_w1, exp_w3, exp_w2):
    B = x.shape[0]
    depth = mod_w.shape[0]
    mods = _mod_all(c, c_ctx, mod_w, mod_b, B)
    for i in range(depth):
        last = i == depth - 1
        m, s = i % N_MIXERS, i // N_MIXERS
        mv = mods[i]
        if m == 0:
            lambda_init = 0.8 - 0.6 * math.exp(-0.3 * i)
            x, ctx = diff_attention_layer(x, ctx, mv, norm_mix_g[i], da_w_in[s], da_q_norm[s], da_k_norm[s],
                                          da_lambda[s], da_sub_norm[s], da_w_out[s], lambda_init, not last)
        elif m == 1:
            x, ctx = retention_layer(x, ctx, mv, norm_mix_g[i], ret_w_in[s], ret_log_alpha[s],
                                     ret_group_norm[s], ret_w_out[s], not last)
        else:
            hy_p = (hy_w_in[s], hy_b_in[s], hy_conv_w[s], hy_conv_b[s], hy_ffn_w1[s], hy_ffn_b1[s],
                    hy_ffn_w2[s], hy_ffn_b2[s], hy_sin_freq[s], hy_ffn_w3[s], hy_filter_bias[s],
                    hy_w_out[s], hy_b_out[s])
            x, ctx = hyena_layer(x, ctx, mv, norm_mix_g[i], hy_p, not last)
        w1, w3, w2 = exp_w1[i].astype(BF16), exp_w3[i].astype(BF16), exp_w2[i].astype(BF16)
        x = moe_layer(x, norm_ffn_g[i], mv["sh2_l"], mv["sc2_l"], mv["g2_l"], router_w[i], w1, w3, w2)
        if not last:
            ctx = moe_layer(ctx, norm_ffn_g[i], mv["sh2_c"], mv["sc2_c"], mv["g2_c"], router_w[i], w1, w3, w2)
    return x
```

```python
import functools
import math

import jax
import jax.numpy as jnp
import numpy as np
from jax import lax
from jax.experimental import pallas as pl
from jax.experimental.pallas import tpu as pltpu

F32 = jnp.float32
BF16 = jnp.bfloat16
I32 = jnp.int32

EPS = 1e-6
N_MOD = 6
ROPE_BASE = 10000.0
GRID_W = 64
N_MIXERS = 3

DA_HEAD_DIM = 64
DA_V_DIM = 128
RET_QK_DIM = 256
RET_V_DIM = 512
RET_CHUNK = 128

HY_ORDER = 2
HY_EMB = 33
HY_BANDS = (HY_EMB - 1) // 2
HY_TARGET = 1e-2
HY_FAST_DECAY_PCT = 0.3
HY_SLOW_DECAY_PCT = 1.5

N_EXPERTS = 16
EC_CAPACITY_FACTOR = 2

LANES = 128
VMEM_LIMIT_BYTES = 56 * 1024 * 1024


def _cparams(*sem):
    return pltpu.CompilerParams(dimension_semantics=sem, vmem_limit_bytes=VMEM_LIMIT_BYTES)


def _split_bf16(a):
    hi = a.astype(BF16)
    lo = (a - hi.astype(F32)).astype(BF16)
    return hi, lo


def _dot(a, b):
    return jnp.dot(a, b, preferred_element_type=F32)


def _dot3(a_hi, a_lo, b_hi, b_lo):
    return _dot(a_hi, b_hi) + _dot(a_lo, b_hi) + _dot(a_hi, b_lo)


def _modulate(x, g, sh, sc):
    ms = jnp.mean(x * x, axis=-1, keepdims=True)
    y = x * lax.rsqrt(ms + EPS) * g
    return y * (1.0 + sc) + sh


def _modvec_kernel(a_ref, w_ref, b_ref, o_ref):
    a = a_ref[...]
    a_hi, a_lo = _split_bf16(a * jax.nn.sigmoid(a))
    w_hi, w_lo = _split_bf16(w_ref[0])
    o_ref[0] = _dot3(a_hi, a_lo, w_hi, w_lo) + b_ref[0]


def mod_vectors(a, mod_w, mod_b):
    depth, d, n = mod_w.shape
    tn = 1024
    return pl.pallas_call(
        _modvec_kernel,
        grid=(depth, n // tn),
        in_specs=[pl.BlockSpec((8, d), lambda i, j: (0, 0)),
                  pl.BlockSpec((1, d, tn), lambda i, j: (i, 0, j)),
                  pl.BlockSpec((1, 1, tn), lambda i, j: (i, 0, j))],
        out_specs=pl.BlockSpec((1, 8, tn), lambda i, j: (i, 0, j)),
        out_shape=jax.ShapeDtypeStruct((depth, 8, n), F32),
        compiler_params=_cparams("parallel", "parallel"),
        name="mod_vectors",
    )(a, mod_w, mod_b.reshape(depth, 1, n))


def _mod_all(c, c_ctx, mod_w, mod_b, batch):
    d = c.shape[-1]
    a = jnp.zeros((8, d), F32).at[:batch].set(c).at[batch].set(c_ctx)
    mv = mod_vectors(a, mod_w, mod_b)
    names = ("sh", "sc", "g", "sh2", "sc2", "g2")
    layers = []
    for i in range(mod_w.shape[0]):
        parts = mv[i].reshape(8, N_MOD, d)
        layer = {}
        for k, nm in enumerate(names):
            layer[nm + "_l"] = parts[:batch, k][:, None, :]
            layer[nm + "_c"] = jnp.broadcast_to(parts[batch, k][None, None, :], (batch, 1, d))
        layers.append(layer)
    return layers


def _proj_kernel(x_ref, g_ref, sh_ref, sc_ref, w_ref, b_ref, *rest, epilogue, n_extra):
    extra, o_ref, h_scr = rest[:n_extra], rest[n_extra], rest[n_extra + 1]
    j = pl.program_id(2)

    @pl.when(j == 0)
    def _():
        h_scr[...] = _modulate(x_ref[0], g_ref[...], sh_ref[0], sc_ref[0]).astype(BF16)

    acc = _dot(h_scr[...], w_ref[...]) + b_ref[...]
    epilogue(acc, j, o_ref, *extra)


def _plain_epilogue(acc, j, o_ref):
    o_ref[0] = acc.astype(o_ref.dtype)


def mod_proj(x, g, sh, sc, w, bias, *, epilogue=_plain_epilogue, extra=(), extra_specs=(),
             out_dtype=BF16, tn=512, name="mod_proj"):
    B, n, K = x.shape
    N = w.shape[1]
    tm = min(512, n)
    tn = min(tn, N)
    kern = functools.partial(_proj_kernel, epilogue=epilogue, n_extra=len(extra))
    return pl.pallas_call(
        kern,
        grid=(B, n // tm, N // tn),
        in_specs=[pl.BlockSpec((1, tm, K), lambda b, i, j: (b, i, 0)),
                  pl.BlockSpec((1, K), lambda b, i, j: (0, 0)),
                  pl.BlockSpec((1, 1, K), lambda b, i, j: (b, 0, 0)),
                  pl.BlockSpec((1, 1, K), lambda b, i, j: (b, 0, 0)),
                  pl.BlockSpec((K, tn), lambda b, i, j: (0, j)),
                  pl.BlockSpec((1, tn), lambda b, i, j: (0, j)),
                  *extra_specs],
        out_specs=pl.BlockSpec((1, tm, tn), lambda b, i, j: (b, i, j)),
        out_shape=jax.ShapeDtypeStruct((B, n, N), out_dtype),
        scratch_shapes=[pltpu.VMEM((tm, K), BF16)],
        compiler_params=_cparams("parallel", "parallel", "arbitrary"),
        name=name,
    )(x, g.reshape(1, K), sh, sc, w, bias.reshape(1, N), *extra)


def _resmm_kernel(a_ref, w_ref, b_ref, r_ref, gt_ref, o_ref):
    acc = _dot(a_ref[0], w_ref[...]) + b_ref[...]
    o_ref[0] = r_ref[0] + gt_ref[0] * acc


def res_matmul(a, w, bias, res, gate):
    B, n, K = a.shape
    N = w.shape[1]
    tm = min(512, n)
    tn = min(512, N)
    return pl.pallas_call(
        _resmm_kernel,
        grid=(B, n // tm, N // tn),
        in_specs=[pl.BlockSpec((1, tm, K), lambda b, i, j: (b, i, 0)),
                  pl.BlockSpec((K, tn), lambda b, i, j: (0, j)),
                  pl.BlockSpec((1, tn), lambda b, i, j: (0, j)),
                  pl.BlockSpec((1, tm, tn), lambda b, i, j: (b, i, j)),
                  pl.BlockSpec((1, 1, tn), lambda b, i, j: (b, 0, j))],
        out_specs=pl.BlockSpec((1, tm, tn), lambda b, i, j: (b, i, j)),
        out_shape=jax.ShapeDtypeStruct((B, n, N), F32),
        compiler_params=_cparams("parallel", "parallel", "parallel"),
        name="res_matmul",
    )(a, w, bias.reshape(1, N), res, gate)


def _da_epilogue(acc, j, o_ref, cos_ref, sin_ref, qg_ref, kg_ref, gm_ref, *, tn, n_qk_tiles):
    @pl.when(j >= n_qk_tiles)
    def _():
        o_ref[0] = acc.astype(o_ref.dtype)

    @pl.when(j < n_qk_tiles)
    def _():
        gain = jnp.where(j < n_qk_tiles // 2, qg_ref[...], kg_ref[...])
        cos, sin = cos_ref[...], sin_ref[...]
        lane = lax.broadcasted_iota(I32, (1, LANES), 1)
        first_half = (lane % 32) < 16
        for s in range(tn // LANES):
            y = acc[:, s * LANES:(s + 1) * LANES]
            hi, lo = _split_bf16(y * y)
            gsum = _dot(hi, gm_ref[...]) + _dot(lo, gm_ref[...])
            yn = y * lax.rsqrt(gsum * (1.0 / DA_HEAD_DIM) + EPS) * gain
            partner = jnp.where(first_half, pltpu.roll(yn, LANES - 16, 1), pltpu.roll(yn, 16, 1))
            o_ref[0, :, s * LANES:(s + 1) * LANES] = (yn * cos + partner * sin).astype(o_ref.dtype)


def _attn_kernel(lam_ref, q_ref, kc_ref, vc_ref, *rest, has_lat, tk, n_chunks, out_scale):
    if has_lat:
        kl_ref, vl_ref, sg_ref, o_ref = rest
    else:
        sg_ref, o_ref = rest
    q = q_ref[0]
    tq = q.shape[0]
    lane = lax.broadcasted_iota(I32, (1, LANES), 1)
    zero = jnp.zeros_like(q)
    q2 = jnp.concatenate([jnp.where(lane < DA_HEAD_DIM, q, zero),
                          jnp.where(lane >= DA_HEAD_DIM, q, zero)], axis=0)

    def scores(k):
        return lax.dot_general(q2, k, (((1,), (1,)), ((), ())), preferred_element_type=F32)

    def absorb(s, v, carry):
        m, l, acc = carry
        m_new = jnp.maximum(m, jnp.max(s, axis=-1, keepdims=True))
        alpha = jnp.exp2(m - m_new)
        p = jnp.exp2(s - m_new)
        l_new = alpha * l + jnp.sum(p, axis=-1, keepdims=True)
        acc_new = alpha * acc + _dot(p.astype(BF16), v)
        return m_new, l_new, acc_new

    carry = (jnp.full((2 * tq, 1), -1e30, F32), jnp.zeros((2 * tq, 1), F32),
             jnp.zeros((2 * tq, LANES), F32))
    carry = absorb(scores(kc_ref[0]), vc_ref[0], carry)
    if has_lat:
        def body(c, carry):
            off = pl.multiple_of(c * tk, tk)
            return absorb(scores(kl_ref[0, pl.ds(off, tk), :]), vl_ref[0, pl.ds(off, tk), :], carry)
        carry = lax.fori_loop(0, n_chunks, body, carry, unroll=4 if n_chunks % 4 == 0 else 1)
    _, l, acc = carry
    o = acc / l
    o = o[:tq] - lam_ref[0] * o[tq:]
    ms = jnp.mean(o * o, axis=-1, keepdims=True)
    o_ref[0] = (o * lax.rsqrt(ms + EPS) * sg_ref[...] * out_scale).astype(o_ref.dtype)


def diff_attention_core(lam, qkv_q, qkv_c, qkv_l, sub_g, out_scale):
    B, nq, D3 = qkv_q.shape
    D = D3 // 3
    H = D // DA_V_DIM
    nc = qkv_c.shape[1]
    tq = min(256, nq)
    has_lat = qkv_l is not None
    in_specs = [pl.BlockSpec(memory_space=pltpu.SMEM),
                pl.BlockSpec((1, tq, LANES), lambda b, h, i: (b, i, h)),
                pl.BlockSpec((1, nc, LANES), lambda b, h, i: (b, 0, H + h)),
                pl.BlockSpec((1, nc, LANES), lambda b, h, i: (b, 0, 2 * H + h))]
    args = [lam, qkv_q, qkv_c, qkv_c]
    tk, n_chunks = 0, 0
    if has_lat:
        nl = qkv_l.shape[1]
        tk = min(512, nl)
        n_chunks = nl // tk
        in_specs += [pl.BlockSpec((1, nl, LANES), lambda b, h, i: (b, 0, H + h)),
                     pl.BlockSpec((1, nl, LANES), lambda b, h, i: (b, 0, 2 * H + h))]
        args += [qkv_l, qkv_l]
    in_specs.append(pl.BlockSpec((1, LANES), lambda b, h, i: (0, 0)))
    args.append(sub_g.reshape(1, LANES))
    kern = functools.partial(_attn_kernel, has_lat=has_lat, tk=tk, n_chunks=n_chunks,
                             out_scale=out_scale)
    return pl.pallas_call(
        kern,
        grid=(B, H, nq // tq),
        in_specs=in_specs,
        out_specs=pl.BlockSpec((1, tq, LANES), lambda b, h, i: (b, i, h)),
        out_shape=jax.ShapeDtypeStruct((B, nq, D), BF16),
        compiler_params=_cparams("parallel", "parallel", "arbitrary"),
        name="diff_attention",
    )(*args)


def _axial_rope_tables(n):
    f = DA_HEAD_DIM // 4
    inv = 1.0 / (ROPE_BASE ** (jnp.arange(f, dtype=F32) / f))
    rows = n // GRID_W
    row = jnp.repeat(jnp.arange(rows), GRID_W).astype(F32)
    col = jnp.tile(jnp.arange(GRID_W), rows).astype(F32)
    ang = jnp.stack([row[:, None] * inv, col[:, None] * inv], axis=1)
    cos, sin = jnp.cos(ang), jnp.sin(ang)
    cos64 = jnp.concatenate([cos, cos], axis=-1).reshape(n, DA_HEAD_DIM)
    sin64 = jnp.concatenate([-sin, sin], axis=-1).reshape(n, DA_HEAD_DIM)
    return jnp.tile(cos64, (1, 2)), jnp.tile(sin64, (1, 2))


def da_project(h_in, g, sh, sc, w_in_bf, q_g, k_g, cos, sin):
    B, n, D = h_in.shape
    tn = 512
    tm = min(512, n)
    gm = (np.arange(LANES)[:, None] // DA_HEAD_DIM == np.arange(LANES)[None, :] // DA_HEAD_DIM)
    gm = jnp.asarray(gm, BF16)
    qg = (jnp.tile(q_g.astype(F32), 2) * (DA_HEAD_DIM ** -0.5 * math.log2(math.e))).reshape(1, LANES)
    kg = jnp.tile(k_g.astype(F32), 2).reshape(1, LANES)
    extra = (cos, sin, qg, kg, gm)
    extra_specs = (pl.BlockSpec((tm, LANES), lambda b, i, j: (i, 0)),
                   pl.BlockSpec((tm, LANES), lambda b, i, j: (i, 0)),
                   pl.BlockSpec((1, LANES), lambda b, i, j: (0, 0)),
                   pl.BlockSpec((1, LANES), lambda b, i, j: (0, 0)),
                   pl.BlockSpec((LANES, LANES), lambda b, i, j: (0, 0)))
    epi = functools.partial(_da_epilogue, tn=tn, n_qk_tiles=2 * D // tn)
    return mod_proj(h_in, g, sh, sc, w_in_bf, jnp.zeros((3 * D,), F32), epilogue=epi, extra=extra,
                    extra_specs=extra_specs, tn=tn, name="da_project")


def diff_attention_layer(x, ctx, mv, norm_g, w_in, q_g, k_g, lam, sub_g, w_out, lambda_init, need_ctx):
    B, L, D = x.shape
    nc = ctx.shape[1]
    w_in_bf = w_in.astype(BF16)
    w_out_bf = w_out.astype(BF16)
    cos_l, sin_l = _axial_rope_tables(L)
    cos_c, sin_c = jnp.ones((nc, LANES), F32), jnp.zeros((nc, LANES), F32)
    qkv_l = da_project(x, norm_g, mv["sh_l"], mv["sc_l"], w_in_bf, q_g, k_g, cos_l, sin_l)
    qkv_c = da_project(ctx, norm_g, mv["sh_c"], mv["sc_c"], w_in_bf, q_g, k_g, cos_c, sin_c)
    lamf = lam.astype(F32)
    lam_val = (jnp.exp(jnp.sum(lamf[0] * lamf[1])) - jnp.exp(jnp.sum(lamf[2] * lamf[3]))
               + lambda_init).reshape(1)
    out_scale = 1.0 - lambda_init
    o_l = diff_attention_core(lam_val, qkv_l, qkv_c, qkv_l, sub_g, out_scale)
    zb = jnp.zeros((D,), F32)
    x = res_matmul(o_l, w_out_bf, zb, x, mv["g_l"])
    if need_ctx:
        o_c = diff_attention_core(lam_val, qkv_c, qkv_c, None, sub_g, out_scale)
        ctx = res_matmul(o_c, w_out_bf, zb, ctx, mv["g_c"])
    return x, ctx


def _router_kernel(x_ref, g_ref, sh_ref, sc_ref, whi_ref, wlo_ref, o_ref):
    h = _modulate(x_ref[0], g_ref[...], sh_ref[0], sc_ref[0])
    o_ref[0] = _router_affinity(h, whi_ref[...], wlo_ref[...])


def _router_affinity(h, w_hi, w_lo):
    h_hi, h_lo = _split_bf16(h)
    logits = _dot3(h_hi, h_lo, w_hi, w_lo)
    lane = lax.broadcasted_iota(I32, (1, LANES), 1)
    logits = jnp.where(lane < N_EXPERTS, logits, -1e30)
    z = jnp.exp(logits - jnp.max(logits, axis=-1, keepdims=True))
    return z / jnp.sum(z, axis=-1, keepdims=True)


def router_affinity(x, g, sh, sc, w_hi, w_lo):
    B, n, D = x.shape
    tm = min(512, n)
    return pl.pallas_call(
        _router_kernel,
        grid=(B, n // tm),
        in_specs=[pl.BlockSpec((1, tm, D), lambda b, i: (b, i, 0)),
                  pl.BlockSpec((1, D), lambda b, i: (0, 0)),
                  pl.BlockSpec((1, 1, D), lambda b, i: (b, 0, 0)),
                  pl.BlockSpec((1, 1, D), lambda b, i: (b, 0, 0)),
                  pl.BlockSpec((D, LANES), lambda b, i: (0, 0)),
                  pl.BlockSpec((D, LANES), lambda b, i: (0, 0))],
        out_specs=pl.BlockSpec((1, tm, LANES), lambda b, i: (b, i, 0)),
        out_shape=jax.ShapeDtypeStruct((B, n, LANES), F32),
        compiler_params=_cparams("parallel", "parallel"),
        name="router",
    )(x, g.reshape(1, D), sh, sc, w_hi, w_lo)


def _ffn_kernel(idx_ref, x_hbm, acc_in, g_ref, sh_ref, sc_ref, g2_ref, rhi_ref, rlo_ref,
                w1_ref, w3_ref, w2_ref, o_hbm, xbuf, abuf, sem, *, n, cap, tm):
    del acc_in
    b, e, j = pl.program_id(0), pl.program_id(1), pl.program_id(2)
    n_exp = pl.num_programs(1)
    base = (b * n_exp + e) * cap + j * tm
    row0 = b * n

    def gather(k, c):
        row = row0 + idx_ref[base + k]
        pltpu.make_async_copy(x_hbm.at[pl.ds(row, 1)], xbuf.at[pl.ds(k, 1)], sem.at[0]).start()
        pltpu.make_async_copy(o_hbm.at[pl.ds(row, 1)], abuf.at[pl.ds(k, 1)], sem.at[1]).start()
        return c

    lax.fori_loop(0, tm, gather, 0)
    pltpu.make_async_copy(x_hbm.at[pl.ds(0, tm)], xbuf, sem.at[0]).wait()

    h = _modulate(xbuf[...], g_ref[...], sh_ref[0], sc_ref[0])
    aff = _router_affinity(h, rhi_ref[...], rlo_ref[...])
    lane = lax.broadcasted_iota(I32, (1, LANES), 1)
    gate = jnp.sum(jnp.where(lane == e, aff, 0.0), axis=-1, keepdims=True)
    hb = h.astype(BF16)
    a1 = _dot(hb, w1_ref[0])
    a3 = _dot(hb, w3_ref[0])
    hid = (a1 * jax.nn.sigmoid(a1) * a3).astype(BF16)
    y = _dot(hid, w2_ref[0])
    upd = g2_ref[0] * (gate * y)

    pltpu.make_async_copy(o_hbm.at[pl.ds(0, tm)], abuf, sem.at[1]).wait()
    abuf[...] = abuf[...] + upd

    def scatter(k, c):
        row = row0 + idx_ref[base + k]
        pltpu.make_async_copy(abuf.at[pl.ds(k, 1)], o_hbm.at[pl.ds(row, 1)], sem.at[2]).start()
        return c

    lax.fori_loop(0, tm, scatter, 0)
    pltpu.make_async_copy(abuf, o_hbm.at[pl.ds(0, tm)], sem.at[2]).wait()


def expert_ffn(idx, x, g, sh, sc, g2, r_hi, r_lo, w1, w3, w2):
    B, n, D = x.shape
    E, cap = idx.shape[1], idx.shape[2]
    F = w1.shape[-1]
    tm = min(256, cap)
    x2 = x.reshape(B * n, D)
    kern = functools.partial(_ffn_kernel, n=n, cap=cap, tm=tm)
    vec = lambda: pl.BlockSpec((1, 1, D), lambda b, e, j, idx: (b, 0, 0))
    grid_spec = pltpu.PrefetchScalarGridSpec(
        num_scalar_prefetch=1,
        grid=(B, E, cap // tm),
        in_specs=[pl.BlockSpec(memory_space=pl.ANY),
                  pl.BlockSpec(memory_space=pl.ANY),
                  pl.BlockSpec((1, D), lambda b, e, j, idx: (0, 0)),
                  vec(), vec(), vec(),
                  pl.BlockSpec((D, LANES), lambda b, e, j, idx: (0, 0)),
                  pl.BlockSpec((D, LANES), lambda b, e, j, idx: (0, 0)),
                  pl.BlockSpec((1, D, F), lambda b, e, j, idx: (e, 0, 0)),
                  pl.BlockSpec((1, D, F), lambda b, e, j, idx: (e, 0, 0)),
                  pl.BlockSpec((1, F, D), lambda b, e, j, idx: (e, 0, 0))],
        out_specs=pl.BlockSpec(memory_space=pl.ANY),
        scratch_shapes=[pltpu.VMEM((tm, D), F32), pltpu.VMEM((tm, D), F32),
                        pltpu.SemaphoreType.DMA((3,))])
    out = pl.pallas_call(
        kern,
        grid_spec=grid_spec,
        out_shape=jax.ShapeDtypeStruct((B * n, D), F32),
        input_output_aliases={2: 0},
        compiler_params=_cparams("arbitrary", "arbitrary", "arbitrary"),
        name="expert_ffn",
    )(idx.reshape(-1), x2, x2, g.reshape(1, D), sh, sc, g2, r_hi, r_lo, w1, w3, w2)
    return out.reshape(B, n, D)


TOPK_ROWS = 128


def _topk_kernel(a_ref, idx_ref, *, k, capp):
    a = a_ref[0, 0]
    bits = lax.bitcast_convert_type(a, I32)
    row_i = lax.broadcasted_iota(I32, (TOPK_ROWS, LANES), 0)
    lane_i = lax.broadcasted_iota(I32, (TOPK_ROWS, LANES), 1)
    tok = row_i * LANES + lane_i

    def count(ind):
        return jnp.sum(jnp.sum(ind, axis=1, keepdims=True), axis=0, keepdims=True)

    thr = jnp.zeros((1, 1), I32)
    for bit in range(30, -1, -1):
        cand = thr | (1 << bit)
        thr = jnp.where(count(jnp.where(bits >= cand, 1, 0)) >= k, cand, thr)
    gt = jnp.where(bits > thr, 1, 0)
    eq = jnp.where(bits == thr, 1, 0)
    need = k - count(gt)
    tcut = jnp.zeros((1, 1), I32)
    for bit in range((TOPK_ROWS * LANES).bit_length() - 2, -1, -1):
        cand = tcut | (1 << bit)
        tcut = jnp.where(count(jnp.where(tok < cand, eq, 0)) < need, cand, tcut)
    sel = gt + jnp.where(tok <= tcut, eq, 0)

    m = sel.astype(F32)
    cnt = jnp.sum(m, axis=1, keepdims=True)
    tri = jnp.where(lax.broadcasted_iota(I32, (TOPK_ROWS, TOPK_ROWS), 0)
                    >= lax.broadcasted_iota(I32, (TOPK_ROWS, TOPK_ROWS), 1), 1.0, 0.0).astype(BF16)
    incl = _dot(tri, jnp.broadcast_to(cnt, (TOPK_ROWS, LANES)).astype(BF16))[:, :1]
    slot = lax.broadcasted_iota(I32, (1, capp), 1).astype(F32)
    before = incl <= slot
    row_of = jnp.sum(jnp.where(before, 1.0, 0.0), axis=0, keepdims=True)
    base = jnp.sum(jnp.where(before, cnt, 0.0), axis=0, keepdims=True)
    rows_col = lax.broadcasted_iota(I32, (TOPK_ROWS, 1), 0).astype(F32)
    onehot = jnp.where(rows_col == row_of, 1.0, 0.0).astype(BF16)
    picked = _dot(m.T.astype(BF16), onehot)
    cum = _dot(tri, picked.astype(BF16))
    local = jnp.sum(jnp.where(cum <= slot - base, 1.0, 0.0), axis=0, keepdims=True)
    idx_ref[0, 0] = (row_of * LANES + local).astype(I32)


def expert_choice_topk(aff, cap):
    B, E, n = aff.shape
    n_pad = TOPK_ROWS * LANES
    assert n <= n_pad and TOPK_ROWS == LANES
    a = jnp.pad(aff, ((0, 0), (0, 0), (0, n_pad - n)), constant_values=-1.0)
    a = a.reshape(B, E, TOPK_ROWS, LANES)
    capp = max(cap, LANES)
    out = pl.pallas_call(
        functools.partial(_topk_kernel, k=cap, capp=capp),
        grid=(B, E),
        in_specs=[pl.BlockSpec((1, 1, TOPK_ROWS, LANES), lambda b, e: (b, e, 0, 0))],
        out_specs=pl.BlockSpec((1, 1, 1, capp), lambda b, e: (b, e, 0, 0)),
        out_shape=jax.ShapeDtypeStruct((B, E, 1, capp), I32),
        compiler_params=_cparams("parallel", "parallel"),
        name="expert_topk",
    )(a)
    return out[:, :, 0, :cap]


def moe_layer(x, g, sh, sc, g2, router_w, w1, w3, w2):
    B, n, D = x.shape
    E = router_w.shape[1]
    cap = EC_CAPACITY_FACTOR * n // E
    rw = jnp.pad(router_w.astype(F32), ((0, 0), (0, LANES - E)))
    r_hi, r_lo = _split_bf16(rw)
    aff = router_affinity(x, g, sh, sc, r_hi, r_lo)
    idx = expert_choice_topk(aff[:, :, :E].transpose(0, 2, 1), cap)
    return expert_ffn(idx, x, g, sh, sc, g2, r_hi, r_lo, w1, w3, w2)


def _ret_epilogue(acc, j, o_ref, cos_ref, sin_ref, *, tn, n_q_tiles):
    @pl.when(j >= 2 * n_q_tiles)
    def _():
        o_ref[0] = acc.astype(o_ref.dtype)

    @pl.when(j < 2 * n_q_tiles)
    def _():
        scale = jnp.where(j < n_q_tiles, 1.0, RET_QK_DIM ** -0.5)
        cos, sin = cos_ref[...] * scale, sin_ref[...] * scale
        f = RET_QK_DIM // 2
        for s in range(tn // RET_QK_DIM):
            lo, mid, hi = s * RET_QK_DIM, s * RET_QK_DIM + f, (s + 1) * RET_QK_DIM
            x1, x2 = acc[:, lo:mid], acc[:, mid:hi]
            o_ref[0, :, lo:mid] = (x1 * cos - x2 * sin).astype(o_ref.dtype)
            o_ref[0, :, mid:hi] = (x2 * cos + x1 * sin).astype(o_ref.dtype)


def _retention_kernel(lg_ref, qf_ref, kf_ref, vf_ref, qb_ref, kb_ref, vb_ref, s0f_ref, s0b_ref,
                      of_ref, ob_ref, sf_ref, sb_ref, stf, stb, *, T):
    h, c = pl.program_id(1), pl.program_id(2)

    @pl.when(c == 0)
    def _():
        stf[...] = s0f_ref[0, 0]
        stb[...] = s0b_ref[0, 0]

    ii = lax.broadcasted_iota(I32, (T, T), 0).astype(F32)
    jj = lax.broadcasted_iota(I32, (T, T), 1).astype(F32)
    row = lax.broadcasted_iota(I32, (T, 1), 0).astype(F32)

    def one(q_ref, k_ref, v_ref, st, o_ref, lg, forward):
        q, k, v = q_ref[0], k_ref[0], v_ref[0]
        d = ii - jj if forward else jj - ii
        mask = d >= 0 if forward else d > 0
        dmask = jnp.where(mask, jnp.exp(jnp.where(mask, d, 0.0) * lg), 0.0)
        xi = jnp.exp(((row + 1.0) if forward else (T - row)) * lg)
        zeta = jnp.exp(((T - 1.0 - row) if forward else row) * lg)
        inner = lax.dot_general(q, k, (((1,), (1,)), ((), ())), preferred_element_type=F32) * dmask
        o = _dot(inner.astype(BF16), v) + _dot(q, st[...].astype(BF16)) * xi
        o_ref[0] = o.astype(o_ref.dtype)
        kz = (k.astype(F32) * zeta).astype(BF16)
        decay = jnp.exp(jnp.full((1, 1), T, F32) * lg)
        st[...] = st[...] * decay + lax.dot_general(kz, v, (((0,), (0,)), ((), ())),
                                                    preferred_element_type=F32)

    one(qf_ref, kf_ref, vf_ref, stf, of_ref, lg_ref[0, h], True)
    one(qb_ref, kb_ref, vb_ref, stb, ob_ref, lg_ref[1, h], False)

    @pl.when(c == pl.num_programs(2) - 1)
    def _():
        sf_ref[0, 0] = stf[...]
        sb_ref[0, 0] = stb[...]


def retention_core(log_g, p, s0f, s0b):
    B, n, _ = p.shape
    H, dk, dv = s0f.shape[1], s0f.shape[2], s0f.shape[3]
    T = min(256, n)
    nc = n // T
    kq, kk, kv = 0, (H * dk) // dk, (2 * H * dk) // dv
    fwd = lambda off: (lambda b, h, c: (b, c, off + h))
    bwd = lambda off: (lambda b, h, c: (b, nc - 1 - c, off + h))
    st_spec = pl.BlockSpec((1, 1, dk, dv), lambda b, h, c: (b, h, 0, 0))
    out_sd = jax.ShapeDtypeStruct((B, n, H * dv), BF16)
    st_sd = jax.ShapeDtypeStruct((B, H, dk, dv), F32)
    return pl.pallas_call(
        functools.partial(_retention_kernel, T=T),
        grid=(B, H, nc),
        in_specs=[pl.BlockSpec(memory_space=pltpu.SMEM),
                  pl.BlockSpec((1, T, dk), fwd(kq)), pl.BlockSpec((1, T, dk), fwd(kk)),
                  pl.BlockSpec((1, T, dv), fwd(kv)),
                  pl.BlockSpec((1, T, dk), bwd(kq)), pl.BlockSpec((1, T, dk), bwd(kk)),
                  pl.BlockSpec((1, T, dv), bwd(kv)),
                  st_spec, st_spec],
        out_specs=[pl.BlockSpec((1, T, dv), fwd(0)), pl.BlockSpec((1, T, dv), bwd(0)), st_spec, st_spec],
        out_shape=[out_sd, out_sd, st_sd, st_sd],
        scratch_shapes=[pltpu.VMEM((dk, dv), F32), pltpu.VMEM((dk, dv), F32)],
        compiler_params=_cparams("parallel", "parallel", "arbitrary"),
        name="retention",
    )(log_g, p, p, p, p, p, p, s0f, s0b)


def _ret_finish_kernel(of_ref, ob_ref, g_ref, gn_ref, z_ref, *, dv):
    for s in range(of_ref.shape[-1] // dv):
        sl = slice(s * dv, (s + 1) * dv)
        y = of_ref[0, :, sl].astype(F32) + ob_ref[0, :, sl].astype(F32)
        ms = jnp.mean(y * y, axis=-1, keepdims=True)
        y = y * lax.rsqrt(ms + EPS) * gn_ref[...]
        g = g_ref[0, :, sl].astype(F32)
        z_ref[0, :, sl] = (g * jax.nn.sigmoid(g) * y).astype(z_ref.dtype)


def retention_finish(of, ob, p, gn_g):
    B, n, N = of.shape
    dv = gn_g.shape[0]
    tm = min(512, n)
    blk = lambda off: pl.BlockSpec((1, tm, N), lambda b, i: (b, i, off))
    return pl.pallas_call(
        functools.partial(_ret_finish_kernel, dv=dv),
        grid=(B, n // tm),
        in_specs=[blk(0), blk(0), blk(p.shape[-1] // N - 1), pl.BlockSpec((1, dv), lambda b, i: (0, 0))],
        out_specs=blk(0),
        out_shape=jax.ShapeDtypeStruct((B, n, N), BF16),
        compiler_params=_cparams("parallel", "parallel"),
        name="retention_finish",
    )(of, ob, p, gn_g.astype(F32).reshape(1, dv))


def retention_layer(x, ctx, mv, norm_g, w_in, log_alpha, gn_g, w_out, need_ctx):
    B, L, D = x.shape
    nc = ctx.shape[1]
    dk, dv = RET_QK_DIM, RET_V_DIM
    H = D // dk
    w_in_bf = w_in.astype(BF16)
    w_out_bf = w_out.astype(BF16)
    zb = jnp.zeros((w_in.shape[1],), F32)
    log_g = jnp.log1p(-jnp.exp(log_alpha.astype(F32)))
    f = dk // 2
    inv = 1.0 / (ROPE_BASE ** jnp.linspace(0.0, 1.0, f, dtype=F32))
    ang = jnp.arange(L, dtype=F32)[:, None] * inv
    tn = 512

    def project(h_in, sh, sc, cos, sin):
        tm = min(512, h_in.shape[1])
        specs = (pl.BlockSpec((tm, f), lambda b, i, j: (i, 0)), pl.BlockSpec((tm, f), lambda b, i, j: (i, 0)))
        epi = functools.partial(_ret_epilogue, tn=tn, n_q_tiles=H * dk // tn)
        return mod_proj(h_in, norm_g, sh, sc, w_in_bf, zb, epilogue=epi, extra=(cos, sin),
                        extra_specs=specs, tn=tn, name="ret_project")

    p_c = project(ctx, mv["sh_c"], mv["sc_c"], jnp.ones((nc, f), F32), jnp.zeros((nc, f), F32))
    p_l = project(x, mv["sh_l"], mv["sc_l"], jnp.cos(ang), jnp.sin(ang))
    zero = jnp.zeros((B, H, dk, dv), F32)
    of_c, ob_c, s_f, s_b = retention_core(log_g, p_c, zero, zero)
    of_l, ob_l, _, _ = retention_core(log_g, p_l, s_f, s_b)
    zd = jnp.zeros((D,), F32)
    x = res_matmul(retention_finish(of_l, ob_l, p_l, gn_g), w_out_bf, zd, x, mv["g_l"])
    if need_ctx:
        ctx = res_matmul(retention_finish(of_c, ob_c, p_c, gn_g), w_out_bf, zd, ctx, mv["g_c"])
    return x, ctx


def _hyena_filter_spectra(n, w1, b1, w2, b2, freq, w3):
    t = jnp.linspace(0.0, 1.0, n, dtype=F32)[:, None]
    w = 2.0 * math.pi * jnp.arange(n, dtype=F32)[:, None] / n
    f = jnp.linspace(1e-4, HY_BANDS - 1, HY_BANDS, dtype=F32)[None]
    z = jnp.concatenate([t, jnp.cos(f * w), -jnp.sin(f * w)], axis=-1)
    hp = lax.Precision.HIGHEST
    a = jnp.sin(freq[0].astype(F32) * (jnp.dot(z, w1.astype(F32), precision=hp) + b1.astype(F32)))
    a = jnp.sin(freq[1].astype(F32) * (jnp.dot(a, w2.astype(F32), precision=hp) + b2.astype(F32)))
    C = w3.shape[-1] // (2 * HY_ORDER)
    h = jnp.dot(a, w3.astype(F32), precision=hp).reshape(n, HY_ORDER, 2, C)
    max_decay = math.log(HY_TARGET) / HY_FAST_DECAY_PCT
    min_decay = math.log(HY_TARGET) / HY_SLOW_DECAY_PCT
    deltas = jnp.abs(jnp.linspace(min_decay, max_decay, C, dtype=F32))
    h = h * jnp.exp(-t * deltas)[:, None, None, :]
    buf = jnp.concatenate([h[:, :, 0], jnp.zeros((1, HY_ORDER, C), F32),
                           jnp.flip(h[1:, :, 1], axis=0)], axis=0)
    buf = buf / jnp.sum(jnp.abs(buf), axis=0, keepdims=True)
    return jnp.fft.rfft(buf, axis=0)


def _short_conv_kernel(u_ref, prev_ref, next_ref, w_ref, b_ref, o_ref):
    i, ni = pl.program_id(1), pl.num_programs(1)
    u = u_ref[0]
    tm = u.shape[0]
    row = lax.broadcasted_iota(I32, (tm, 1), 0)
    prev_row = jnp.where(i > 0, prev_ref[0, 7:8, :], 0.0)
    next_row = jnp.where(i < ni - 1, next_ref[0, 0:1, :], 0.0)
    before = jnp.where(row == 0, prev_row, pltpu.roll(u, 1, 0))
    after = jnp.where(row == tm - 1, next_row, pltpu.roll(u, tm - 1, 0))
    w = w_ref[...]
    o_ref[0, 0] = (before * w[0:1] + u * w[1:2] + after * w[2:3] + b_ref[...]).astype(o_ref.dtype)


def short_conv_split(u, conv_w, conv_b):
    B, n, C3 = u.shape
    C = C3 // 3
    tm = min(512, n)
    r8 = tm // 8
    return pl.pallas_call(
        _short_conv_kernel,
        grid=(B, n // tm, 3),
        in_specs=[pl.BlockSpec((1, tm, C), lambda b, i, j: (b, i, j)),
                  pl.BlockSpec((1, 8, C), lambda b, i, j: (b, jnp.maximum(i * r8 - 1, 0), j)),
                  pl.BlockSpec((1, 8, C), lambda b, i, j: (b, jnp.minimum((i + 1) * r8, n // 8 - 1), j)),
                  pl.BlockSpec((8, C), lambda b, i, j: (0, j)),
                  pl.BlockSpec((1, C), lambda b, i, j: (0, j))],
        out_specs=pl.BlockSpec((1, 1, tm, C), lambda b, i, j: (j, b, i, 0)),
        out_shape=jax.ShapeDtypeStruct((3, B, n, C), BF16),
        compiler_params=_cparams("parallel", "parallel", "parallel"),
        name="hy_short_conv",
    )(u, u, u, jnp.pad(conv_w.astype(F32), ((0, 5), (0, 0))), conv_b.astype(F32).reshape(1, C3))


FFT_N1 = 128


def _stack_complex(re, im):
    return jnp.concatenate([jnp.concatenate([re, -im], axis=-1),
                            jnp.concatenate([im, re], axis=-1)], axis=-2)


def _phase(k, period):
    a = (2.0 * math.pi / period) * (k % period).astype(F32)
    return jnp.cos(a), jnp.sin(a)


def _fft_a_kernel(z_ref, m_ref, o_ref):
    z = z_ref[...]
    o_ref[...] = _dot(m_ref[...], z.reshape(2 * z.shape[1], z.shape[2])).astype(o_ref.dtype)


def _fft_c_kernel(a_ref, gf_ref, gi_ref, h_ref, o_ref):
    n2, C = a_ref.shape[2], a_ref.shape[3]
    x = _dot(gf_ref[0], a_ref[:, 0].reshape(2 * n2, C))
    xr, xi = x[:n2], x[n2:]
    hr, hi = h_ref[0, 0], h_ref[0, 1]
    y = jnp.concatenate([xr * hr - xi * hi, xr * hi + xi * hr], axis=0).astype(BF16)
    o_ref[:, 0] = _dot(gi_ref[0], y).astype(o_ref.dtype).reshape(2, n2, C)


def _fft_b_kernel(b_ref, m_ref, u_ref, xg_ref, bias_ref, o_ref):
    s1, tc = u_ref.shape[1], u_ref.shape[2]
    y = _dot(m_ref[...], b_ref[...]).reshape(2, s1, tc)
    o_ref[...] = (xg_ref[...].astype(F32) * (y + u_ref[...].astype(F32) * bias_ref[...])).astype(o_ref.dtype)


def _dense_conv_kernel(u_ref, mf_ref, mi_ref, h_ref, xg_ref, bias_ref, o_ref):
    n, tc = u_ref.shape[1], u_ref.shape[2]
    u = u_ref[...]
    x = _dot(mf_ref[...], u.reshape(2 * n, tc))
    N = x.shape[0] // 2
    xr, xi = x[:N], x[N:]
    hr, hi = h_ref[0], h_ref[1]
    y = jnp.concatenate([xr * hr - xi * hi, xr * hi + xi * hr], axis=0).astype(BF16)
    out = _dot(mi_ref[...], y).reshape(2, n, tc)
    o_ref[...] = (xg_ref[...].astype(F32) * (out + u.astype(F32) * bias_ref[...])).astype(o_ref.dtype)


def long_conv_gate(u, xg, hfull, bias):
    B, n, C = u.shape
    assert B == 2
    N = 2 * n
    hr, hi = jnp.real(hfull).astype(F32), jnp.imag(hfull).astype(F32)
    bias3 = bias.astype(F32).reshape(1, 1, C)
    if n <= 512:
        f, t = jnp.arange(N, dtype=I32)[:, None], jnp.arange(n, dtype=I32)[None, :]
        c, s = _phase(f * t, N)
        mf = _stack_complex(c, -s).astype(BF16)
        mi = _stack_complex(c.T / N, s.T / N).astype(BF16)
        tc = 256
        blk = pl.BlockSpec((2, n, tc), lambda j: (0, 0, j))
        return pl.pallas_call(
            _dense_conv_kernel,
            grid=(C // tc,),
            in_specs=[blk, pl.BlockSpec((2 * N, 2 * n), lambda j: (0, 0)),
                      pl.BlockSpec((2 * n, 2 * N), lambda j: (0, 0)),
                      pl.BlockSpec((2, N, tc), lambda j: (0, 0, j)), blk,
                      pl.BlockSpec((1, 1, tc), lambda j: (0, 0, j))],
            out_specs=blk,
            out_shape=jax.ShapeDtypeStruct((2, n, C), BF16),
            compiler_params=_cparams("parallel"),
            name="hy_dense_conv",
        )(u, mf, mi, jnp.stack([hr, hi]), xg, bias3)

    N1, S1 = FFT_N1, FFT_N1 // 2
    N2 = N // N1
    cols = N2 * C
    tc = 2 * C
    f1 = jnp.arange(N1, dtype=I32)
    c, s = _phase(f1[:, None] * jnp.arange(S1, dtype=I32)[None, :], N1)
    m_a = _stack_complex(c, -s).astype(BF16)
    m_b = _stack_complex(c.T / N, s.T / N).astype(BF16)
    f2 = jnp.arange(N2, dtype=I32)
    c, s = _phase(f2[None, None, :] * (f1[:, None, None] + N1 * f2[None, :, None]), N)
    g_f = _stack_complex(c, -s).astype(BF16)
    ct, st = jnp.swapaxes(c, 1, 2), jnp.swapaxes(s, 1, 2)
    g_i = _stack_complex(ct, st).astype(BF16)
    hperm = jnp.stack([hr.reshape(N2, N1, C).transpose(1, 0, 2), hi.reshape(N2, N1, C).transpose(1, 0, 2)],
                      axis=1)
    zb = pl.BlockSpec((2, S1, tc), lambda j: (0, 0, j))
    a = pl.pallas_call(
        _fft_a_kernel,
        grid=(cols // tc,),
        in_specs=[zb, pl.BlockSpec((2 * N1, 2 * S1), lambda j: (0, 0))],
        out_specs=pl.BlockSpec((2 * N1, tc), lambda j: (0, j)),
        out_shape=jax.ShapeDtypeStruct((2 * N1, cols), BF16),
        compiler_params=_cparams("parallel"),
        name="hy_fft_a",
    )(u.reshape(2, S1, cols), m_a)
    ab = pl.BlockSpec((2, 1, N2, C), lambda i: (0, i, 0, 0))
    gb = pl.BlockSpec((1, 2 * N2, 2 * N2), lambda i: (i, 0, 0))
    bm = pl.pallas_call(
        _fft_c_kernel,
        grid=(N1,),
        in_specs=[ab, gb, gb, pl.BlockSpec((1, 2, N2, C), lambda i: (i, 0, 0, 0))],
        out_specs=ab,
        out_shape=jax.ShapeDtypeStruct((2, N1, N2, C), BF16),
        compiler_params=_cparams("parallel"),
        name="hy_fft_c",
    )(a.reshape(2, N1, N2, C), g_f, g_i, hperm)
    out = pl.pallas_call(
        _fft_b_kernel,
        grid=(cols // tc,),
        in_specs=[pl.BlockSpec((2 * N1, tc), lambda j: (0, j)),
                  pl.BlockSpec((2 * S1, 2 * N1), lambda j: (0, 0)), zb, zb,
                  pl.BlockSpec((1, 1, tc), lambda j: (0, 0, 0))],
        out_specs=zb,
        out_shape=jax.ShapeDtypeStruct((2, S1, cols), BF16),
        compiler_params=_cparams("parallel"),
        name="hy_fft_b",
    )(bm.reshape(2 * N1, cols), m_b, u.reshape(2, S1, cols), xg.reshape(2, S1, cols),
      jnp.tile(bias3, (1, 1, tc // C)))
    return out.reshape(2, n, C)


def hyena_layer(x, ctx, mv, norm_g, hy_p, need_ctx):
    (w_in, b_in, conv_w, conv_b, w1, b1, w2, b2, freq, w3, fbias, w_out, b_out) = hy_p
    w_in_bf = w_in.astype(BF16)
    w_out_bf = w_out.astype(BF16)

    def mix(h_in, sh, sc, res, gate):
        n = h_in.shape[1]
        u = mod_proj(h_in, norm_g, sh, sc, w_in_bf, b_in.astype(F32), out_dtype=F32, name="hy_project")
        v, x1, x2 = short_conv_split(u, conv_w, conv_b)
        hf = _hyena_filter_spectra(n, w1, b1, w2, b2, freq, w3)
        hfull = jnp.concatenate([hf, jnp.conj(jnp.flip(hf[1:n], axis=0))], axis=0)
        z = long_conv_gate(v, x1, hfull[:, 0], fbias[0])
        z = long_conv_gate(z, x2, hfull[:, 1], fbias[1])
        return res_matmul(z, w_out_bf, b_out.astype(F32), res, gate)

    x = mix(x, mv["sh_l"], mv["sc_l"], x, mv["g_l"])
    if need_ctx:
        ctx = mix(ctx, mv["sh_c"], mv["sc_c"], ctx, mv["g_c"])
    return x, ctx


def kernel(x, c, ctx, c_ctx, norm_mix_g, norm_ffn_g, mod_w, mod_b,
           da_w_in, da_q_norm, da_k_norm, da_lambda, da_sub_norm, da_w_out,
           ret_w_in, ret_log_alpha, ret_group_norm, ret_w_out,
           hy_w_in, hy_b_in, hy_conv_w, hy_conv_b, hy_ffn_w1, hy_ffn_b1, hy_ffn_w2, hy_ffn_b2,
           hy_sin_freq, hy_ffn_w3, hy_filter_bias, hy_w_out, hy_b_out,
           router_w, exp_w1, exp_w3, exp_w2):
    B = x.shape[0]
    depth = mod_w.shape[0]
    mods = _mod_all(c, c_ctx, mod_w, mod_b, B)
    for i in range(depth):
        last = i == depth - 1
        m, s = i % N_MIXERS, i // N_MIXERS
        mv = mods[i]
        if m == 0:
            lambda_init = 0.8 - 0.6 * math.exp(-0.3 * i)
            x, ctx = diff_attention_layer(x, ctx, mv, norm_mix_g[i], da_w_in[s], da_q_norm[s], da_k_norm[s],
                                          da_lambda[s], da_sub_norm[s], da_w_out[s], lambda_init, not last)
        elif m == 1:
            x, ctx = retention_layer(x, ctx, mv, norm_mix_g[i], ret_w_in[s], ret_log_alpha[s],
                                     ret_group_norm[s], ret_w_out[s], not last)
        else:
            hy_p = (hy_w_in[s], hy_b_in[s], hy_conv_w[s], hy_conv_b[s], hy_ffn_w1[s], hy_ffn_b1[s],
                    hy_ffn_w2[s], hy_ffn_b2[s], hy_sin_freq[s], hy_ffn_w3[s], hy_filter_bias[s],
                    hy_w_out[s], hy_b_out[s])
            x, ctx = hyena_layer(x, ctx, mv, norm_mix_g[i], hy_p, not last)
        w1, w3, w2 = exp_w1[i].astype(BF16), exp_w3[i].astype(BF16), exp_w2[i].astype(BF16)
        x = moe_layer(x, norm_ffn_g[i], mv["sh2_l"], mv["sc2_l"], mv["g2_l"], router_w[i], w1, w3, w2)
        if not last:
            ctx = moe_layer(ctx, norm_ffn_g[i], mv["sh2_c"], mv["sc2_c"], mv["g2_c"], router_w[i], w1, w3, w2)
    return x
```

```python
import functools
import math

import jax
import jax.numpy as jnp
import numpy as np
from jax import lax
from jax.experimental import pallas as pl
from jax.experimental.pallas import tpu as pltpu

F32 = jnp.float32
BF16 = jnp.bfloat16
I32 = jnp.int32

EPS = 1e-6
N_MOD = 6
ROPE_BASE = 10000.0
GRID_W = 64
N_MIXERS = 3

DA_HEAD_DIM = 64
DA_V_DIM = 128
RET_QK_DIM = 256
RET_V_DIM = 512
RET_CHUNK = 128

HY_ORDER = 2
HY_EMB = 33
HY_BANDS = (HY_EMB - 1) // 2
HY_TARGET = 1e-2
HY_FAST_DECAY_PCT = 0.3
HY_SLOW_DECAY_PCT = 1.5

N_EXPERTS = 16
EC_CAPACITY_FACTOR = 2

LANES = 128
VMEM_LIMIT_BYTES = 56 * 1024 * 1024


def _cparams(*sem):
    return pltpu.CompilerParams(dimension_semantics=sem, vmem_limit_bytes=VMEM_LIMIT_BYTES)


def _split_bf16(a):
    hi = a.astype(BF16)
    lo = (a - hi.astype(F32)).astype(BF16)
    return hi, lo


def _dot(a, b):
    return jnp.dot(a, b, preferred_element_type=F32)


def _dot3(a_hi, a_lo, b_hi, b_lo):
    return _dot(a_hi, b_hi) + _dot(a_lo, b_hi) + _dot(a_hi, b_lo)


def _modulate(x, g, sh, sc):
    ms = jnp.mean(x * x, axis=-1, keepdims=True)
    y = x * lax.rsqrt(ms + EPS) * g
    return y * (1.0 + sc) + sh


def _modvec_kernel(a_ref, w_ref, b_ref, o_ref):
    a = a_ref[...]
    a_hi, a_lo = _split_bf16(a * jax.nn.sigmoid(a))
    w_hi, w_lo = _split_bf16(w_ref[0])
    o_ref[0] = _dot3(a_hi, a_lo, w_hi, w_lo) + b_ref[0]


def mod_vectors(a, mod_w, mod_b):
    depth, d, n = mod_w.shape
    tn = 1024
    return pl.pallas_call(
        _modvec_kernel,
        grid=(depth, n // tn),
        in_specs=[pl.BlockSpec((8, d), lambda i, j: (0, 0)),
                  pl.BlockSpec((1, d, tn), lambda i, j: (i, 0, j)),
                  pl.BlockSpec((1, 1, tn), lambda i, j: (i, 0, j))],
        out_specs=pl.BlockSpec((1, 8, tn), lambda i, j: (i, 0, j)),
        out_shape=jax.ShapeDtypeStruct((depth, 8, n), F32),
        compiler_params=_cparams("parallel", "parallel"),
        name="mod_vectors",
    )(a, mod_w, mod_b.reshape(depth, 1, n))


def _mod_all(c, c_ctx, mod_w, mod_b, batch):
    d = c.shape[-1]
    a = jnp.zeros((8, d), F32).at[:batch].set(c).at[batch].set(c_ctx)
    mv = mod_vectors(a, mod_w, mod_b)
    names = ("sh", "sc", "g", "sh2", "sc2", "g2")
    layers = []
    for i in range(mod_w.shape[0]):
        parts = mv[i].reshape(8, N_MOD, d)
        layer = {}
        for k, nm in enumerate(names):
            layer[nm + "_l"] = parts[:batch, k][:, None, :]
            layer[nm + "_c"] = jnp.broadcast_to(parts[batch, k][None, None, :], (batch, 1, d))
        layers.append(layer)
    return layers


def _proj_kernel(x_ref, g_ref, sh_ref, sc_ref, w_ref, b_ref, *rest, epilogue, n_extra):
    extra, o_ref, h_scr = rest[:n_extra], rest[n_extra], rest[n_extra + 1]
    j = pl.program_id(2)

    @pl.when(j == 0)
    def _():
        h_scr[...] = _modulate(x_ref[0], g_ref[...], sh_ref[0], sc_ref[0]).astype(BF16)

    acc = _dot(h_scr[...], w_ref[...]) + b_ref[...]
    epilogue(acc, j, o_ref, *extra)


def _plain_epilogue(acc, j, o_ref):
    o_ref[0] = acc.astype(o_ref.dtype)


def mod_proj(x, g, sh, sc, w, bias, *, epilogue=_plain_epilogue, extra=(), extra_specs=(),
             out_dtype=BF16, tn=512, name="mod_proj"):
    B, n, K = x.shape
    N = w.shape[1]
    tm = min(512, n)
    tn = min(tn, N)
    kern = functools.partial(_proj_kernel, epilogue=epilogue, n_extra=len(extra))
    return pl.pallas_call(
        kern,
        grid=(B, n // tm, N // tn),
        in_specs=[pl.BlockSpec((1, tm, K), lambda b, i, j: (b, i, 0)),
                  pl.BlockSpec((1, K), lambda b, i, j: (0, 0)),
                  pl.BlockSpec((1, 1, K), lambda b, i, j: (b, 0, 0)),
                  pl.BlockSpec((1, 1, K), lambda b, i, j: (b, 0, 0)),
                  pl.BlockSpec((K, tn), lambda b, i, j: (0, j)),
                  pl.BlockSpec((1, tn), lambda b, i, j: (0, j)),
                  *extra_specs],
        out_specs=pl.BlockSpec((1, tm, tn), lambda b, i, j: (b, i, j)),
        out_shape=jax.ShapeDtypeStruct((B, n, N), out_dtype),
        scratch_shapes=[pltpu.VMEM((tm, K), BF16)],
        compiler_params=_cparams("parallel", "parallel", "arbitrary"),
        name=name,
    )(x, g.reshape(1, K), sh, sc, w, bias.reshape(1, N), *extra)


def _resmm_kernel(a_ref, w_ref, b_ref, r_ref, gt_ref, o_ref):
    acc = _dot(a_ref[0], w_ref[...]) + b_ref[...]
    o_ref[0] = r_ref[0] + gt_ref[0] * acc


def res_matmul(a, w, bias, res, gate):
    B, n, K = a.shape
    N = w.shape[1]
    tm = min(512, n)
    tn = min(512, N)
    return pl.pallas_call(
        _resmm_kernel,
        grid=(B, n // tm, N // tn),
        in_specs=[pl.BlockSpec((1, tm, K), lambda b, i, j: (b, i, 0)),
                  pl.BlockSpec((K, tn), lambda b, i, j: (0, j)),
                  pl.BlockSpec((1, tn), lambda b, i, j: (0, j)),
                  pl.BlockSpec((1, tm, tn), lambda b, i, j: (b, i, j)),
                  pl.BlockSpec((1, 1, tn), lambda b, i, j: (b, 0, j))],
        out_specs=pl.BlockSpec((1, tm, tn), lambda b, i, j: (b, i, j)),
        out_shape=jax.ShapeDtypeStruct((B, n, N), F32),
        compiler_params=_cparams("parallel", "parallel", "parallel"),
        name="res_matmul",
    )(a, w, bias.reshape(1, N), res, gate)


def _da_epilogue(acc, j, o_ref, cos_ref, sin_ref, qg_ref, kg_ref, gm_ref, *, tn, n_qk_tiles):
    @pl.when(j >= n_qk_tiles)
    def _():
        o_ref[0] = acc.astype(o_ref.dtype)

    @pl.when(j < n_qk_tiles)
    def _():
        gain = jnp.where(j < n_qk_tiles // 2, qg_ref[...], kg_ref[...])
        cos, sin = cos_ref[...], sin_ref[...]
        lane = lax.broadcasted_iota(I32, (1, LANES), 1)
        first_half = (lane % 32) < 16
        for s in range(tn // LANES):
            y = acc[:, s * LANES:(s + 1) * LANES]
            hi, lo = _split_bf16(y * y)
            gsum = _dot(hi, gm_ref[...]) + _dot(lo, gm_ref[...])
            yn = y * lax.rsqrt(gsum * (1.0 / DA_HEAD_DIM) + EPS) * gain
            partner = jnp.where(first_half, pltpu.roll(yn, LANES - 16, 1), pltpu.roll(yn, 16, 1))
            o_ref[0, :, s * LANES:(s + 1) * LANES] = (yn * cos + partner * sin).astype(o_ref.dtype)


def _attn_kernel(lam_ref, q_ref, kc_ref, vc_ref, *rest, has_lat, tk, n_chunks, out_scale):
    if has_lat:
        kl_ref, vl_ref, sg_ref, o_ref = rest
    else:
        sg_ref, o_ref = rest
    q = q_ref[0]
    tq = q.shape[0]
    lane = lax.broadcasted_iota(I32, (1, LANES), 1)
    zero = jnp.zeros_like(q)
    q2 = jnp.concatenate([jnp.where(lane < DA_HEAD_DIM, q, zero),
                          jnp.where(lane >= DA_HEAD_DIM, q, zero)], axis=0)

    def scores(k):
        return lax.dot_general(q2, k, (((1,), (1,)), ((), ())), preferred_element_type=F32)

    def absorb(s, v, carry):
        m, l, acc = carry
        m_new = jnp.maximum(m, jnp.max(s, axis=-1, keepdims=True))
        alpha = jnp.exp2(m - m_new)
        p = jnp.exp2(s - m_new)
        l_new = alpha * l + jnp.sum(p, axis=-1, keepdims=True)
        acc_new = alpha * acc + _dot(p.astype(BF16), v)
        return m_new, l_new, acc_new

    carry = (jnp.full((2 * tq, 1), -1e30, F32), jnp.zeros((2 * tq, 1), F32),
             jnp.zeros((2 * tq, LANES), F32))
    carry = absorb(scores(kc_ref[0]), vc_ref[0], carry)
    if has_lat:
        def body(c, carry):
            off = pl.multiple_of(c * tk, tk)
            return absorb(scores(kl_ref[0, pl.ds(off, tk), :]), vl_ref[0, pl.ds(off, tk), :], carry)
        carry = lax.fori_loop(0, n_chunks, body, carry, unroll=8 if n_chunks % 8 == 0 else 1)
    _, l, acc = carry
    o = acc / l
    o = o[:tq] - lam_ref[0] * o[tq:]
    ms = jnp.mean(o * o, axis=-1, keepdims=True)
    o_ref[0] = (o * lax.rsqrt(ms + EPS) * sg_ref[...] * out_scale).astype(o_ref.dtype)


def diff_attention_core(lam, qkv_q, qkv_c, qkv_l, sub_g, out_scale):
    B, nq, D3 = qkv_q.shape
    D = D3 // 3
    H = D // DA_V_DIM
    nc = qkv_c.shape[1]
    tq = min(256, nq)
    has_lat = qkv_l is not None
    in_specs = [pl.BlockSpec(memory_space=pltpu.SMEM),
                pl.BlockSpec((1, tq, LANES), lambda b, h, i: (b, i, h)),
                pl.BlockSpec((1, nc, LANES), lambda b, h, i: (b, 0, H + h)),
                pl.BlockSpec((1, nc, LANES), lambda b, h, i: (b, 0, 2 * H + h))]
    args = [lam, qkv_q, qkv_c, qkv_c]
    tk, n_chunks = 0, 0
    if has_lat:
        nl = qkv_l.shape[1]
        tk = min(1024, nl)
        n_chunks = nl // tk
        in_specs += [pl.BlockSpec((1, nl, LANES), lambda b, h, i: (b, 0, H + h)),
                     pl.BlockSpec((1, nl, LANES), lambda b, h, i: (b, 0, 2 * H + h))]
        args += [qkv_l, qkv_l]
    in_specs.append(pl.BlockSpec((1, LANES), lambda b, h, i: (0, 0)))
    args.append(sub_g.reshape(1, LANES))
    kern = functools.partial(_attn_kernel, has_lat=has_lat, tk=tk, n_chunks=n_chunks,
                             out_scale=out_scale)
    return pl.pallas_call(
        kern,
        grid=(B, H, nq // tq),
        in_specs=in_specs,
        out_specs=pl.BlockSpec((1, tq, LANES), lambda b, h, i: (b, i, h)),
        out_shape=jax.ShapeDtypeStruct((B, nq, D), BF16),
        compiler_params=_cparams("parallel", "parallel", "arbitrary"),
        name="diff_attention",
    )(*args)


def _axial_rope_tables(n):
    f = DA_HEAD_DIM // 4
    inv = 1.0 / (ROPE_BASE ** (jnp.arange(f, dtype=F32) / f))
    rows = n // GRID_W
    row = jnp.repeat(jnp.arange(rows), GRID_W).astype(F32)
    col = jnp.tile(jnp.arange(GRID_W), rows).astype(F32)
    ang = jnp.stack([row[:, None] * inv, col[:, None] * inv], axis=1)
    cos, sin = jnp.cos(ang), jnp.sin(ang)
    cos64 = jnp.concatenate([cos, cos], axis=-1).reshape(n, DA_HEAD_DIM)
    sin64 = jnp.concatenate([-sin, sin], axis=-1).reshape(n, DA_HEAD_DIM)
    return jnp.tile(cos64, (1, 2)), jnp.tile(sin64, (1, 2))


def da_project(h_in, g, sh, sc, w_in_bf, q_g, k_g, cos, sin):
    B, n, D = h_in.shape
    tn = 512
    tm = min(512, n)
    gm = (np.arange(LANES)[:, None] // DA_HEAD_DIM == np.arange(LANES)[None, :] // DA_HEAD_DIM)
    gm = jnp.asarray(gm, BF16)
    qg = (jnp.tile(q_g.astype(F32), 2) * (DA_HEAD_DIM ** -0.5 * math.log2(math.e))).reshape(1, LANES)
    kg = jnp.tile(k_g.astype(F32), 2).reshape(1, LANES)
    extra = (cos, sin, qg, kg, gm)
    extra_specs = (pl.BlockSpec((tm, LANES), lambda b, i, j: (i, 0)),
                   pl.BlockSpec((tm, LANES), lambda b, i, j: (i, 0)),
                   pl.BlockSpec((1, LANES), lambda b, i, j: (0, 0)),
                   pl.BlockSpec((1, LANES), lambda b, i, j: (0, 0)),
                   pl.BlockSpec((LANES, LANES), lambda b, i, j: (0, 0)))
    epi = functools.partial(_da_epilogue, tn=tn, n_qk_tiles=2 * D // tn)
    return mod_proj(h_in, g, sh, sc, w_in_bf, jnp.zeros((3 * D,), F32), epilogue=epi, extra=extra,
                    extra_specs=extra_specs, tn=tn, name="da_project")


def diff_attention_layer(x, ctx, mv, norm_g, w_in, q_g, k_g, lam, sub_g, w_out, lambda_init, need_ctx):
    B, L, D = x.shape
    nc = ctx.shape[1]
    w_in_bf = w_in.astype(BF16)
    w_out_bf = w_out.astype(BF16)
    cos_l, sin_l = _axial_rope_tables(L)
    cos_c, sin_c = jnp.ones((nc, LANES), F32), jnp.zeros((nc, LANES), F32)
    qkv_l = da_project(x, norm_g, mv["sh_l"], mv["sc_l"], w_in_bf, q_g, k_g, cos_l, sin_l)
    qkv_c = da_project(ctx, norm_g, mv["sh_c"], mv["sc_c"], w_in_bf, q_g, k_g, cos_c, sin_c)
    lamf = lam.astype(F32)
    lam_val = (jnp.exp(jnp.sum(lamf[0] * lamf[1])) - jnp.exp(jnp.sum(lamf[2] * lamf[3]))
               + lambda_init).reshape(1)
    out_scale = 1.0 - lambda_init
    o_l = diff_attention_core(lam_val, qkv_l, qkv_c, qkv_l, sub_g, out_scale)
    zb = jnp.zeros((D,), F32)
    x = res_matmul(o_l, w_out_bf, zb, x, mv["g_l"])
    if need_ctx:
        o_c = diff_attention_core(lam_val, qkv_c, qkv_c, None, sub_g, out_scale)
        ctx = res_matmul(o_c, w_out_bf, zb, ctx, mv["g_c"])
    return x, ctx


def _router_kernel(x_ref, g_ref, sh_ref, sc_ref, whi_ref, wlo_ref, o_ref):
    h = _modulate(x_ref[0], g_ref[...], sh_ref[0], sc_ref[0])
    o_ref[0] = _router_affinity(h, whi_ref[...], wlo_ref[...])


def _router_affinity(h, w_hi, w_lo):
    h_hi, h_lo = _split_bf16(h)
    logits = _dot3(h_hi, h_lo, w_hi, w_lo)
    lane = lax.broadcasted_iota(I32, (1, LANES), 1)
    logits = jnp.where(lane < N_EXPERTS, logits, -1e30)
    z = jnp.exp(logits - jnp.max(logits, axis=-1, keepdims=True))
    return z / jnp.sum(z, axis=-1, keepdims=True)


def router_affinity(x, g, sh, sc, w_hi, w_lo):
    B, n, D = x.shape
    tm = min(512, n)
    return pl.pallas_call(
        _router_kernel,
        grid=(B, n // tm),
        in_specs=[pl.BlockSpec((1, tm, D), lambda b, i: (b, i, 0)),
                  pl.BlockSpec((1, D), lambda b, i: (0, 0)),
                  pl.BlockSpec((1, 1, D), lambda b, i: (b, 0, 0)),
                  pl.BlockSpec((1, 1, D), lambda b, i: (b, 0, 0)),
                  pl.BlockSpec((D, LANES), lambda b, i: (0, 0)),
                  pl.BlockSpec((D, LANES), lambda b, i: (0, 0))],
        out_specs=pl.BlockSpec((1, tm, LANES), lambda b, i: (b, i, 0)),
        out_shape=jax.ShapeDtypeStruct((B, n, LANES), F32),
        compiler_params=_cparams("parallel", "parallel"),
        name="router",
    )(x, g.reshape(1, D), sh, sc, w_hi, w_lo)


def _ffn_kernel(idx_ref, x_hbm, acc_in, g_ref, sh_ref, sc_ref, g2_ref, rhi_ref, rlo_ref,
                w1_ref, w3_ref, w2_ref, o_hbm, xbuf, abuf, sem, *, n, cap, tm):
    del acc_in
    b, e, j = pl.program_id(0), pl.program_id(1), pl.program_id(2)
    n_exp = pl.num_programs(1)
    base = (b * n_exp + e) * cap + j * tm
    row0 = b * n

    def gather(k, c):
        row = row0 + idx_ref[base + k]
        pltpu.make_async_copy(x_hbm.at[pl.ds(row, 1)], xbuf.at[pl.ds(k, 1)], sem.at[0]).start()
        pltpu.make_async_copy(o_hbm.at[pl.ds(row, 1)], abuf.at[pl.ds(k, 1)], sem.at[1]).start()
        return c

    lax.fori_loop(0, tm, gather, 0)
    pltpu.make_async_copy(x_hbm.at[pl.ds(0, tm)], xbuf, sem.at[0]).wait()

    h = _modulate(xbuf[...], g_ref[...], sh_ref[0], sc_ref[0])
    aff = _router_affinity(h, rhi_ref[...], rlo_ref[...])
    lane = lax.broadcasted_iota(I32, (1, LANES), 1)
    gate = jnp.sum(jnp.where(lane == e, aff, 0.0), axis=-1, keepdims=True)
    hb = h.astype(BF16)
    a1 = _dot(hb, w1_ref[0])
    a3 = _dot(hb, w3_ref[0])
    hid = (a1 * jax.nn.sigmoid(a1) * a3).astype(BF16)
    y = _dot(hid, w2_ref[0])
    upd = g2_ref[0] * (gate * y)

    pltpu.make_async_copy(o_hbm.at[pl.ds(0, tm)], abuf, sem.at[1]).wait()
    abuf[...] = abuf[...] + upd

    def scatter(k, c):
        row = row0 + idx_ref[base + k]
        pltpu.make_async_copy(abuf.at[pl.ds(k, 1)], o_hbm.at[pl.ds(row, 1)], sem.at[2]).start()
        return c

    lax.fori_loop(0, tm, scatter, 0)
    pltpu.make_async_copy(abuf, o_hbm.at[pl.ds(0, tm)], sem.at[2]).wait()


def expert_ffn(idx, x, g, sh, sc, g2, r_hi, r_lo, w1, w3, w2):
    B, n, D = x.shape
    E, cap = idx.shape[1], idx.shape[2]
    F = w1.shape[-1]
    tm = min(256, cap)
    x2 = x.reshape(B * n, D)
    kern = functools.partial(_ffn_kernel, n=n, cap=cap, tm=tm)
    vec = lambda: pl.BlockSpec((1, 1, D), lambda b, e, j, idx: (b, 0, 0))
    grid_spec = pltpu.PrefetchScalarGridSpec(
        num_scalar_prefetch=1,
        grid=(B, E, cap // tm),
        in_specs=[pl.BlockSpec(memory_space=pl.ANY),
                  pl.BlockSpec(memory_space=pl.ANY),
                  pl.BlockSpec((1, D), lambda b, e, j, idx: (0, 0)),
                  vec(), vec(), vec(),
                  pl.BlockSpec((D, LANES), lambda b, e, j, idx: (0, 0)),
                  pl.BlockSpec((D, LANES), lambda b, e, j, idx: (0, 0)),
                  pl.BlockSpec((1, D, F), lambda b, e, j, idx: (e, 0, 0)),
                  pl.BlockSpec((1, D, F), lambda b, e, j, idx: (e, 0, 0)),
                  pl.BlockSpec((1, F, D), lambda b, e, j, idx: (e, 0, 0))],
        out_specs=pl.BlockSpec(memory_space=pl.ANY),
        scratch_shapes=[pltpu.VMEM((tm, D), F32), pltpu.VMEM((tm, D), F32),
                        pltpu.SemaphoreType.DMA((3,))])
    out = pl.pallas_call(
        kern,
        grid_spec=grid_spec,
        out_shape=jax.ShapeDtypeStruct((B * n, D), F32),
        input_output_aliases={2: 0},
        compiler_params=_cparams("arbitrary", "arbitrary", "arbitrary"),
        name="expert_ffn",
    )(idx.reshape(-1), x2, x2, g.reshape(1, D), sh, sc, g2, r_hi, r_lo, w1, w3, w2)
    return out.reshape(B, n, D)


TOPK_ROWS = 128


def _topk_kernel(a_ref, idx_ref, *, k, capp):
    a = a_ref[0, 0]
    bits = lax.bitcast_convert_type(a, I32)
    row_i = lax.broadcasted_iota(I32, (TOPK_ROWS, LANES), 0)
    lane_i = lax.broadcasted_iota(I32, (TOPK_ROWS, LANES), 1)
    tok = row_i * LANES + lane_i

    def count(ind):
        return jnp.sum(jnp.sum(ind, axis=1, keepdims=True), axis=0, keepdims=True)

    thr = jnp.zeros((1, 1), I32)
    for bit in range(30, -1, -1):
        cand = thr | (1 << bit)
        thr = jnp.where(count(jnp.where(bits >= cand, 1, 0)) >= k, cand, thr)
    gt = jnp.where(bits > thr, 1, 0)
    eq = jnp.where(bits == thr, 1, 0)
    need = k - count(gt)
    tcut = jnp.zeros((1, 1), I32)
    for bit in range((TOPK_ROWS * LANES).bit_length() - 2, -1, -1):
        cand = tcut | (1 << bit)
        tcut = jnp.where(count(jnp.where(tok < cand, eq, 0)) < need, cand, tcut)
    sel = gt + jnp.where(tok <= tcut, eq, 0)

    m = sel.astype(F32)
    cnt = jnp.sum(m, axis=1, keepdims=True)
    tri = jnp.where(lax.broadcasted_iota(I32, (TOPK_ROWS, TOPK_ROWS), 0)
                    >= lax.broadcasted_iota(I32, (TOPK_ROWS, TOPK_ROWS), 1), 1.0, 0.0).astype(BF16)
    incl = _dot(tri, jnp.broadcast_to(cnt, (TOPK_ROWS, LANES)).astype(BF16))[:, :1]
    slot = lax.broadcasted_iota(I32, (1, capp), 1).astype(F32)
    before = incl <= slot
    row_of = jnp.sum(jnp.where(before, 1.0, 0.0), axis=0, keepdims=True)
    base = jnp.sum(jnp.where(before, cnt, 0.0), axis=0, keepdims=True)
    rows_col = lax.broadcasted_iota(I32, (TOPK_ROWS, 1), 0).astype(F32)
    onehot = jnp.where(rows_col == row_of, 1.0, 0.0).astype(BF16)
    picked = _dot(m.T.astype(BF16), onehot)
    cum = _dot(tri, picked.astype(BF16))
    local = jnp.sum(jnp.where(cum <= slot - base, 1.0, 0.0), axis=0, keepdims=True)
    idx_ref[0, 0] = (row_of * LANES + local).astype(I32)


def expert_choice_topk(aff, cap):
    B, E, n = aff.shape
    n_pad = TOPK_ROWS * LANES
    assert n <= n_pad and TOPK_ROWS == LANES
    a = jnp.pad(aff, ((0, 0), (0, 0), (0, n_pad - n)), constant_values=-1.0)
    a = a.reshape(B, E, TOPK_ROWS, LANES)
    capp = max(cap, LANES)
    out = pl.pallas_call(
        functools.partial(_topk_kernel, k=cap, capp=capp),
        grid=(B, E),
        in_specs=[pl.BlockSpec((1, 1, TOPK_ROWS, LANES), lambda b, e: (b, e, 0, 0))],
        out_specs=pl.BlockSpec((1, 1, 1, capp), lambda b, e: (b, e, 0, 0)),
        out_shape=jax.ShapeDtypeStruct((B, E, 1, capp), I32),
        compiler_params=_cparams("parallel", "parallel"),
        name="expert_topk",
    )(a)
    return out[:, :, 0, :cap]


def moe_layer(x, g, sh, sc, g2, router_w, w1, w3, w2):
    B, n, D = x.shape
    E = router_w.shape[1]
    cap = EC_CAPACITY_FACTOR * n // E
    rw = jnp.pad(router_w.astype(F32), ((0, 0), (0, LANES - E)))
    r_hi, r_lo = _split_bf16(rw)
    aff = router_affinity(x, g, sh, sc, r_hi, r_lo)
    idx = expert_choice_topk(aff[:, :, :E].transpose(0, 2, 1), cap)
    return expert_ffn(idx, x, g, sh, sc, g2, r_hi, r_lo, w1, w3, w2)


def _ret_epilogue(acc, j, o_ref, cos_ref, sin_ref, *, tn, n_q_tiles):
    @pl.when(j >= 2 * n_q_tiles)
    def _():
        o_ref[0] = acc.astype(o_ref.dtype)

    @pl.when(j < 2 * n_q_tiles)
    def _():
        scale = jnp.where(j < n_q_tiles, 1.0, RET_QK_DIM ** -0.5)
        cos, sin = cos_ref[...] * scale, sin_ref[...] * scale
        f = RET_QK_DIM // 2
        for s in range(tn // RET_QK_DIM):
            lo, mid, hi = s * RET_QK_DIM, s * RET_QK_DIM + f, (s + 1) * RET_QK_DIM
            x1, x2 = acc[:, lo:mid], acc[:, mid:hi]
            o_ref[0, :, lo:mid] = (x1 * cos - x2 * sin).astype(o_ref.dtype)
            o_ref[0, :, mid:hi] = (x2 * cos + x1 * sin).astype(o_ref.dtype)


def _retention_kernel(lg_ref, qf_ref, kf_ref, vf_ref, qb_ref, kb_ref, vb_ref, s0f_ref, s0b_ref,
                      of_ref, ob_ref, sf_ref, sb_ref, stf, stb, *, T):
    h, c = pl.program_id(1), pl.program_id(2)

    @pl.when(c == 0)
    def _():
        stf[...] = s0f_ref[0, 0]
        stb[...] = s0b_ref[0, 0]

    ii = lax.broadcasted_iota(I32, (T, T), 0).astype(F32)
    jj = lax.broadcasted_iota(I32, (T, T), 1).astype(F32)
    row = lax.broadcasted_iota(I32, (T, 1), 0).astype(F32)

    def one(q_ref, k_ref, v_ref, st, o_ref, lg, forward):
        q, k, v = q_ref[0], k_ref[0], v_ref[0]
        d = ii - jj if forward else jj - ii
        mask = d >= 0 if forward else d > 0
        dmask = jnp.where(mask, jnp.exp(jnp.where(mask, d, 0.0) * lg), 0.0)
        xi = jnp.exp(((row + 1.0) if forward else (T - row)) * lg)
        zeta = jnp.exp(((T - 1.0 - row) if forward else row) * lg)
        inner = lax.dot_general(q, k, (((1,), (1,)), ((), ())), preferred_element_type=F32) * dmask
        o = _dot(inner.astype(BF16), v) + _dot(q, st[...].astype(BF16)) * xi
        o_ref[0] = o.astype(o_ref.dtype)
        kz = (k.astype(F32) * zeta).astype(BF16)
        decay = jnp.exp(jnp.full((1, 1), T, F32) * lg)
        st[...] = st[...] * decay + lax.dot_general(kz, v, (((0,), (0,)), ((), ())),
                                                    preferred_element_type=F32)

    one(qf_ref, kf_ref, vf_ref, stf, of_ref, lg_ref[0, h], True)
    one(qb_ref, kb_ref, vb_ref, stb, ob_ref, lg_ref[1, h], False)

    @pl.when(c == pl.num_programs(2) - 1)
    def _():
        sf_ref[0, 0] = stf[...]
        sb_ref[0, 0] = stb[...]


def retention_core(log_g, p, s0f, s0b):
    B, n, _ = p.shape
    H, dk, dv = s0f.shape[1], s0f.shape[2], s0f.shape[3]
    T = min(256, n)
    nc = n // T
    kq, kk, kv = 0, (H * dk) // dk, (2 * H * dk) // dv
    fwd = lambda off: (lambda b, h, c: (b, c, off + h))
    bwd = lambda off: (lambda b, h, c: (b, nc - 1 - c, off + h))
    st_spec = pl.BlockSpec((1, 1, dk, dv), lambda b, h, c: (b, h, 0, 0))
    out_sd = jax.ShapeDtypeStruct((B, n, H * dv), BF16)
    st_sd = jax.ShapeDtypeStruct((B, H, dk, dv), F32)
    return pl.pallas_call(
        functools.partial(_retention_kernel, T=T),
        grid=(B, H, nc),
        in_specs=[pl.BlockSpec(memory_space=pltpu.SMEM),
                  pl.BlockSpec((1, T, dk), fwd(kq)), pl.BlockSpec((1, T, dk), fwd(kk)),
                  pl.BlockSpec((1, T, dv), fwd(kv)),
                  pl.BlockSpec((1, T, dk), bwd(kq)), pl.BlockSpec((1, T, dk), bwd(kk)),
                  pl.BlockSpec((1, T, dv), bwd(kv)),
                  st_spec, st_spec],
        out_specs=[pl.BlockSpec((1, T, dv), fwd(0)), pl.BlockSpec((1, T, dv), bwd(0)), st_spec, st_spec],
        out_shape=[out_sd, out_sd, st_sd, st_sd],
        scratch_shapes=[pltpu.VMEM((dk, dv), F32), pltpu.VMEM((dk, dv), F32)],
        compiler_params=_cparams("parallel", "parallel", "arbitrary"),
        name="retention",
    )(log_g, p, p, p, p, p, p, s0f, s0b)


def _ret_finish_kernel(of_ref, ob_ref, g_ref, gn_ref, z_ref, *, dv):
    for s in range(of_ref.shape[-1] // dv):
        sl = slice(s * dv, (s + 1) * dv)
        y = of_ref[0, :, sl].astype(F32) + ob_ref[0, :, sl].astype(F32)
        ms = jnp.mean(y * y, axis=-1, keepdims=True)
        y = y * lax.rsqrt(ms + EPS) * gn_ref[...]
        g = g_ref[0, :, sl].astype(F32)
        z_ref[0, :, sl] = (g * jax.nn.sigmoid(g) * y).astype(z_ref.dtype)


def retention_finish(of, ob, p, gn_g):
    B, n, N = of.shape
    dv = gn_g.shape[0]
    tm = min(512, n)
    blk = lambda off: pl.BlockSpec((1, tm, N), lambda b, i: (b, i, off))
    return pl.pallas_call(
        functools.partial(_ret_finish_kernel, dv=dv),
        grid=(B, n // tm),
        in_specs=[blk(0), blk(0), blk(p.shape[-1] // N - 1), pl.BlockSpec((1, dv), lambda b, i: (0, 0))],
        out_specs=blk(0),
        out_shape=jax.ShapeDtypeStruct((B, n, N), BF16),
        compiler_params=_cparams("parallel", "parallel"),
        name="retention_finish",
    )(of, ob, p, gn_g.astype(F32).reshape(1, dv))


def retention_layer(x, ctx, mv, norm_g, w_in, log_alpha, gn_g, w_out, need_ctx):
    B, L, D = x.shape
    nc = ctx.shape[1]
    dk, dv = RET_QK_DIM, RET_V_DIM
    H = D // dk
    w_in_bf = w_in.astype(BF16)
    w_out_bf = w_out.astype(BF16)
    zb = jnp.zeros((w_in.shape[1],), F32)
    log_g = jnp.log1p(-jnp.exp(log_alpha.astype(F32)))
    f = dk // 2
    inv = 1.0 / (ROPE_BASE ** jnp.linspace(0.0, 1.0, f, dtype=F32))
    ang = jnp.arange(L, dtype=F32)[:, None] * inv
    tn = 512

    def project(h_in, sh, sc, cos, sin):
        tm = min(512, h_in.shape[1])
        specs = (pl.BlockSpec((tm, f), lambda b, i, j: (i, 0)), pl.BlockSpec((tm, f), lambda b, i, j: (i, 0)))
        epi = functools.partial(_ret_epilogue, tn=tn, n_q_tiles=H * dk // tn)
        return mod_proj(h_in, norm_g, sh, sc, w_in_bf, zb, epilogue=epi, extra=(cos, sin),
                        extra_specs=specs, tn=tn, name="ret_project")

    p_c = project(ctx, mv["sh_c"], mv["sc_c"], jnp.ones((nc, f), F32), jnp.zeros((nc, f), F32))
    p_l = project(x, mv["sh_l"], mv["sc_l"], jnp.cos(ang), jnp.sin(ang))
    zero = jnp.zeros((B, H, dk, dv), F32)
    of_c, ob_c, s_f, s_b = retention_core(log_g, p_c, zero, zero)
    of_l, ob_l, _, _ = retention_core(log_g, p_l, s_f, s_b)
    zd = jnp.zeros((D,), F32)
    x = res_matmul(retention_finish(of_l, ob_l, p_l, gn_g), w_out_bf, zd, x, mv["g_l"])
    if need_ctx:
        ctx = res_matmul(retention_finish(of_c, ob_c, p_c, gn_g), w_out_bf, zd, ctx, mv["g_c"])
    return x, ctx


def _hyena_filter_taps(n, w1, b1, w2, b2, freq, w3):
    t = jnp.linspace(0.0, 1.0, n, dtype=F32)[:, None]
    w = 2.0 * math.pi * jnp.arange(n, dtype=F32)[:, None] / n
    f = jnp.linspace(1e-4, HY_BANDS - 1, HY_BANDS, dtype=F32)[None]
    z = jnp.concatenate([t, jnp.cos(f * w), -jnp.sin(f * w)], axis=-1)
    hp = lax.Precision.HIGHEST
    a = jnp.sin(freq[0].astype(F32) * (jnp.dot(z, w1.astype(F32), precision=hp) + b1.astype(F32)))
    a = jnp.sin(freq[1].astype(F32) * (jnp.dot(a, w2.astype(F32), precision=hp) + b2.astype(F32)))
    C = w3.shape[-1] // (2 * HY_ORDER)
    h = jnp.dot(a, w3.astype(F32), precision=hp).reshape(n, HY_ORDER, 2, C)
    max_decay = math.log(HY_TARGET) / HY_FAST_DECAY_PCT
    min_decay = math.log(HY_TARGET) / HY_SLOW_DECAY_PCT
    deltas = jnp.abs(jnp.linspace(min_decay, max_decay, C, dtype=F32))
    return h * jnp.exp(-t * deltas)[:, None, None, :]


def _filter_spectrum_small(h):
    n, _, _, C = h.shape
    buf = jnp.concatenate([h[:, :, 0], jnp.zeros((1, HY_ORDER, C), F32),
                           jnp.flip(h[1:, :, 1], axis=0)], axis=0)
    buf = buf / jnp.sum(jnp.abs(buf), axis=0, keepdims=True)
    hf = jnp.fft.rfft(buf, axis=0)
    return jnp.concatenate([hf, jnp.conj(jnp.flip(hf[1:n], axis=0))], axis=0)


def _fft_a_real_kernel(z_ref, m_ref, o_ref):
    o_ref[...] = _dot(m_ref[...], z_ref[...]).astype(o_ref.dtype)


def _fft_c_filter_kernel(af_ref, ab_ref, gf_ref, inv_ref, o_ref):
    n2, C = af_ref.shape[2], af_ref.shape[3]
    xf = _dot(gf_ref[0], af_ref[:, 0].reshape(2 * n2, C))
    xb = _dot(gf_ref[0], ab_ref[:, 0].reshape(2 * n2, C))
    inv = inv_ref[...]
    o_ref[0, 0] = (xf[:n2] + xb[:n2]) * inv
    o_ref[0, 1] = (xf[n2:] - xb[n2:]) * inv


def _filter_spectrum_fft(h, tabs):
    n, _, _, C = h.shape
    N1, S1, N2 = tabs["N1"], tabs["S1"], tabs["N2"]
    cols = N2 * C
    tc = 2 * C
    fwd = h[:, :, 0]
    bwd = h[:, :, 1].at[0].set(0.0)
    inv_norm = 1.0 / (jnp.sum(jnp.abs(fwd), axis=0) + jnp.sum(jnp.abs(bwd), axis=0))
    m_re = tabs["m_a"][:, :S1]
    out = []
    for o in range(HY_ORDER):
        parts = []
        for taps in (fwd[:, o], bwd[:, o]):
            a = pl.pallas_call(
                _fft_a_real_kernel,
                grid=(cols // tc,),
                in_specs=[pl.BlockSpec((S1, tc), lambda j: (0, j)),
                          pl.BlockSpec((2 * N1, S1), lambda j: (0, 0))],
                out_specs=pl.BlockSpec((2 * N1, tc), lambda j: (0, j)),
                out_shape=jax.ShapeDtypeStruct((2 * N1, cols), BF16),
                compiler_params=_cparams("parallel"),
                name="hy_filter_fft_a",
            )(taps.astype(BF16).reshape(S1, cols), m_re)
            parts.append(a.reshape(2, N1, N2, C))
        ab = pl.BlockSpec((2, 1, N2, C), lambda i: (0, i, 0, 0))
        out.append(pl.pallas_call(
            _fft_c_filter_kernel,
            grid=(N1,),
            in_specs=[ab, ab, pl.BlockSpec((1, 2 * N2, 2 * N2), lambda i: (i, 0, 0)),
                      pl.BlockSpec((1, C), lambda i: (0, 0))],
            out_specs=pl.BlockSpec((1, 2, N2, C), lambda i: (i, 0, 0, 0)),
            out_shape=jax.ShapeDtypeStruct((N1, 2, N2, C), F32),
            compiler_params=_cparams("parallel"),
            name="hy_filter_fft_c",
        )(parts[0], parts[1], tabs["g_f"], inv_norm[o].reshape(1, C)))
    return out


def _short_conv_kernel(u_ref, prev_ref, next_ref, w_ref, b_ref, o_ref):
    i, ni = pl.program_id(1), pl.num_programs(1)
    u = u_ref[0]
    tm = u.shape[0]
    row = lax.broadcasted_iota(I32, (tm, 1), 0)
    prev_row = jnp.where(i > 0, prev_ref[0, 7:8, :], 0.0)
    next_row = jnp.where(i < ni - 1, next_ref[0, 0:1, :], 0.0)
    before = jnp.where(row == 0, prev_row, pltpu.roll(u, 1, 0))
    after = jnp.where(row == tm - 1, next_row, pltpu.roll(u, tm - 1, 0))
    w = w_ref[...]
    o_ref[0, 0] = (before * w[0:1] + u * w[1:2] + after * w[2:3] + b_ref[...]).astype(o_ref.dtype)


def short_conv_split(u, conv_w, conv_b):
    B, n, C3 = u.shape
    C = C3 // 3
    tm = min(512, n)
    r8 = tm // 8
    return pl.pallas_call(
        _short_conv_kernel,
        grid=(B, n // tm, 3),
        in_specs=[pl.BlockSpec((1, tm, C), lambda b, i, j: (b, i, j)),
                  pl.BlockSpec((1, 8, C), lambda b, i, j: (b, jnp.maximum(i * r8 - 1, 0), j)),
                  pl.BlockSpec((1, 8, C), lambda b, i, j: (b, jnp.minimum((i + 1) * r8, n // 8 - 1), j)),
                  pl.BlockSpec((8, C), lambda b, i, j: (0, j)),
                  pl.BlockSpec((1, C), lambda b, i, j: (0, j))],
        out_specs=pl.BlockSpec((1, 1, tm, C), lambda b, i, j: (j, b, i, 0)),
        out_shape=jax.ShapeDtypeStruct((3, B, n, C), BF16),
        compiler_params=_cparams("parallel", "parallel", "parallel"),
        name="hy_short_conv",
    )(u, u, u, jnp.pad(conv_w.astype(F32), ((0, 5), (0, 0))), conv_b.astype(F32).reshape(1, C3))


FFT_N1 = 128


def _stack_complex(re, im):
    return jnp.concatenate([jnp.concatenate([re, -im], axis=-1),
                            jnp.concatenate([im, re], axis=-1)], axis=-2)


def _phase(k, period):
    a = (2.0 * math.pi / period) * (k % period).astype(F32)
    return jnp.cos(a), jnp.sin(a)


def _fft_a_kernel(z_ref, m_ref, o_ref):
    z = z_ref[...]
    o_ref[...] = _dot(m_ref[...], z.reshape(2 * z.shape[1], z.shape[2])).astype(o_ref.dtype)


def _fft_c_kernel(a_ref, gf_ref, gi_ref, h_ref, o_ref):
    n2, C = a_ref.shape[2], a_ref.shape[3]
    x = _dot(gf_ref[0], a_ref[:, 0].reshape(2 * n2, C))
    xr, xi = x[:n2], x[n2:]
    hr, hi = h_ref[0, 0], h_ref[0, 1]
    y = jnp.concatenate([xr * hr - xi * hi, xr * hi + xi * hr], axis=0).astype(BF16)
    o_ref[:, 0] = _dot(gi_ref[0], y).astype(o_ref.dtype).reshape(2, n2, C)


def _fft_b_kernel(b_ref, m_ref, u_ref, xg_ref, bias_ref, o_ref):
    s1, tc = u_ref.shape[1], u_ref.shape[2]
    y = _dot(m_ref[...], b_ref[...]).reshape(2, s1, tc)
    o_ref[...] = (xg_ref[...].astype(F32) * (y + u_ref[...].astype(F32) * bias_ref[...])).astype(o_ref.dtype)


def _dense_conv_kernel(u_ref, mf_ref, mi_ref, h_ref, xg_ref, bias_ref, o_ref):
    n, tc = u_ref.shape[1], u_ref.shape[2]
    u = u_ref[...]
    x = _dot(mf_ref[...], u.reshape(2 * n, tc))
    N = x.shape[0] // 2
    xr, xi = x[:N], x[N:]
    hr, hi = h_ref[0], h_ref[1]
    y = jnp.concatenate([xr * hr - xi * hi, xr * hi + xi * hr], axis=0).astype(BF16)
    out = _dot(mi_ref[...], y).reshape(2, n, tc)
    o_ref[...] = (xg_ref[...].astype(F32) * (out + u.astype(F32) * bias_ref[...])).astype(o_ref.dtype)


def _fft_tables(n):
    N = 2 * n
    N1, S1 = FFT_N1, FFT_N1 // 2
    N2 = N // N1
    f1 = jnp.arange(N1, dtype=I32)
    c, s = _phase(f1[:, None] * jnp.arange(S1, dtype=I32)[None, :], N1)
    m_a = _stack_complex(c, -s).astype(BF16)
    m_b = _stack_complex(c.T / N, s.T / N).astype(BF16)
    f2 = jnp.arange(N2, dtype=I32)
    c, s = _phase(f2[None, None, :] * (f1[:, None, None] + N1 * f2[None, :, None]), N)
    g_f = _stack_complex(c, -s).astype(BF16)
    g_i = _stack_complex(jnp.swapaxes(c, 1, 2), jnp.swapaxes(s, 1, 2)).astype(BF16)
    return dict(N1=N1, S1=S1, N2=N2, m_a=m_a, m_b=m_b, g_f=g_f, g_i=g_i)


def long_conv_gate(u, xg, spec, bias, tabs):
    B, n, C = u.shape
    assert B == 2
    N = 2 * n
    bias3 = bias.astype(F32).reshape(1, 1, C)
    if tabs is None:
        hr, hi = jnp.real(spec).astype(F32), jnp.imag(spec).astype(F32)
        f, t = jnp.arange(N, dtype=I32)[:, None], jnp.arange(n, dtype=I32)[None, :]
        c, s = _phase(f * t, N)
        mf = _stack_complex(c, -s).astype(BF16)
        mi = _stack_complex(c.T / N, s.T / N).astype(BF16)
        tc = 256
        blk = pl.BlockSpec((2, n, tc), lambda j: (0, 0, j))
        return pl.pallas_call(
            _dense_conv_kernel,
            grid=(C // tc,),
            in_specs=[blk, pl.BlockSpec((2 * N, 2 * n), lambda j: (0, 0)),
                      pl.BlockSpec((2 * n, 2 * N), lambda j: (0, 0)),
                      pl.BlockSpec((2, N, tc), lambda j: (0, 0, j)), blk,
                      pl.BlockSpec((1, 1, tc), lambda j: (0, 0, j))],
            out_specs=blk,
            out_shape=jax.ShapeDtypeStruct((2, n, C), BF16),
            compiler_params=_cparams("parallel"),
            name="hy_dense_conv",
        )(u, mf, mi, jnp.stack([hr, hi]), xg, bias3)

    N1, S1, N2 = tabs["N1"], tabs["S1"], tabs["N2"]
    m_a, m_b, g_f, g_i, hperm = tabs["m_a"], tabs["m_b"], tabs["g_f"], tabs["g_i"], spec
    cols = N2 * C
    tc = 2 * C
    zb = pl.BlockSpec((2, S1, tc), lambda j: (0, 0, j))
    a = pl.pallas_call(
        _fft_a_kernel,
        grid=(cols // tc,),
        in_specs=[zb, pl.BlockSpec((2 * N1, 2 * S1), lambda j: (0, 0))],
        out_specs=pl.BlockSpec((2 * N1, tc), lambda j: (0, j)),
        out_shape=jax.ShapeDtypeStruct((2 * N1, cols), BF16),
        compiler_params=_cparams("parallel"),
        name="hy_fft_a",
    )(u.reshape(2, S1, cols), m_a)
    ab = pl.BlockSpec((2, 1, N2, C), lambda i: (0, i, 0, 0))
    gb = pl.BlockSpec((1, 2 * N2, 2 * N2), lambda i: (i, 0, 0))
    bm = pl.pallas_call(
        _fft_c_kernel,
        grid=(N1,),
        in_specs=[ab, gb, gb, pl.BlockSpec((1, 2, N2, C), lambda i: (i, 0, 0, 0))],
        out_specs=ab,
        out_shape=jax.ShapeDtypeStruct((2, N1, N2, C), BF16),
        compiler_params=_cparams("parallel"),
        name="hy_fft_c",
    )(a.reshape(2, N1, N2, C), g_f, g_i, hperm)
    out = pl.pallas_call(
        _fft_b_kernel,
        grid=(cols // tc,),
        in_specs=[pl.BlockSpec((2 * N1, tc), lambda j: (0, j)),
                  pl.BlockSpec((2 * S1, 2 * N1), lambda j: (0, 0)), zb, zb,
                  pl.BlockSpec((1, 1, tc), lambda j: (0, 0, 0))],
        out_specs=zb,
        out_shape=jax.ShapeDtypeStruct((2, S1, cols), BF16),
        compiler_params=_cparams("parallel"),
        name="hy_fft_b",
    )(bm.reshape(2 * N1, cols), m_b, u.reshape(2, S1, cols), xg.reshape(2, S1, cols),
      jnp.tile(bias3, (1, 1, tc // C)))
    return out.reshape(2, n, C)


def hyena_layer(x, ctx, mv, norm_g, hy_p, need_ctx):
    (w_in, b_in, conv_w, conv_b, w1, b1, w2, b2, freq, w3, fbias, w_out, b_out) = hy_p
    w_in_bf = w_in.astype(BF16)
    w_out_bf = w_out.astype(BF16)

    def mix(h_in, sh, sc, res, gate):
        n = h_in.shape[1]
        u = mod_proj(h_in, norm_g, sh, sc, w_in_bf, b_in.astype(F32), out_dtype=F32, name="hy_project")
        v, x1, x2 = short_conv_split(u, conv_w, conv_b)
        taps = _hyena_filter_taps(n, w1, b1, w2, b2, freq, w3)
        if n <= 512:
            tabs = None
            hfull = _filter_spectrum_small(taps)
            spec = [hfull[:, o] for o in range(HY_ORDER)]
        else:
            tabs = _fft_tables(n)
            spec = _filter_spectrum_fft(taps, tabs)
        z = long_conv_gate(v, x1, spec[0], fbias[0], tabs)
        z = long_conv_gate(z, x2, spec[1], fbias[1], tabs)
        return res_matmul(z, w_out_bf, b_out.astype(F32), res, gate)

    x = mix(x, mv["sh_l"], mv["sc_l"], x, mv["g_l"])
    if need_ctx:
        ctx = mix(ctx, mv["sh_c"], mv["sc_c"], ctx, mv["g_c"])
    return x, ctx


def kernel(x, c, ctx, c_ctx, norm_mix_g, norm_ffn_g, mod_w, mod_b,
           da_w_in, da_q_norm, da_k_norm, da_lambda, da_sub_norm, da_w_out,
           ret_w_in, ret_log_alpha, ret_group_norm, ret_w_out,
           hy_w_in, hy_b_in, hy_conv_w, hy_conv_b, hy_ffn_w1, hy_ffn_b1, hy_ffn_w2, hy_ffn_b2,
           hy_sin_freq, hy_ffn_w3, hy_filter_bias, hy_w_out, hy_b_out,
           router_w, exp_w1, exp_w3, exp_w2):
    B = x.shape[0]
    depth = mod_w.shape[0]
    mods = _mod_all(c, c_ctx, mod_w, mod_b, B)
    for i in range(depth):
        last = i == depth - 1
        m, s = i % N_MIXERS, i // N_MIXERS
        mv = mods[i]
        if m == 0:
            lambda_init = 0.8 - 0.6 * math.exp(-0.3 * i)
            x, ctx = diff_attention_layer(x, ctx, mv, norm_mix_g[i], da_w_in[s], da_q_norm[s], da_k_norm[s],
                                          da_lambda[s], da_sub_norm[s], da_w_out[s], lambda_init, not last)
        elif m == 1:
            x, ctx = retention_layer(x, ctx, mv, norm_mix_g[i], ret_w_in[s], ret_log_alpha[s],
                                     ret_group_norm[s], ret_w_out[s], not last)
        else:
            hy_p = (hy_w_in[s], hy_b_in[s], hy_conv_w[s], hy_conv_b[s], hy_ffn_w1[s], hy_ffn_b1[s],
                    hy_ffn_w2[s], hy_ffn_b2[s], hy_sin_freq[s], hy_ffn_w3[s], hy_filter_bias[s],
                    hy_w_out[s], hy_b_out[s])
            x, ctx = hyena_layer(x, ctx, mv, norm_mix_g[i], hy_p, not last)
        w1, w3, w2 = exp_w1[i].astype(BF16), exp_w3[i].astype(BF16), exp_w2[i].astype(BF16)
        x = moe_layer(x, norm_ffn_g[i], mv["sh2_l"], mv["sc2_l"], mv["g2_l"], router_w[i], w1, w3, w2)
        if not last:
            ctx = moe_layer(ctx, norm_ffn_g[i], mv["sh2_c"], mv["sc2_c"], mv["g2_c"], router_w[i], w1, w3, w2)
    return x
```

```python
import functools
import math

import jax
import jax.numpy as jnp
import numpy as np
from jax import lax
from jax.experimental import pallas as pl
from jax.experimental.pallas import tpu as pltpu

F32 = jnp.float32
BF16 = jnp.bfloat16
I32 = jnp.int32

EPS = 1e-6
N_MOD = 6
ROPE_BASE = 10000.0
GRID_W = 64
N_MIXERS = 3

DA_HEAD_DIM = 64
DA_V_DIM = 128
RET_QK_DIM = 256
RET_V_DIM = 512
RET_CHUNK = 128

HY_ORDER = 2
HY_EMB = 33
HY_BANDS = (HY_EMB - 1) // 2
HY_TARGET = 1e-2
HY_FAST_DECAY_PCT = 0.3
HY_SLOW_DECAY_PCT = 1.5

N_EXPERTS = 16
EC_CAPACITY_FACTOR = 2

LANES = 128
VMEM_LIMIT_BYTES = 56 * 1024 * 1024


def _cparams(*sem):
    return pltpu.CompilerParams(dimension_semantics=sem, vmem_limit_bytes=VMEM_LIMIT_BYTES)


def _split_bf16(a):
    hi = a.astype(BF16)
    lo = (a - hi.astype(F32)).astype(BF16)
    return hi, lo


def _dot(a, b):
    return jnp.dot(a, b, preferred_element_type=F32)


def _dot3(a_hi, a_lo, b_hi, b_lo):
    return _dot(a_hi, b_hi) + _dot(a_lo, b_hi) + _dot(a_hi, b_lo)


def _modulate(x, g, sh, sc):
    ms = jnp.mean(x * x, axis=-1, keepdims=True)
    y = x * lax.rsqrt(ms + EPS) * g
    return y * (1.0 + sc) + sh


def _modvec_kernel(a_ref, w_ref, b_ref, o_ref):
    a = a_ref[...]
    a_hi, a_lo = _split_bf16(a * jax.nn.sigmoid(a))
    w_hi, w_lo = _split_bf16(w_ref[0])
    o_ref[0] = _dot3(a_hi, a_lo, w_hi, w_lo) + b_ref[0]


def mod_vectors(a, mod_w, mod_b):
    depth, d, n = mod_w.shape
    tn = 1024
    return pl.pallas_call(
        _modvec_kernel,
        grid=(depth, n // tn),
        in_specs=[pl.BlockSpec((8, d), lambda i, j: (0, 0)),
                  pl.BlockSpec((1, d, tn), lambda i, j: (i, 0, j)),
                  pl.BlockSpec((1, 1, tn), lambda i, j: (i, 0, j))],
        out_specs=pl.BlockSpec((1, 8, tn), lambda i, j: (i, 0, j)),
        out_shape=jax.ShapeDtypeStruct((depth, 8, n), F32),
        compiler_params=_cparams("parallel", "parallel"),
        name="mod_vectors",
    )(a, mod_w, mod_b.reshape(depth, 1, n))


def _mod_all(c, c_ctx, mod_w, mod_b, batch):
    d = c.shape[-1]
    a = jnp.zeros((8, d), F32).at[:batch].set(c).at[batch].set(c_ctx)
    mv = mod_vectors(a, mod_w, mod_b)
    names = ("sh", "sc", "g", "sh2", "sc2", "g2")
    layers = []
    for i in range(mod_w.shape[0]):
        parts = mv[i].reshape(8, N_MOD, d)
        layer = {}
        for k, nm in enumerate(names):
            layer[nm + "_l"] = parts[:batch, k][:, None, :]
            layer[nm + "_c"] = jnp.broadcast_to(parts[batch, k][None, None, :], (batch, 1, d))
        layers.append(layer)
    return layers


def _proj_kernel(x_ref, g_ref, sh_ref, sc_ref, w_ref, b_ref, *rest, epilogue, n_extra):
    extra, o_ref, h_scr = rest[:n_extra], rest[n_extra], rest[n_extra + 1]
    j = pl.program_id(2)

    @pl.when(j == 0)
    def _():
        h_scr[...] = _modulate(x_ref[0], g_ref[...], sh_ref[0], sc_ref[0]).astype(BF16)

    acc = _dot(h_scr[...], w_ref[...]) + b_ref[...]
    epilogue(acc, j, o_ref, *extra)


def _plain_epilogue(acc, j, o_ref):
    o_ref[0] = acc.astype(o_ref.dtype)


def mod_proj(x, g, sh, sc, w, bias, *, epilogue=_plain_epilogue, extra=(), extra_specs=(),
             out_dtype=BF16, tn=512, name="mod_proj"):
    B, n, K = x.shape
    N = w.shape[1]
    tm = min(512, n)
    tn = min(tn, N)
    kern = functools.partial(_proj_kernel, epilogue=epilogue, n_extra=len(extra))
    return pl.pallas_call(
        kern,
        grid=(B, n // tm, N // tn),
        in_specs=[pl.BlockSpec((1, tm, K), lambda b, i, j: (b, i, 0)),
                  pl.BlockSpec((1, K), lambda b, i, j: (0, 0)),
                  pl.BlockSpec((1, 1, K), lambda b, i, j: (b, 0, 0)),
                  pl.BlockSpec((1, 1, K), lambda b, i, j: (b, 0, 0)),
                  pl.BlockSpec((K, tn), lambda b, i, j: (0, j)),
                  pl.BlockSpec((1, tn), lambda b, i, j: (0, j)),
                  *extra_specs],
        out_specs=pl.BlockSpec((1, tm, tn), lambda b, i, j: (b, i, j)),
        out_shape=jax.ShapeDtypeStruct((B, n, N), out_dtype),
        scratch_shapes=[pltpu.VMEM((tm, K), BF16)],
        compiler_params=_cparams("parallel", "parallel", "arbitrary"),
        name=name,
    )(x, g.reshape(1, K), sh, sc, w, bias.reshape(1, N), *extra)


def _resmm_kernel(a_ref, w_ref, b_ref, r_ref, gt_ref, o_ref):
    acc = _dot(a_ref[0], w_ref[...]) + b_ref[...]
    o_ref[0] = r_ref[0] + gt_ref[0] * acc


def res_matmul(a, w, bias, res, gate):
    B, n, K = a.shape
    N = w.shape[1]
    tm = min(512, n)
    tn = min(512, N)
    return pl.pallas_call(
        _resmm_kernel,
        grid=(B, n // tm, N // tn),
        in_specs=[pl.BlockSpec((1, tm, K), lambda b, i, j: (b, i, 0)),
                  pl.BlockSpec((K, tn), lambda b, i, j: (0, j)),
                  pl.BlockSpec((1, tn), lambda b, i, j: (0, j)),
                  pl.BlockSpec((1, tm, tn), lambda b, i, j: (b, i, j)),
                  pl.BlockSpec((1, 1, tn), lambda b, i, j: (b, 0, j))],
        out_specs=pl.BlockSpec((1, tm, tn), lambda b, i, j: (b, i, j)),
        out_shape=jax.ShapeDtypeStruct((B, n, N), F32),
        compiler_params=_cparams("parallel", "parallel", "parallel"),
        name="res_matmul",
    )(a, w, bias.reshape(1, N), res, gate)


def _da_epilogue(acc, j, o_ref, cos_ref, sin_ref, qg_ref, kg_ref, gm_ref, *, tn, n_qk_tiles):
    @pl.when(j >= n_qk_tiles)
    def _():
        o_ref[0] = acc.astype(o_ref.dtype)

    @pl.when(j < n_qk_tiles)
    def _():
        gain = jnp.where(j < n_qk_tiles // 2, qg_ref[...], kg_ref[...])
        cos, sin = cos_ref[...], sin_ref[...]
        lane = lax.broadcasted_iota(I32, (1, LANES), 1)
        first_half = (lane % 32) < 16
        for s in range(tn // LANES):
            y = acc[:, s * LANES:(s + 1) * LANES]
            hi, lo = _split_bf16(y * y)
            gsum = _dot(hi, gm_ref[...]) + _dot(lo, gm_ref[...])
            yn = y * lax.rsqrt(gsum * (1.0 / DA_HEAD_DIM) + EPS) * gain
            partner = jnp.where(first_half, pltpu.roll(yn, LANES - 16, 1), pltpu.roll(yn, 16, 1))
            o_ref[0, :, s * LANES:(s + 1) * LANES] = (yn * cos + partner * sin).astype(o_ref.dtype)


def _attn_kernel(lam_ref, q_ref, kc_ref, vc_ref, *rest, has_lat, tk, n_chunks, out_scale):
    if has_lat:
        kl_ref, vl_ref, sg_ref, o_ref = rest
    else:
        sg_ref, o_ref = rest
    q = q_ref[0]
    tq = q.shape[0]
    lane = lax.broadcasted_iota(I32, (1, LANES), 1)
    zero = jnp.zeros_like(q)
    q2 = jnp.concatenate([jnp.where(lane < DA_HEAD_DIM, q, zero),
                          jnp.where(lane >= DA_HEAD_DIM, q, zero)], axis=0)

    def scores(k):
        return lax.dot_general(q2, k, (((1,), (1,)), ((), ())), preferred_element_type=F32)

    def absorb(s, v, carry):
        m, l, acc = carry
        m_new = jnp.maximum(m, jnp.max(s, axis=-1, keepdims=True))
        alpha = jnp.exp2(m - m_new)
        p = jnp.exp2(s - m_new)
        l_new = alpha * l + jnp.sum(p, axis=-1, keepdims=True)
        acc_new = alpha * acc + _dot(p.astype(BF16), v)
        return m_new, l_new, acc_new

    carry = (jnp.full((2 * tq, 1), -1e30, F32), jnp.zeros((2 * tq, 1), F32),
             jnp.zeros((2 * tq, LANES), F32))
    carry = absorb(scores(kc_ref[0]), vc_ref[0], carry)
    if has_lat:
        def body(c, carry):
            off = pl.multiple_of(c * tk, tk)
            return absorb(scores(kl_ref[0, pl.ds(off, tk), :]), vl_ref[0, pl.ds(off, tk), :], carry)
        carry = lax.fori_loop(0, n_chunks, body, carry, unroll=8 if n_chunks % 8 == 0 else 1)
    _, l, acc = carry
    o = acc / l
    o = o[:tq] - lam_ref[0] * o[tq:]
    ms = jnp.mean(o * o, axis=-1, keepdims=True)
    o_ref[0] = (o * lax.rsqrt(ms + EPS) * sg_ref[...] * out_scale).astype(o_ref.dtype)


def diff_attention_core(lam, qkv_q, qkv_c, qkv_l, sub_g, out_scale):
    B, nq, D3 = qkv_q.shape
    D = D3 // 3
    H = D // DA_V_DIM
    nc = qkv_c.shape[1]
    tq = min(256, nq)
    has_lat = qkv_l is not None
    in_specs = [pl.BlockSpec(memory_space=pltpu.SMEM),
                pl.BlockSpec((1, tq, LANES), lambda b, h, i: (b, i, h)),
                pl.BlockSpec((1, nc, LANES), lambda b, h, i: (b, 0, H + h)),
                pl.BlockSpec((1, nc, LANES), lambda b, h, i: (b, 0, 2 * H + h))]
    args = [lam, qkv_q, qkv_c, qkv_c]
    tk, n_chunks = 0, 0
    if has_lat:
        nl = qkv_l.shape[1]
        tk = min(1024, nl)
        n_chunks = nl // tk
        in_specs += [pl.BlockSpec((1, nl, LANES), lambda b, h, i: (b, 0, H + h)),
                     pl.BlockSpec((1, nl, LANES), lambda b, h, i: (b, 0, 2 * H + h))]
        args += [qkv_l, qkv_l]
    in_specs.append(pl.BlockSpec((1, LANES), lambda b, h, i: (0, 0)))
    args.append(sub_g.reshape(1, LANES))
    kern = functools.partial(_attn_kernel, has_lat=has_lat, tk=tk, n_chunks=n_chunks,
                             out_scale=out_scale)
    return pl.pallas_call(
        kern,
        grid=(B, H, nq // tq),
        in_specs=in_specs,
        out_specs=pl.BlockSpec((1, tq, LANES), lambda b, h, i: (b, i, h)),
        out_shape=jax.ShapeDtypeStruct((B, nq, D), BF16),
        compiler_params=_cparams("parallel", "parallel", "arbitrary"),
        name="diff_attention",
    )(*args)


def _axial_rope_tables(n):
    f = DA_HEAD_DIM // 4
    inv = 1.0 / (ROPE_BASE ** (jnp.arange(f, dtype=F32) / f))
    rows = n // GRID_W
    row = jnp.repeat(jnp.arange(rows), GRID_W).astype(F32)
    col = jnp.tile(jnp.arange(GRID_W), rows).astype(F32)
    ang = jnp.stack([row[:, None] * inv, col[:, None] * inv], axis=1)
    cos, sin = jnp.cos(ang), jnp.sin(ang)
    cos64 = jnp.concatenate([cos, cos], axis=-1).reshape(n, DA_HEAD_DIM)
    sin64 = jnp.concatenate([-sin, sin], axis=-1).reshape(n, DA_HEAD_DIM)
    return jnp.tile(cos64, (1, 2)), jnp.tile(sin64, (1, 2))


def da_project(h_in, g, sh, sc, w_in_bf, q_g, k_g, cos, sin):
    B, n, D = h_in.shape
    tn = 512
    tm = min(512, n)
    gm = (np.arange(LANES)[:, None] // DA_HEAD_DIM == np.arange(LANES)[None, :] // DA_HEAD_DIM)
    gm = jnp.asarray(gm, BF16)
    qg = (jnp.tile(q_g.astype(F32), 2) * (DA_HEAD_DIM ** -0.5 * math.log2(math.e))).reshape(1, LANES)
    kg = jnp.tile(k_g.astype(F32), 2).reshape(1, LANES)
    extra = (cos, sin, qg, kg, gm)
    extra_specs = (pl.BlockSpec((tm, LANES), lambda b, i, j: (i, 0)),
                   pl.BlockSpec((tm, LANES), lambda b, i, j: (i, 0)),
                   pl.BlockSpec((1, LANES), lambda b, i, j: (0, 0)),
                   pl.BlockSpec((1, LANES), lambda b, i, j: (0, 0)),
                   pl.BlockSpec((LANES, LANES), lambda b, i, j: (0, 0)))
    epi = functools.partial(_da_epilogue, tn=tn, n_qk_tiles=2 * D // tn)
    return mod_proj(h_in, g, sh, sc, w_in_bf, jnp.zeros((3 * D,), F32), epilogue=epi, extra=extra,
                    extra_specs=extra_specs, tn=tn, name="da_project")


def diff_attention_layer(x, ctx, mv, norm_g, w_in, q_g, k_g, lam, sub_g, w_out, lambda_init, need_ctx):
    B, L, D = x.shape
    nc = ctx.shape[1]
    w_in_bf = w_in.astype(BF16)
    w_out_bf = w_out.astype(BF16)
    cos_l, sin_l = _axial_rope_tables(L)
    cos_c, sin_c = jnp.ones((nc, LANES), F32), jnp.zeros((nc, LANES), F32)
    qkv_l = da_project(x, norm_g, mv["sh_l"], mv["sc_l"], w_in_bf, q_g, k_g, cos_l, sin_l)
    qkv_c = da_project(ctx, norm_g, mv["sh_c"], mv["sc_c"], w_in_bf, q_g, k_g, cos_c, sin_c)
    lamf = lam.astype(F32)
    lam_val = (jnp.exp(jnp.sum(lamf[0] * lamf[1])) - jnp.exp(jnp.sum(lamf[2] * lamf[3]))
               + lambda_init).reshape(1)
    out_scale = 1.0 - lambda_init
    o_l = diff_attention_core(lam_val, qkv_l, qkv_c, qkv_l, sub_g, out_scale)
    zb = jnp.zeros((D,), F32)
    x = res_matmul(o_l, w_out_bf, zb, x, mv["g_l"])
    if need_ctx:
        o_c = diff_attention_core(lam_val, qkv_c, qkv_c, None, sub_g, out_scale)
        ctx = res_matmul(o_c, w_out_bf, zb, ctx, mv["g_c"])
    return x, ctx


def _router_kernel(x_ref, g_ref, sh_ref, sc_ref, whi_ref, wlo_ref, o_ref):
    h = _modulate(x_ref[0], g_ref[...], sh_ref[0], sc_ref[0])
    o_ref[0] = _router_affinity(h, whi_ref[...], wlo_ref[...])


def _router_affinity(h, w_hi, w_lo):
    h_hi, h_lo = _split_bf16(h)
    logits = _dot3(h_hi, h_lo, w_hi, w_lo)
    lane = lax.broadcasted_iota(I32, (1, LANES), 1)
    logits = jnp.where(lane < N_EXPERTS, logits, -1e30)
    z = jnp.exp(logits - jnp.max(logits, axis=-1, keepdims=True))
    return z / jnp.sum(z, axis=-1, keepdims=True)


def router_affinity(x, g, sh, sc, w_hi, w_lo):
    B, n, D = x.shape
    tm = min(512, n)
    return pl.pallas_call(
        _router_kernel,
        grid=(B, n // tm),
        in_specs=[pl.BlockSpec((1, tm, D), lambda b, i: (b, i, 0)),
                  pl.BlockSpec((1, D), lambda b, i: (0, 0)),
                  pl.BlockSpec((1, 1, D), lambda b, i: (b, 0, 0)),
                  pl.BlockSpec((1, 1, D), lambda b, i: (b, 0, 0)),
                  pl.BlockSpec((D, LANES), lambda b, i: (0, 0)),
                  pl.BlockSpec((D, LANES), lambda b, i: (0, 0))],
        out_specs=pl.BlockSpec((1, tm, LANES), lambda b, i: (b, i, 0)),
        out_shape=jax.ShapeDtypeStruct((B, n, LANES), F32),
        compiler_params=_cparams("parallel", "parallel"),
        name="router",
    )(x, g.reshape(1, D), sh, sc, w_hi, w_lo)


def _ffn_kernel(idx_ref, x_hbm, acc_in, g_ref, sh_ref, sc_ref, g2_ref, rhi_ref, rlo_ref,
                w1_ref, w3_ref, w2_ref, o_hbm, xbuf, abuf, sem, *, n, cap, tm):
    del acc_in
    b, e, j = pl.program_id(0), pl.program_id(1), pl.program_id(2)
    n_exp = pl.num_programs(1)
    base = (b * n_exp + e) * cap + j * tm
    row0 = b * n

    def gather(k, c):
        row = row0 + idx_ref[base + k]
        pltpu.make_async_copy(x_hbm.at[pl.ds(row, 1)], xbuf.at[pl.ds(k, 1)], sem.at[0]).start()
        pltpu.make_async_copy(o_hbm.at[pl.ds(row, 1)], abuf.at[pl.ds(k, 1)], sem.at[1]).start()
        return c

    lax.fori_loop(0, tm, gather, 0)
    pltpu.make_async_copy(x_hbm.at[pl.ds(0, tm)], xbuf, sem.at[0]).wait()

    h = _modulate(xbuf[...], g_ref[...], sh_ref[0], sc_ref[0])
    aff = _router_affinity(h, rhi_ref[...], rlo_ref[...])
    lane = lax.broadcasted_iota(I32, (1, LANES), 1)
    gate = jnp.sum(jnp.where(lane == e, aff, 0.0), axis=-1, keepdims=True)
    hb = h.astype(BF16)
    a1 = _dot(hb, w1_ref[0])
    a3 = _dot(hb, w3_ref[0])
    hid = (a1 * jax.nn.sigmoid(a1) * a3).astype(BF16)
    y = _dot(hid, w2_ref[0])
    upd = g2_ref[0] * (gate * y)

    pltpu.make_async_copy(o_hbm.at[pl.ds(0, tm)], abuf, sem.at[1]).wait()
    abuf[...] = abuf[...] + upd

    def scatter(k, c):
        row = row0 + idx_ref[base + k]
        pltpu.make_async_copy(abuf.at[pl.ds(k, 1)], o_hbm.at[pl.ds(row, 1)], sem.at[2]).start()
        return c

    lax.fori_loop(0, tm, scatter, 0)
    pltpu.make_async_copy(abuf, o_hbm.at[pl.ds(0, tm)], sem.at[2]).wait()


def expert_ffn(idx, x, g, sh, sc, g2, r_hi, r_lo, w1, w3, w2):
    B, n, D = x.shape
    E, cap = idx.shape[1], idx.shape[2]
    F = w1.shape[-1]
    tm = min(512, cap)
    x2 = x.reshape(B * n, D)
    kern = functools.partial(_ffn_kernel, n=n, cap=cap, tm=tm)
    vec = lambda: pl.BlockSpec((1, 1, D), lambda b, e, j, idx: (b, 0, 0))
    grid_spec = pltpu.PrefetchScalarGridSpec(
        num_scalar_prefetch=1,
        grid=(B, E, cap // tm),
        in_specs=[pl.BlockSpec(memory_space=pl.ANY),
                  pl.BlockSpec(memory_space=pl.ANY),
                  pl.BlockSpec((1, D), lambda b, e, j, idx: (0, 0)),
                  vec(), vec(), vec(),
                  pl.BlockSpec((D, LANES), lambda b, e, j, idx: (0, 0)),
                  pl.BlockSpec((D, LANES), lambda b, e, j, idx: (0, 0)),
                  pl.BlockSpec((1, D, F), lambda b, e, j, idx: (e, 0, 0)),
                  pl.BlockSpec((1, D, F), lambda b, e, j, idx: (e, 0, 0)),
                  pl.BlockSpec((1, F, D), lambda b, e, j, idx: (e, 0, 0))],
        out_specs=pl.BlockSpec(memory_space=pl.ANY),
        scratch_shapes=[pltpu.VMEM((tm, D), F32), pltpu.VMEM((tm, D), F32),
                        pltpu.SemaphoreType.DMA((3,))])
    out = pl.pallas_call(
        kern,
        grid_spec=grid_spec,
        out_shape=jax.ShapeDtypeStruct((B * n, D), F32),
        input_output_aliases={2: 0},
        compiler_params=_cparams("arbitrary", "arbitrary", "arbitrary"),
        name="expert_ffn",
    )(idx.reshape(-1), x2, x2, g.reshape(1, D), sh, sc, g2, r_hi, r_lo, w1, w3, w2)
    return out.reshape(B, n, D)


TOPK_ROWS = 128


def _topk_kernel(a_ref, idx_ref, *, k, capp):
    a = a_ref[0]
    n_exp = a.shape[0]
    bits = lax.bitcast_convert_type(a, I32)
    row_i = lax.broadcasted_iota(I32, (1, TOPK_ROWS, LANES), 1)
    lane_i = lax.broadcasted_iota(I32, (1, TOPK_ROWS, LANES), 2)
    tok = row_i * LANES + lane_i

    def count(ind):
        return jnp.sum(jnp.sum(ind, axis=2, keepdims=True), axis=1, keepdims=True)

    thr = jnp.zeros((n_exp, 1, 1), I32)
    for bit in range(30, -1, -1):
        cand = thr | (1 << bit)
        thr = jnp.where(count(jnp.where(bits >= cand, 1, 0)) >= k, cand, thr)
    gt = jnp.where(bits > thr, 1, 0)
    eq = jnp.where(bits == thr, 1, 0)
    need = k - count(gt)
    tcut = jnp.zeros((n_exp, 1, 1), I32)
    for bit in range((TOPK_ROWS * LANES).bit_length() - 2, -1, -1):
        cand = tcut | (1 << bit)
        tcut = jnp.where(count(jnp.where(tok < cand, eq, 0)) < need, cand, tcut)
    sel = (gt + jnp.where(tok <= tcut, eq, 0)).astype(F32)

    tri = jnp.where(lax.broadcasted_iota(I32, (TOPK_ROWS, TOPK_ROWS), 0)
                    >= lax.broadcasted_iota(I32, (TOPK_ROWS, TOPK_ROWS), 1), 1.0, 0.0).astype(BF16)
    slot = lax.broadcasted_iota(I32, (1, capp), 1).astype(F32)
    rows_col = lax.broadcasted_iota(I32, (TOPK_ROWS, 1), 0).astype(F32)
    for e in range(n_exp):
        m = sel[e]
        cnt = jnp.sum(m, axis=1, keepdims=True)
        incl = _dot(tri, jnp.broadcast_to(cnt, (TOPK_ROWS, LANES)).astype(BF16))[:, :1]
        before = incl <= slot
        row_of = jnp.sum(jnp.where(before, 1.0, 0.0), axis=0, keepdims=True)
        base = jnp.sum(jnp.where(before, cnt, 0.0), axis=0, keepdims=True)
        onehot = jnp.where(rows_col == row_of, 1.0, 0.0).astype(BF16)
        picked = _dot(m.T.astype(BF16), onehot)
        cum = _dot(tri, picked.astype(BF16))
        local = jnp.sum(jnp.where(cum <= slot - base, 1.0, 0.0), axis=0, keepdims=True)
        idx_ref[0, e] = (row_of * LANES + local).astype(I32)


def expert_choice_topk(aff, cap):
    B, E, n = aff.shape
    n_pad = TOPK_ROWS * LANES
    assert n <= n_pad and TOPK_ROWS == LANES
    a = jnp.pad(aff, ((0, 0), (0, 0), (0, n_pad - n)), constant_values=-1.0)
    a = a.reshape(B, E, TOPK_ROWS, LANES)
    capp = max(cap, LANES)
    out = pl.pallas_call(
        functools.partial(_topk_kernel, k=cap, capp=capp),
        grid=(B,),
        in_specs=[pl.BlockSpec((1, E, TOPK_ROWS, LANES), lambda b: (b, 0, 0, 0))],
        out_specs=pl.BlockSpec((1, E, 1, capp), lambda b: (b, 0, 0, 0)),
        out_shape=jax.ShapeDtypeStruct((B, E, 1, capp), I32),
        compiler_params=_cparams("parallel"),
        name="expert_topk",
    )(a)
    return out[:, :, 0, :cap]


def moe_layer(x, g, sh, sc, g2, router_w, w1, w3, w2):
    B, n, D = x.shape
    E = router_w.shape[1]
    cap = EC_CAPACITY_FACTOR * n // E
    rw = jnp.pad(router_w.astype(F32), ((0, 0), (0, LANES - E)))
    r_hi, r_lo = _split_bf16(rw)
    aff = router_affinity(x, g, sh, sc, r_hi, r_lo)
    idx = expert_choice_topk(aff[:, :, :E].transpose(0, 2, 1), cap)
    return expert_ffn(idx, x, g, sh, sc, g2, r_hi, r_lo, w1, w3, w2)


def _ret_epilogue(acc, j, o_ref, cos_ref, sin_ref, *, tn, n_q_tiles):
    @pl.when(j >= 2 * n_q_tiles)
    def _():
        o_ref[0] = acc.astype(o_ref.dtype)

    @pl.when(j < 2 * n_q_tiles)
    def _():
        scale = jnp.where(j < n_q_tiles, 1.0, RET_QK_DIM ** -0.5)
        cos, sin = cos_ref[...] * scale, sin_ref[...] * scale
        f = RET_QK_DIM // 2
        for s in range(tn // RET_QK_DIM):
            lo, mid, hi = s * RET_QK_DIM, s * RET_QK_DIM + f, (s + 1) * RET_QK_DIM
            x1, x2 = acc[:, lo:mid], acc[:, mid:hi]
            o_ref[0, :, lo:mid] = (x1 * cos - x2 * sin).astype(o_ref.dtype)
            o_ref[0, :, mid:hi] = (x2 * cos + x1 * sin).astype(o_ref.dtype)


def _retention_kernel(lg_ref, qf_ref, kf_ref, vf_ref, qb_ref, kb_ref, vb_ref, s0f_ref, s0b_ref,
                      of_ref, ob_ref, sf_ref, sb_ref, stf, stb, *, T):
    h, c = pl.program_id(1), pl.program_id(2)

    @pl.when(c == 0)
    def _():
        stf[...] = s0f_ref[0, 0]
        stb[...] = s0b_ref[0, 0]

    ii = lax.broadcasted_iota(I32, (T, T), 0).astype(F32)
    jj = lax.broadcasted_iota(I32, (T, T), 1).astype(F32)
    row = lax.broadcasted_iota(I32, (T, 1), 0).astype(F32)

    def one(q_ref, k_ref, v_ref, st, o_ref, lg, forward):
        q, k, v = q_ref[0], k_ref[0], v_ref[0]
        d = ii - jj if forward else jj - ii
        mask = d >= 0 if forward else d > 0
        dmask = jnp.where(mask, jnp.exp(jnp.where(mask, d, 0.0) * lg), 0.0)
        xi = jnp.exp(((row + 1.0) if forward else (T - row)) * lg)
        zeta = jnp.exp(((T - 1.0 - row) if forward else row) * lg)
        inner = lax.dot_general(q, k, (((1,), (1,)), ((), ())), preferred_element_type=F32) * dmask
        o = _dot(inner.astype(BF16), v) + _dot(q, st[...].astype(BF16)) * xi
        o_ref[0] = o.astype(o_ref.dtype)
        kz = (k.astype(F32) * zeta).astype(BF16)
        decay = jnp.exp(jnp.full((1, 1), T, F32) * lg)
        st[...] = st[...] * decay + lax.dot_general(kz, v, (((0,), (0,)), ((), ())),
                                                    preferred_element_type=F32)

    one(qf_ref, kf_ref, vf_ref, stf, of_ref, lg_ref[0, h], True)
    one(qb_ref, kb_ref, vb_ref, stb, ob_ref, lg_ref[1, h], False)

    @pl.when(c == pl.num_programs(2) - 1)
    def _():
        sf_ref[0, 0] = stf[...]
        sb_ref[0, 0] = stb[...]


def retention_core(log_g, p, s0f, s0b):
    B, n, _ = p.shape
    H, dk, dv = s0f.shape[1], s0f.shape[2], s0f.shape[3]
    T = min(256, n)
    nc = n // T
    kq, kk, kv = 0, (H * dk) // dk, (2 * H * dk) // dv
    fwd = lambda off: (lambda b, h, c: (b, c, off + h))
    bwd = lambda off: (lambda b, h, c: (b, nc - 1 - c, off + h))
    st_spec = pl.BlockSpec((1, 1, dk, dv), lambda b, h, c: (b, h, 0, 0))
    out_sd = jax.ShapeDtypeStruct((B, n, H * dv), BF16)
    st_sd = jax.ShapeDtypeStruct((B, H, dk, dv), F32)
    return pl.pallas_call(
        functools.partial(_retention_kernel, T=T),
        grid=(B, H, nc),
        in_specs=[pl.BlockSpec(memory_space=pltpu.SMEM),
                  pl.BlockSpec((1, T, dk), fwd(kq)), pl.BlockSpec((1, T, dk), fwd(kk)),
                  pl.BlockSpec((1, T, dv), fwd(kv)),
                  pl.BlockSpec((1, T, dk), bwd(kq)), pl.BlockSpec((1, T, dk), bwd(kk)),
                  pl.BlockSpec((1, T, dv), bwd(kv)),
                  st_spec, st_spec],
        out_specs=[pl.BlockSpec((1, T, dv), fwd(0)), pl.BlockSpec((1, T, dv), bwd(0)), st_spec, st_spec],
        out_shape=[out_sd, out_sd, st_sd, st_sd],
        scratch_shapes=[pltpu.VMEM((dk, dv), F32), pltpu.VMEM((dk, dv), F32)],
        compiler_params=_cparams("parallel", "parallel", "arbitrary"),
        name="retention",
    )(log_g, p, p, p, p, p, p, s0f, s0b)


def _ret_finish_kernel(of_ref, ob_ref, g_ref, gn_ref, z_ref, *, dv):
    for s in range(of_ref.shape[-1] // dv):
        sl = slice(s * dv, (s + 1) * dv)
        y = of_ref[0, :, sl].astype(F32) + ob_ref[0, :, sl].astype(F32)
        ms = jnp.mean(y * y, axis=-1, keepdims=True)
        y = y * lax.rsqrt(ms + EPS) * gn_ref[...]
        g = g_ref[0, :, sl].astype(F32)
        z_ref[0, :, sl] = (g * jax.nn.sigmoid(g) * y).astype(z_ref.dtype)


def retention_finish(of, ob, p, gn_g):
    B, n, N = of.shape
    dv = gn_g.shape[0]
    tm = min(512, n)
    blk = lambda off: pl.BlockSpec((1, tm, N), lambda b, i: (b, i, off))
    return pl.pallas_call(
        functools.partial(_ret_finish_kernel, dv=dv),
        grid=(B, n // tm),
        in_specs=[blk(0), blk(0), blk(p.shape[-1] // N - 1), pl.BlockSpec((1, dv), lambda b, i: (0, 0))],
        out_specs=blk(0),
        out_shape=jax.ShapeDtypeStruct((B, n, N), BF16),
        compiler_params=_cparams("parallel", "parallel"),
        name="retention_finish",
    )(of, ob, p, gn_g.astype(F32).reshape(1, dv))


def retention_layer(x, ctx, mv, norm_g, w_in, log_alpha, gn_g, w_out, need_ctx):
    B, L, D = x.shape
    nc = ctx.shape[1]
    dk, dv = RET_QK_DIM, RET_V_DIM
    H = D // dk
    w_in_bf = w_in.astype(BF16)
    w_out_bf = w_out.astype(BF16)
    zb = jnp.zeros((w_in.shape[1],), F32)
    log_g = jnp.log1p(-jnp.exp(log_alpha.astype(F32)))
    f = dk // 2
    inv = 1.0 / (ROPE_BASE ** jnp.linspace(0.0, 1.0, f, dtype=F32))
    ang = jnp.arange(L, dtype=F32)[:, None] * inv
    tn = 512

    def project(h_in, sh, sc, cos, sin):
        tm = min(512, h_in.shape[1])
        specs = (pl.BlockSpec((tm, f), lambda b, i, j: (i, 0)), pl.BlockSpec((tm, f), lambda b, i, j: (i, 0)))
        epi = functools.partial(_ret_epilogue, tn=tn, n_q_tiles=H * dk // tn)
        return mod_proj(h_in, norm_g, sh, sc, w_in_bf, zb, epilogue=epi, extra=(cos, sin),
                        extra_specs=specs, tn=tn, name="ret_project")

    p_c = project(ctx, mv["sh_c"], mv["sc_c"], jnp.ones((nc, f), F32), jnp.zeros((nc, f), F32))
    p_l = project(x, mv["sh_l"], mv["sc_l"], jnp.cos(ang), jnp.sin(ang))
    zero = jnp.zeros((B, H, dk, dv), F32)
    of_c, ob_c, s_f, s_b = retention_core(log_g, p_c, zero, zero)
    of_l, ob_l, _, _ = retention_core(log_g, p_l, s_f, s_b)
    zd = jnp.zeros((D,), F32)
    x = res_matmul(retention_finish(of_l, ob_l, p_l, gn_g), w_out_bf, zd, x, mv["g_l"])
    if need_ctx:
        ctx = res_matmul(retention_finish(of_c, ob_c, p_c, gn_g), w_out_bf, zd, ctx, mv["g_c"])
    return x, ctx


def _hyena_filter_taps(n, w1, b1, w2, b2, freq, w3):
    t = jnp.linspace(0.0, 1.0, n, dtype=F32)[:, None]
    w = 2.0 * math.pi * jnp.arange(n, dtype=F32)[:, None] / n
    f = jnp.linspace(1e-4, HY_BANDS - 1, HY_BANDS, dtype=F32)[None]
    z = jnp.concatenate([t, jnp.cos(f * w), -jnp.sin(f * w)], axis=-1)
    hp = lax.Precision.HIGHEST
    a = jnp.sin(freq[0].astype(F32) * (jnp.dot(z, w1.astype(F32), precision=hp) + b1.astype(F32)))
    a = jnp.sin(freq[1].astype(F32) * (jnp.dot(a, w2.astype(F32), precision=hp) + b2.astype(F32)))
    C = w3.shape[-1] // (2 * HY_ORDER)
    h = jnp.dot(a, w3.astype(F32), precision=hp).reshape(n, HY_ORDER, 2, C)
    max_decay = math.log(HY_TARGET) / HY_FAST_DECAY_PCT
    min_decay = math.log(HY_TARGET) / HY_SLOW_DECAY_PCT
    deltas = jnp.abs(jnp.linspace(min_decay, max_decay, C, dtype=F32))
    return h * jnp.exp(-t * deltas)[:, None, None, :]


def _filter_spectrum_small(h):
    n, _, _, C = h.shape
    buf = jnp.concatenate([h[:, :, 0], jnp.zeros((1, HY_ORDER, C), F32),
                           jnp.flip(h[1:, :, 1], axis=0)], axis=0)
    buf = buf / jnp.sum(jnp.abs(buf), axis=0, keepdims=True)
    hf = jnp.fft.rfft(buf, axis=0)
    return jnp.concatenate([hf, jnp.conj(jnp.flip(hf[1:n], axis=0))], axis=0)


def _fft_a_real_kernel(z_ref, m_ref, o_ref):
    o_ref[...] = _dot(m_ref[...], z_ref[...]).astype(o_ref.dtype)


def _fft_c_filter_kernel(af_ref, ab_ref, gf_ref, inv_ref, o_ref):
    n2, C = af_ref.shape[2], af_ref.shape[3]
    xf = _dot(gf_ref[0], af_ref[:, 0].reshape(2 * n2, C))
    xb = _dot(gf_ref[0], ab_ref[:, 0].reshape(2 * n2, C))
    inv = inv_ref[...]
    o_ref[0, 0] = (xf[:n2] + xb[:n2]) * inv
    o_ref[0, 1] = (xf[n2:] - xb[n2:]) * inv


def _filter_spectrum_fft(h, tabs):
    n, _, _, C = h.shape
    N1, S1, N2 = tabs["N1"], tabs["S1"], tabs["N2"]
    cols = N2 * C
    tc = 2 * C
    fwd = h[:, :, 0]
    bwd = h[:, :, 1].at[0].set(0.0)
    inv_norm = 1.0 / (jnp.sum(jnp.abs(fwd), axis=0) + jnp.sum(jnp.abs(bwd), axis=0))
    m_re = tabs["m_a"][:, :S1]
    out = []
    for o in range(HY_ORDER):
        parts = []
        for taps in (fwd[:, o], bwd[:, o]):
            a = pl.pallas_call(
                _fft_a_real_kernel,
                grid=(cols // tc,),
                in_specs=[pl.BlockSpec((S1, tc), lambda j: (0, j)),
                          pl.BlockSpec((2 * N1, S1), lambda j: (0, 0))],
                out_specs=pl.BlockSpec((2 * N1, tc), lambda j: (0, j)),
                out_shape=jax.ShapeDtypeStruct((2 * N1, cols), BF16),
                compiler_params=_cparams("parallel"),
                name="hy_filter_fft_a",
            )(taps.astype(BF16).reshape(S1, cols), m_re)
            parts.append(a.reshape(2, N1, N2, C))
        ab = pl.BlockSpec((2, 1, N2, C), lambda i: (0, i, 0, 0))
        out.append(pl.pallas_call(
            _fft_c_filter_kernel,
            grid=(N1,),
            in_specs=[ab, ab, pl.BlockSpec((1, 2 * N2, 2 * N2), lambda i: (i, 0, 0)),
                      pl.BlockSpec((1, C), lambda i: (0, 0))],
            out_specs=pl.BlockSpec((1, 2, N2, C), lambda i: (i, 0, 0, 0)),
            out_shape=jax.ShapeDtypeStruct((N1, 2, N2, C), F32),
            compiler_params=_cparams("parallel"),
            name="hy_filter_fft_c",
        )(parts[0], parts[1], tabs["g_f"], inv_norm[o].reshape(1, C)))
    return out


def _short_conv_kernel(u_ref, prev_ref, next_ref, w_ref, b_ref, o_ref):
    i, ni = pl.program_id(1), pl.num_programs(1)
    u = u_ref[0]
    tm = u.shape[0]
    row = lax.broadcasted_iota(I32, (tm, 1), 0)
    prev_row = jnp.where(i > 0, prev_ref[0, 7:8, :], 0.0)
    next_row = jnp.where(i < ni - 1, next_ref[0, 0:1, :], 0.0)
    before = jnp.where(row == 0, prev_row, pltpu.roll(u, 1, 0))
    after = jnp.where(row == tm - 1, next_row, pltpu.roll(u, tm - 1, 0))
    w = w_ref[...]
    o_ref[0, 0] = (before * w[0:1] + u * w[1:2] + after * w[2:3] + b_ref[...]).astype(o_ref.dtype)


def short_conv_split(u, conv_w, conv_b):
    B, n, C3 = u.shape
    C = C3 // 3
    tm = min(512, n)
    r8 = tm // 8
    return pl.pallas_call(
        _short_conv_kernel,
        grid=(B, n // tm, 3),
        in_specs=[pl.BlockSpec((1, tm, C), lambda b, i, j: (b, i, j)),
                  pl.BlockSpec((1, 8, C), lambda b, i, j: (b, jnp.maximum(i * r8 - 1, 0), j)),
                  pl.BlockSpec((1, 8, C), lambda b, i, j: (b, jnp.minimum((i + 1) * r8, n // 8 - 1), j)),
                  pl.BlockSpec((8, C), lambda b, i, j: (0, j)),
                  pl.BlockSpec((1, C), lambda b, i, j: (0, j))],
        out_specs=pl.BlockSpec((1, 1, tm, C), lambda b, i, j: (j, b, i, 0)),
        out_shape=jax.ShapeDtypeStruct((3, B, n, C), BF16),
        compiler_params=_cparams("parallel", "parallel", "parallel"),
        name="hy_short_conv",
    )(u, u, u, jnp.pad(conv_w.astype(F32), ((0, 5), (0, 0))), conv_b.astype(F32).reshape(1, C3))


FFT_N1 = 128


def _stack_complex(re, im):
    return jnp.concatenate([jnp.concatenate([re, -im], axis=-1),
                            jnp.concatenate([im, re], axis=-1)], axis=-2)


def _phase(k, period):
    a = (2.0 * math.pi / period) * (k % period).astype(F32)
    return jnp.cos(a), jnp.sin(a)


def _fft_a_kernel(z_ref, m_ref, o_ref):
    z = z_ref[...]
    o_ref[...] = _dot(m_ref[...], z.reshape(2 * z.shape[1], z.shape[2])).astype(o_ref.dtype)


def _fft_c_kernel(a_ref, gf_ref, gi_ref, h_ref, o_ref):
    n2, C = a_ref.shape[2], a_ref.shape[3]
    x = _dot(gf_ref[0], a_ref[:, 0].reshape(2 * n2, C))
    xr, xi = x[:n2], x[n2:]
    hr, hi = h_ref[0, 0], h_ref[0, 1]
    y = jnp.concatenate([xr * hr - xi * hi, xr * hi + xi * hr], axis=0).astype(BF16)
    o_ref[:, 0] = _dot(gi_ref[0], y).astype(o_ref.dtype).reshape(2, n2, C)


def _fft_b_kernel(b_ref, m_ref, u_ref, xg_ref, bias_ref, o_ref):
    s1, tc = u_ref.shape[1], u_ref.shape[2]
    y = _dot(m_ref[...], b_ref[...]).reshape(2, s1, tc)
    o_ref[...] = (xg_ref[...].astype(F32) * (y + u_ref[...].astype(F32) * bias_ref[...])).astype(o_ref.dtype)


def _dense_conv_kernel(u_ref, mf_ref, mi_ref, h_ref, xg_ref, bias_ref, o_ref):
    n, tc = u_ref.shape[1], u_ref.shape[2]
    u = u_ref[...]
    x = _dot(mf_ref[...], u.reshape(2 * n, tc))
    N = x.shape[0] // 2
    xr, xi = x[:N], x[N:]
    hr, hi = h_ref[0], h_ref[1]
    y = jnp.concatenate([xr * hr - xi * hi, xr * hi + xi * hr], axis=0).astype(BF16)
    out = _dot(mi_ref[...], y).reshape(2, n, tc)
    o_ref[...] = (xg_ref[...].astype(F32) * (out + u.astype(F32) * bias_ref[...])).astype(o_ref.dtype)


def _fft_tables(n):
    N = 2 * n
    N1, S1 = FFT_N1, FFT_N1 // 2
    N2 = N // N1
    f1 = jnp.arange(N1, dtype=I32)
    c, s = _phase(f1[:, None] * jnp.arange(S1, dtype=I32)[None, :], N1)
    m_a = _stack_complex(c, -s).astype(BF16)
    m_b = _stack_complex(c.T / N, s.T / N).astype(BF16)
    f2 = jnp.arange(N2, dtype=I32)
    c, s = _phase(f2[None, None, :] * (f1[:, None, None] + N1 * f2[None, :, None]), N)
    g_f = _stack_complex(c, -s).astype(BF16)
    g_i = _stack_complex(jnp.swapaxes(c, 1, 2), jnp.swapaxes(s, 1, 2)).astype(BF16)
    return dict(N1=N1, S1=S1, N2=N2, m_a=m_a, m_b=m_b, g_f=g_f, g_i=g_i)


def long_conv_gate(u, xg, spec, bias, tabs):
    B, n, C = u.shape
    assert B == 2
    N = 2 * n
    bias3 = bias.astype(F32).reshape(1, 1, C)
    if tabs is None:
        hr, hi = jnp.real(spec).astype(F32), jnp.imag(spec).astype(F32)
        f, t = jnp.arange(N, dtype=I32)[:, None], jnp.arange(n, dtype=I32)[None, :]
        c, s = _phase(f * t, N)
        mf = _stack_complex(c, -s).astype(BF16)
        mi = _stack_complex(c.T / N, s.T / N).astype(BF16)
        tc = 256
        blk = pl.BlockSpec((2, n, tc), lambda j: (0, 0, j))
        return pl.pallas_call(
            _dense_conv_kernel,
            grid=(C // tc,),
            in_specs=[blk, pl.BlockSpec((2 * N, 2 * n), lambda j: (0, 0)),
                      pl.BlockSpec((2 * n, 2 * N), lambda j: (0, 0)),
                      pl.BlockSpec((2, N, tc), lambda j: (0, 0, j)), blk,
                      pl.BlockSpec((1, 1, tc), lambda j: (0, 0, j))],
            out_specs=blk,
            out_shape=jax.ShapeDtypeStruct((2, n, C), BF16),
            compiler_params=_cparams("parallel"),
            name="hy_dense_conv",
        )(u, mf, mi, jnp.stack([hr, hi]), xg, bias3)

    N1, S1, N2 = tabs["N1"], tabs["S1"], tabs["N2"]
    m_a, m_b, g_f, g_i, hperm = tabs["m_a"], tabs["m_b"], tabs["g_f"], tabs["g_i"], spec
    cols = N2 * C
    tc = 2 * C
    zb = pl.BlockSpec((2, S1, tc), lambda j: (0, 0, j))
    a = pl.pallas_call(
        _fft_a_kernel,
        grid=(cols // tc,),
        in_specs=[zb, pl.BlockSpec((2 * N1, 2 * S1), lambda j: (0, 0))],
        out_specs=pl.BlockSpec((2 * N1, tc), lambda j: (0, j)),
        out_shape=jax.ShapeDtypeStruct((2 * N1, cols), BF16),
        compiler_params=_cparams("parallel"),
        name="hy_fft_a",
    )(u.reshape(2, S1, cols), m_a)
    ab = pl.BlockSpec((2, 1, N2, C), lambda i: (0, i, 0, 0))
    gb = pl.BlockSpec((1, 2 * N2, 2 * N2), lambda i: (i, 0, 0))
    bm = pl.pallas_call(
        _fft_c_kernel,
        grid=(N1,),
        in_specs=[ab, gb, gb, pl.BlockSpec((1, 2, N2, C), lambda i: (i, 0, 0, 0))],
        out_specs=ab,
        out_shape=jax.ShapeDtypeStruct((2, N1, N2, C), BF16),
        compiler_params=_cparams("parallel"),
        name="hy_fft_c",
    )(a.reshape(2, N1, N2, C), g_f, g_i, hperm)
    out = pl.pallas_call(
        _fft_b_kernel,
        grid=(cols // tc,),
        in_specs=[pl.BlockSpec((2 * N1, tc), lambda j: (0, j)),
                  pl.BlockSpec((2 * S1, 2 * N1), lambda j: (0, 0)), zb, zb,
                  pl.BlockSpec((1, 1, tc), lambda j: (0, 0, 0))],
        out_specs=zb,
        out_shape=jax.ShapeDtypeStruct((2, S1, cols), BF16),
        compiler_params=_cparams("parallel"),
        name="hy_fft_b",
    )(bm.reshape(2 * N1, cols), m_b, u.reshape(2, S1, cols), xg.reshape(2, S1, cols),
      jnp.tile(bias3, (1, 1, tc // C)))
    return out.reshape(2, n, C)


def hyena_layer(x, ctx, mv, norm_g, hy_p, need_ctx):
    (w_in, b_in, conv_w, conv_b, w1, b1, w2, b2, freq, w3, fbias, w_out, b_out) = hy_p
    w_in_bf = w_in.astype(BF16)
    w_out_bf = w_out.astype(BF16)

    def mix(h_in, sh, sc, res, gate):
        n = h_in.shape[1]
        u = mod_proj(h_in, norm_g, sh, sc, w_in_bf, b_in.astype(F32), out_dtype=F32, name="hy_project")
        v, x1, x2 = short_conv_split(u, conv_w, conv_b)
        taps = _hyena_filter_taps(n, w1, b1, w2, b2, freq, w3)
        if n <= 512:
            tabs = None
            hfull = _filter_spectrum_small(taps)
            spec = [hfull[:, o] for o in range(HY_ORDER)]
        else:
            tabs = _fft_tables(n)
            spec = _filter_spectrum_fft(taps, tabs)
        z = long_conv_gate(v, x1, spec[0], fbias[0], tabs)
        z = long_conv_gate(z, x2, spec[1], fbias[1], tabs)
        return res_matmul(z, w_out_bf, b_out.astype(F32), res, gate)

    x = mix(x, mv["sh_l"], mv["sc_l"], x, mv["g_l"])
    if need_ctx:
        ctx = mix(ctx, mv["sh_c"], mv["sc_c"], ctx, mv["g_c"])
    return x, ctx


def kernel(x, c, ctx, c_ctx, norm_mix_g, norm_ffn_g, mod_w, mod_b,
           da_w_in, da_q_norm, da_k_norm, da_lambda, da_sub_norm, da_w_out,
           ret_w_in, ret_log_alpha, ret_group_norm, ret_w_out,
           hy_w_in, hy_b_in, hy_conv_w, hy_conv_b, hy_ffn_w1, hy_ffn_b1, hy_ffn_w2, hy_ffn_b2,
           hy_sin_freq, hy_ffn_w3, hy_filter_bias, hy_w_out, hy_b_out,
           router_w, exp_w1, exp_w3, exp_w2):
    B = x.shape[0]
    depth = mod_w.shape[0]
    mods = _mod_all(c, c_ctx, mod_w, mod_b, B)
    for i in range(depth):
        last = i == depth - 1
        m, s = i % N_MIXERS, i // N_MIXERS
        mv = mods[i]
        if m == 0:
            lambda_init = 0.8 - 0.6 * math.exp(-0.3 * i)
            x, ctx = diff_attention_layer(x, ctx, mv, norm_mix_g[i], da_w_in[s], da_q_norm[s], da_k_norm[s],
                                          da_lambda[s], da_sub_norm[s], da_w_out[s], lambda_init, not last)
        elif m == 1:
            x, ctx = retention_layer(x, ctx, mv, norm_mix_g[i], ret_w_in[s], ret_log_alpha[s],
                                     ret_group_norm[s], ret_w_out[s], not last)
        else:
            hy_p = (hy_w_in[s], hy_b_in[s], hy_conv_w[s], hy_conv_b[s], hy_ffn_w1[s], hy_ffn_b1[s],
                    hy_ffn_w2[s], hy_ffn_b2[s], hy_sin_freq[s], hy_ffn_w3[s], hy_filter_bias[s],
                    hy_w_out[s], hy_b_out[s])
            x, ctx = hyena_layer(x, ctx, mv, norm_mix_g[i], hy_p, not last)
        w1, w3, w2 = exp_w1[i].astype(BF16), exp_w3[i].astype(BF16), exp_w2[i].astype(BF16)
        x = moe_layer(x, norm_ffn_g[i], mv["sh2_l"], mv["sc2_l"], mv["g2_l"], router_w[i], w1, w3, w2)
        if not last:
            ctx = moe_layer(ctx, norm_ffn_g[i], mv["sh2_c"], mv["sc2_c"], mv["g2_c"], router_w[i], w1, w3, w2)
    return x
```

```python
import functools
import math

import jax
import jax.numpy as jnp
import numpy as np
from jax import lax
from jax.experimental import pallas as pl
from jax.experimental.pallas import tpu as pltpu

F32 = jnp.float32
BF16 = jnp.bfloat16
I32 = jnp.int32

EPS = 1e-6
N_MOD = 6
ROPE_BASE = 10000.0
GRID_W = 64
N_MIXERS = 3

DA_HEAD_DIM = 64
DA_V_DIM = 128
RET_QK_DIM = 256
RET_V_DIM = 512
RET_CHUNK = 128

HY_ORDER = 2
HY_EMB = 33
HY_BANDS = (HY_EMB - 1) // 2
HY_TARGET = 1e-2
HY_FAST_DECAY_PCT = 0.3
HY_SLOW_DECAY_PCT = 1.5

N_EXPERTS = 16
EC_CAPACITY_FACTOR = 2

LANES = 128
VMEM_LIMIT_BYTES = 56 * 1024 * 1024


def _cparams(*sem):
    return pltpu.CompilerParams(dimension_semantics=sem, vmem_limit_bytes=VMEM_LIMIT_BYTES)


def _split_bf16(a):
    hi = a.astype(BF16)
    lo = (a - hi.astype(F32)).astype(BF16)
    return hi, lo


def _dot(a, b):
    return jnp.dot(a, b, preferred_element_type=F32)


def _dot3(a_hi, a_lo, b_hi, b_lo):
    return _dot(a_hi, b_hi) + _dot(a_lo, b_hi) + _dot(a_hi, b_lo)


def _modulate(x, g, sh, sc):
    ms = jnp.mean(x * x, axis=-1, keepdims=True)
    y = x * lax.rsqrt(ms + EPS) * g
    return y * (1.0 + sc) + sh


def _modvec_kernel(a_ref, w_ref, b_ref, o_ref):
    a = a_ref[...]
    a_hi, a_lo = _split_bf16(a * jax.nn.sigmoid(a))
    w_hi, w_lo = _split_bf16(w_ref[0])
    o_ref[0] = _dot3(a_hi, a_lo, w_hi, w_lo) + b_ref[0]


def mod_vectors(a, mod_w, mod_b):
    depth, d, n = mod_w.shape
    tn = 1024
    return pl.pallas_call(
        _modvec_kernel,
        grid=(depth, n // tn),
        in_specs=[pl.BlockSpec((8, d), lambda i, j: (0, 0)),
                  pl.BlockSpec((1, d, tn), lambda i, j: (i, 0, j)),
                  pl.BlockSpec((1, 1, tn), lambda i, j: (i, 0, j))],
        out_specs=pl.BlockSpec((1, 8, tn), lambda i, j: (i, 0, j)),
        out_shape=jax.ShapeDtypeStruct((depth, 8, n), F32),
        compiler_params=_cparams("parallel", "parallel"),
        name="mod_vectors",
    )(a, mod_w, mod_b.reshape(depth, 1, n))


def _mod_all(c, c_ctx, mod_w, mod_b, batch):
    d = c.shape[-1]
    a = jnp.zeros((8, d), F32).at[:batch].set(c).at[batch].set(c_ctx)
    mv = mod_vectors(a, mod_w, mod_b)
    names = ("sh", "sc", "g", "sh2", "sc2", "g2")
    layers = []
    for i in range(mod_w.shape[0]):
        parts = mv[i].reshape(8, N_MOD, d)
        layer = {}
        for k, nm in enumerate(names):
            layer[nm + "_l"] = parts[:batch, k][:, None, :]
            layer[nm + "_c"] = jnp.broadcast_to(parts[batch, k][None, None, :], (batch, 1, d))
        layers.append(layer)
    return layers


def _proj_kernel(x_ref, g_ref, sh_ref, sc_ref, w_ref, b_ref, *rest, epilogue, n_extra):
    extra, o_ref, h_scr = rest[:n_extra], rest[n_extra], rest[n_extra + 1]
    j = pl.program_id(2)

    @pl.when(j == 0)
    def _():
        h_scr[...] = _modulate(x_ref[0], g_ref[...], sh_ref[0], sc_ref[0]).astype(BF16)

    acc = _dot(h_scr[...], w_ref[...]) + b_ref[...]
    epilogue(acc, j, o_ref, *extra)


def _plain_epilogue(acc, j, o_ref):
    o_ref[0] = acc.astype(o_ref.dtype)


def mod_proj(x, g, sh, sc, w, bias, *, epilogue=_plain_epilogue, extra=(), extra_specs=(),
             out_dtype=BF16, tn=512, name="mod_proj"):
    B, n, K = x.shape
    N = w.shape[1]
    tm = min(512, n)
    tn = min(tn, N)
    kern = functools.partial(_proj_kernel, epilogue=epilogue, n_extra=len(extra))
    return pl.pallas_call(
        kern,
        grid=(B, n // tm, N // tn),
        in_specs=[pl.BlockSpec((1, tm, K), lambda b, i, j: (b, i, 0)),
                  pl.BlockSpec((1, K), lambda b, i, j: (0, 0)),
                  pl.BlockSpec((1, 1, K), lambda b, i, j: (b, 0, 0)),
                  pl.BlockSpec((1, 1, K), lambda b, i, j: (b, 0, 0)),
                  pl.BlockSpec((K, tn), lambda b, i, j: (0, j)),
                  pl.BlockSpec((1, tn), lambda b, i, j: (0, j)),
                  *extra_specs],
        out_specs=pl.BlockSpec((1, tm, tn), lambda b, i, j: (b, i, j)),
        out_shape=jax.ShapeDtypeStruct((B, n, N), out_dtype),
        scratch_shapes=[pltpu.VMEM((tm, K), BF16)],
        compiler_params=_cparams("parallel", "parallel", "arbitrary"),
        name=name,
    )(x, g.reshape(1, K), sh, sc, w, bias.reshape(1, N), *extra)


def _resmm_kernel(a_ref, w_ref, b_ref, r_ref, gt_ref, o_ref):
    acc = _dot(a_ref[0], w_ref[...]) + b_ref[...]
    o_ref[0] = r_ref[0] + gt_ref[0] * acc


def res_matmul(a, w, bias, res, gate):
    B, n, K = a.shape
    N = w.shape[1]
    tm = min(512, n)
    tn = min(512, N)
    return pl.pallas_call(
        _resmm_kernel,
        grid=(B, n // tm, N // tn),
        in_specs=[pl.BlockSpec((1, tm, K), lambda b, i, j: (b, i, 0)),
                  pl.BlockSpec((K, tn), lambda b, i, j: (0, j)),
                  pl.BlockSpec((1, tn), lambda b, i, j: (0, j)),
                  pl.BlockSpec((1, tm, tn), lambda b, i, j: (b, i, j)),
                  pl.BlockSpec((1, 1, tn), lambda b, i, j: (b, 0, j))],
        out_specs=pl.BlockSpec((1, tm, tn), lambda b, i, j: (b, i, j)),
        out_shape=jax.ShapeDtypeStruct((B, n, N), F32),
        compiler_params=_cparams("parallel", "parallel", "parallel"),
        name="res_matmul",
    )(a, w, bias.reshape(1, N), res, gate)


def _da_epilogue(acc, j, o_ref, cos_ref, sin_ref, qg_ref, kg_ref, gm_ref, *, tn, n_qk_tiles):
    @pl.when(j >= n_qk_tiles)
    def _():
        o_ref[0] = acc.astype(o_ref.dtype)

    @pl.when(j < n_qk_tiles)
    def _():
        gain = jnp.where(j < n_qk_tiles // 2, qg_ref[...], kg_ref[...])
        cos, sin = cos_ref[...], sin_ref[...]
        lane = lax.broadcasted_iota(I32, (1, LANES), 1)
        first_half = (lane % 32) < 16
        for s in range(tn // LANES):
            y = acc[:, s * LANES:(s + 1) * LANES]
            hi, lo = _split_bf16(y * y)
            gsum = _dot(hi, gm_ref[...]) + _dot(lo, gm_ref[...])
            yn = y * lax.rsqrt(gsum * (1.0 / DA_HEAD_DIM) + EPS) * gain
            partner = jnp.where(first_half, pltpu.roll(yn, LANES - 16, 1), pltpu.roll(yn, 16, 1))
            o_ref[0, :, s * LANES:(s + 1) * LANES] = (yn * cos + partner * sin).astype(o_ref.dtype)


def _attn_kernel(lam_ref, q_ref, kc_ref, vc_ref, *rest, has_lat, tk, n_chunks, out_scale):
    if has_lat:
        kl_ref, vl_ref, sg_ref, o_ref = rest
    else:
        sg_ref, o_ref = rest
    q = q_ref[0]
    tq = q.shape[0]
    lane = lax.broadcasted_iota(I32, (1, LANES), 1)
    zero = jnp.zeros_like(q)
    q2 = jnp.concatenate([jnp.where(lane < DA_HEAD_DIM, q, zero),
                          jnp.where(lane >= DA_HEAD_DIM, q, zero)], axis=0)

    def scores(k):
        return lax.dot_general(q2, k, (((1,), (1,)), ((), ())), preferred_element_type=F32)

    def absorb(s, v, carry):
        m, l, acc = carry
        m_new = jnp.maximum(m, jnp.max(s, axis=-1, keepdims=True))
        alpha = jnp.exp2(m - m_new)
        p = jnp.exp2(s - m_new)
        l_new = alpha * l + jnp.sum(p, axis=-1, keepdims=True)
        acc_new = alpha * acc + _dot(p.astype(BF16), v)
        return m_new, l_new, acc_new

    carry = (jnp.full((2 * tq, 1), -1e30, F32), jnp.zeros((2 * tq, 1), F32),
             jnp.zeros((2 * tq, LANES), F32))
    carry = absorb(scores(kc_ref[0]), vc_ref[0], carry)
    if has_lat:
        def body(c, carry):
            off = pl.multiple_of(c * tk, tk)
            return absorb(scores(kl_ref[0, pl.ds(off, tk), :]), vl_ref[0, pl.ds(off, tk), :], carry)
        carry = lax.fori_loop(0, n_chunks, body, carry, unroll=8 if n_chunks % 8 == 0 else 1)
    _, l, acc = carry
    o = acc / l
    o = o[:tq] - lam_ref[0] * o[tq:]
    ms = jnp.mean(o * o, axis=-1, keepdims=True)
    o_ref[0] = (o * lax.rsqrt(ms + EPS) * sg_ref[...] * out_scale).astype(o_ref.dtype)


def diff_attention_core(lam, qkv_q, qkv_c, qkv_l, sub_g, out_scale):
    B, nq, D3 = qkv_q.shape
    D = D3 // 3
    H = D // DA_V_DIM
    nc = qkv_c.shape[1]
    tq = min(256, nq)
    has_lat = qkv_l is not None
    in_specs = [pl.BlockSpec(memory_space=pltpu.SMEM),
                pl.BlockSpec((1, tq, LANES), lambda b, h, i: (b, i, h)),
                pl.BlockSpec((1, nc, LANES), lambda b, h, i: (b, 0, H + h)),
                pl.BlockSpec((1, nc, LANES), lambda b, h, i: (b, 0, 2 * H + h))]
    args = [lam, qkv_q, qkv_c, qkv_c]
    tk, n_chunks = 0, 0
    if has_lat:
        nl = qkv_l.shape[1]
        tk = min(1024, nl)
        n_chunks = nl // tk
        in_specs += [pl.BlockSpec((1, nl, LANES), lambda b, h, i: (b, 0, H + h)),
                     pl.BlockSpec((1, nl, LANES), lambda b, h, i: (b, 0, 2 * H + h))]
        args += [qkv_l, qkv_l]
    in_specs.append(pl.BlockSpec((1, LANES), lambda b, h, i: (0, 0)))
    args.append(sub_g.reshape(1, LANES))
    kern = functools.partial(_attn_kernel, has_lat=has_lat, tk=tk, n_chunks=n_chunks,
                             out_scale=out_scale)
    return pl.pallas_call(
        kern,
        grid=(B, H, nq // tq),
        in_specs=in_specs,
        out_specs=pl.BlockSpec((1, tq, LANES), lambda b, h, i: (b, i, h)),
        out_shape=jax.ShapeDtypeStruct((B, nq, D), BF16),
        compiler_params=_cparams("parallel", "parallel", "arbitrary"),
        name="diff_attention",
    )(*args)


def _axial_rope_tables(n):
    f = DA_HEAD_DIM // 4
    inv = 1.0 / (ROPE_BASE ** (jnp.arange(f, dtype=F32) / f))
    rows = n // GRID_W
    row = jnp.repeat(jnp.arange(rows), GRID_W).astype(F32)
    col = jnp.tile(jnp.arange(GRID_W), rows).astype(F32)
    ang = jnp.stack([row[:, None] * inv, col[:, None] * inv], axis=1)
    cos, sin = jnp.cos(ang), jnp.sin(ang)
    cos64 = jnp.concatenate([cos, cos], axis=-1).reshape(n, DA_HEAD_DIM)
    sin64 = jnp.concatenate([-sin, sin], axis=-1).reshape(n, DA_HEAD_DIM)
    return jnp.tile(cos64, (1, 2)), jnp.tile(sin64, (1, 2))


def da_project(h_in, g, sh, sc, w_in_bf, q_g, k_g, cos, sin):
    B, n, D = h_in.shape
    tn = 512
    tm = min(512, n)
    gm = (np.arange(LANES)[:, None] // DA_HEAD_DIM == np.arange(LANES)[None, :] // DA_HEAD_DIM)
    gm = jnp.asarray(gm, BF16)
    qg = (jnp.tile(q_g.astype(F32), 2) * (DA_HEAD_DIM ** -0.5 * math.log2(math.e))).reshape(1, LANES)
    kg = jnp.tile(k_g.astype(F32), 2).reshape(1, LANES)
    extra = (cos, sin, qg, kg, gm)
    extra_specs = (pl.BlockSpec((tm, LANES), lambda b, i, j: (i, 0)),
                   pl.BlockSpec((tm, LANES), lambda b, i, j: (i, 0)),
                   pl.BlockSpec((1, LANES), lambda b, i, j: (0, 0)),
                   pl.BlockSpec((1, LANES), lambda b, i, j: (0, 0)),
                   pl.BlockSpec((LANES, LANES), lambda b, i, j: (0, 0)))
    epi = functools.partial(_da_epilogue, tn=tn, n_qk_tiles=2 * D // tn)
    return mod_proj(h_in, g, sh, sc, w_in_bf, jnp.zeros((3 * D,), F32), epilogue=epi, extra=extra,
                    extra_specs=extra_specs, tn=tn, name="da_project")


def diff_attention_layer(x, ctx, mv, norm_g, w_in, q_g, k_g, lam, sub_g, w_out, lambda_init, need_ctx):
    B, L, D = x.shape
    nc = ctx.shape[1]
    w_in_bf = w_in.astype(BF16)
    w_out_bf = w_out.astype(BF16)
    cos_l, sin_l = _axial_rope_tables(L)
    cos_c, sin_c = jnp.ones((nc, LANES), F32), jnp.zeros((nc, LANES), F32)
    qkv_l = da_project(x, norm_g, mv["sh_l"], mv["sc_l"], w_in_bf, q_g, k_g, cos_l, sin_l)
    qkv_c = da_project(ctx, norm_g, mv["sh_c"], mv["sc_c"], w_in_bf, q_g, k_g, cos_c, sin_c)
    lamf = lam.astype(F32)
    lam_val = (jnp.exp(jnp.sum(lamf[0] * lamf[1])) - jnp.exp(jnp.sum(lamf[2] * lamf[3]))
               + lambda_init).reshape(1)
    out_scale = 1.0 - lambda_init
    o_l = diff_attention_core(lam_val, qkv_l, qkv_c, qkv_l, sub_g, out_scale)
    zb = jnp.zeros((D,), F32)
    x = res_matmul(o_l, w_out_bf, zb, x, mv["g_l"])
    if need_ctx:
        o_c = diff_attention_core(lam_val, qkv_c, qkv_c, None, sub_g, out_scale)
        ctx = res_matmul(o_c, w_out_bf, zb, ctx, mv["g_c"])
    return x, ctx


def _router_kernel(x_ref, g_ref, sh_ref, sc_ref, whi_ref, wlo_ref, o_ref):
    h = _modulate(x_ref[0], g_ref[...], sh_ref[0], sc_ref[0])
    o_ref[0] = _router_affinity(h, whi_ref[...], wlo_ref[...])


def _router_affinity(h, w_hi, w_lo):
    h_hi, h_lo = _split_bf16(h)
    logits = _dot3(h_hi, h_lo, w_hi, w_lo)
    lane = lax.broadcasted_iota(I32, (1, LANES), 1)
    logits = jnp.where(lane < N_EXPERTS, logits, -1e30)
    z = jnp.exp(logits - jnp.max(logits, axis=-1, keepdims=True))
    return z / jnp.sum(z, axis=-1, keepdims=True)


def router_affinity(x, g, sh, sc, w_hi, w_lo):
    B, n, D = x.shape
    tm = min(512, n)
    return pl.pallas_call(
        _router_kernel,
        grid=(B, n // tm),
        in_specs=[pl.BlockSpec((1, tm, D), lambda b, i: (b, i, 0)),
                  pl.BlockSpec((1, D), lambda b, i: (0, 0)),
                  pl.BlockSpec((1, 1, D), lambda b, i: (b, 0, 0)),
                  pl.BlockSpec((1, 1, D), lambda b, i: (b, 0, 0)),
                  pl.BlockSpec((D, LANES), lambda b, i: (0, 0)),
                  pl.BlockSpec((D, LANES), lambda b, i: (0, 0))],
        out_specs=pl.BlockSpec((1, tm, LANES), lambda b, i: (b, i, 0)),
        out_shape=jax.ShapeDtypeStruct((B, n, LANES), F32),
        compiler_params=_cparams("parallel", "parallel"),
        name="router",
    )(x, g.reshape(1, D), sh, sc, w_hi, w_lo)


def _ffn_kernel(idx_ref, x_hbm, acc_in, g_ref, sh_ref, sc_ref, g2_ref, rhi_ref, rlo_ref,
                w1_ref, w3_ref, w2_ref, o_hbm, xbuf, abuf, obuf, sem_x, sem_a, sem_o, *, n, cap, tm):
    del acc_in
    b, e = pl.program_id(0), pl.program_id(1)
    base = (b * pl.num_programs(1) + e) * cap
    row0 = b * n
    n_tiles = cap // tm

    def start_gather(j, slot):
        def body(k, c):
            row = row0 + idx_ref[base + j * tm + k]
            pltpu.make_async_copy(x_hbm.at[pl.ds(row, 1)], xbuf.at[slot, pl.ds(k, 1)], sem_x.at[slot]).start()
            pltpu.make_async_copy(o_hbm.at[pl.ds(row, 1)], abuf.at[slot, pl.ds(k, 1)], sem_a.at[slot]).start()
            return c
        lax.fori_loop(0, tm, body, 0)

    def wait_gather(slot):
        pltpu.make_async_copy(x_hbm.at[pl.ds(0, tm)], xbuf.at[slot], sem_x.at[slot]).wait()
        pltpu.make_async_copy(o_hbm.at[pl.ds(0, tm)], abuf.at[slot], sem_a.at[slot]).wait()

    def start_scatter(j, slot):
        def body(k, c):
            row = row0 + idx_ref[base + j * tm + k]
            pltpu.make_async_copy(obuf.at[slot, pl.ds(k, 1)], o_hbm.at[pl.ds(row, 1)], sem_o.at[slot]).start()
            return c
        lax.fori_loop(0, tm, body, 0)

    def wait_scatter(slot):
        pltpu.make_async_copy(obuf.at[slot], o_hbm.at[pl.ds(0, tm)], sem_o.at[slot]).wait()

    lane = lax.broadcasted_iota(I32, (1, LANES), 1)
    start_gather(0, 0)
    for j in range(n_tiles):
        slot = j % 2
        if j + 1 < n_tiles:
            start_gather(j + 1, 1 - slot)
        wait_gather(slot)
        h = _modulate(xbuf[slot], g_ref[...], sh_ref[0], sc_ref[0])
        aff = _router_affinity(h, rhi_ref[...], rlo_ref[...])
        gate = jnp.sum(jnp.where(lane == e, aff, 0.0), axis=-1, keepdims=True)
        hb = h.astype(BF16)
        a1 = _dot(hb, w1_ref[0])
        a3 = _dot(hb, w3_ref[0])
        hid = (a1 * jax.nn.sigmoid(a1) * a3).astype(BF16)
        y = _dot(hid, w2_ref[0])
        if j >= 2:
            wait_scatter(slot)
        obuf[slot] = abuf[slot] + g2_ref[0] * (gate * y)
        start_scatter(j, slot)
    for j in range(max(n_tiles - 2, 0), n_tiles):
        wait_scatter(j % 2)


def expert_ffn(idx, x, g, sh, sc, g2, r_hi, r_lo, w1, w3, w2):
    B, n, D = x.shape
    E, cap = idx.shape[1], idx.shape[2]
    F = w1.shape[-1]
    tm = min(256, cap)
    x2 = x.reshape(B * n, D)
    kern = functools.partial(_ffn_kernel, n=n, cap=cap, tm=tm)
    vec = lambda: pl.BlockSpec((1, 1, D), lambda b, e, idx: (b, 0, 0))
    buf = lambda: pltpu.VMEM((2, tm, D), F32)
    grid_spec = pltpu.PrefetchScalarGridSpec(
        num_scalar_prefetch=1,
        grid=(B, E),
        in_specs=[pl.BlockSpec(memory_space=pl.ANY),
                  pl.BlockSpec(memory_space=pl.ANY),
                  pl.BlockSpec((1, D), lambda b, e, idx: (0, 0)),
                  vec(), vec(), vec(),
                  pl.BlockSpec((D, LANES), lambda b, e, idx: (0, 0)),
                  pl.BlockSpec((D, LANES), lambda b, e, idx: (0, 0)),
                  pl.BlockSpec((1, D, F), lambda b, e, idx: (e, 0, 0)),
                  pl.BlockSpec((1, D, F), lambda b, e, idx: (e, 0, 0)),
                  pl.BlockSpec((1, F, D), lambda b, e, idx: (e, 0, 0))],
        out_specs=pl.BlockSpec(memory_space=pl.ANY),
        scratch_shapes=[buf(), buf(), buf(), pltpu.SemaphoreType.DMA((2,)),
                        pltpu.SemaphoreType.DMA((2,)), pltpu.SemaphoreType.DMA((2,))])
    out = pl.pallas_call(
        kern,
        grid_spec=grid_spec,
        out_shape=jax.ShapeDtypeStruct((B * n, D), F32),
        input_output_aliases={2: 0},
        compiler_params=_cparams("arbitrary", "arbitrary"),
        name="expert_ffn",
    )(idx.reshape(-1), x2, x2, g.reshape(1, D), sh, sc, g2, r_hi, r_lo, w1, w3, w2)
    return out.reshape(B, n, D)


TOPK_ROWS = 128


def _topk_kernel(a_ref, idx_ref, *, k, capp):
    a = a_ref[0]
    n_exp = a.shape[0]
    bits = lax.bitcast_convert_type(a, I32)
    row_i = lax.broadcasted_iota(I32, (1, TOPK_ROWS, LANES), 1)
    lane_i = lax.broadcasted_iota(I32, (1, TOPK_ROWS, LANES), 2)
    tok = row_i * LANES + lane_i

    def count(ind):
        return jnp.sum(jnp.sum(ind, axis=2, keepdims=True), axis=1, keepdims=True)

    thr = jnp.zeros((n_exp, 1, 1), I32)
    for bit in range(30, -1, -1):
        cand = thr | (1 << bit)
        thr = jnp.where(count(jnp.where(bits >= cand, 1, 0)) >= k, cand, thr)
    gt = jnp.where(bits > thr, 1, 0)
    eq = jnp.where(bits == thr, 1, 0)
    need = k - count(gt)
    tcut = jnp.zeros((n_exp, 1, 1), I32)
    for bit in range((TOPK_ROWS * LANES).bit_length() - 2, -1, -1):
        cand = tcut | (1 << bit)
        tcut = jnp.where(count(jnp.where(tok < cand, eq, 0)) < need, cand, tcut)
    sel = (gt + jnp.where(tok <= tcut, eq, 0)).astype(F32)

    tri = jnp.where(lax.broadcasted_iota(I32, (TOPK_ROWS, TOPK_ROWS), 0)
                    >= lax.broadcasted_iota(I32, (TOPK_ROWS, TOPK_ROWS), 1), 1.0, 0.0).astype(BF16)
    slot = lax.broadcasted_iota(I32, (1, capp), 1).astype(F32)
    rows_col = lax.broadcasted_iota(I32, (TOPK_ROWS, 1), 0).astype(F32)
    for e in range(n_exp):
        m = sel[e]
        cnt = jnp.sum(m, axis=1, keepdims=True)
        incl = _dot(tri, jnp.broadcast_to(cnt, (TOPK_ROWS, LANES)).astype(BF16))[:, :1]
        before = incl <= slot
        row_of = jnp.sum(jnp.where(before, 1.0, 0.0), axis=0, keepdims=True)
        base = jnp.sum(jnp.where(before, cnt, 0.0), axis=0, keepdims=True)
        onehot = jnp.where(rows_col == row_of, 1.0, 0.0).astype(BF16)
        picked = _dot(m.T.astype(BF16), onehot)
        cum = _dot(tri, picked.astype(BF16))
        local = jnp.sum(jnp.where(cum <= slot - base, 1.0, 0.0), axis=0, keepdims=True)
        idx_ref[0, e] = (row_of * LANES + local).astype(I32)


def expert_choice_topk(aff, cap):
    B, E, n = aff.shape
    n_pad = TOPK_ROWS * LANES
    assert n <= n_pad and TOPK_ROWS == LANES
    a = jnp.pad(aff, ((0, 0), (0, 0), (0, n_pad - n)), constant_values=-1.0)
    a = a.reshape(B, E, TOPK_ROWS, LANES)
    capp = max(cap, LANES)
    out = pl.pallas_call(
        functools.partial(_topk_kernel, k=cap, capp=capp),
        grid=(B,),
        in_specs=[pl.BlockSpec((1, E, TOPK_ROWS, LANES), lambda b: (b, 0, 0, 0))],
        out_specs=pl.BlockSpec((1, E, 1, capp), lambda b: (b, 0, 0, 0)),
        out_shape=jax.ShapeDtypeStruct((B, E, 1, capp), I32),
        compiler_params=_cparams("parallel"),
        name="expert_topk",
    )(a)
    return out[:, :, 0, :cap]


def moe_layer(x, g, sh, sc, g2, router_w, w1, w3, w2):
    B, n, D = x.shape
    E = router_w.shape[1]
    cap = EC_CAPACITY_FACTOR * n // E
    rw = jnp.pad(router_w.astype(F32), ((0, 0), (0, LANES - E)))
    r_hi, r_lo = _split_bf16(rw)
    aff = router_affinity(x, g, sh, sc, r_hi, r_lo)
    idx = expert_choice_topk(aff[:, :, :E].transpose(0, 2, 1), cap)
    return expert_ffn(idx, x, g, sh, sc, g2, r_hi, r_lo, w1, w3, w2)


def _ret_epilogue(acc, j, o_ref, cos_ref, sin_ref, *, tn, n_q_tiles):
    @pl.when(j >= 2 * n_q_tiles)
    def _():
        o_ref[0] = acc.astype(o_ref.dtype)

    @pl.when(j < 2 * n_q_tiles)
    def _():
        scale = jnp.where(j < n_q_tiles, 1.0, RET_QK_DIM ** -0.5)
        cos, sin = cos_ref[...] * scale, sin_ref[...] * scale
        f = RET_QK_DIM // 2
        for s in range(tn // RET_QK_DIM):
            lo, mid, hi = s * RET_QK_DIM, s * RET_QK_DIM + f, (s + 1) * RET_QK_DIM
            x1, x2 = acc[:, lo:mid], acc[:, mid:hi]
            o_ref[0, :, lo:mid] = (x1 * cos - x2 * sin).astype(o_ref.dtype)
            o_ref[0, :, mid:hi] = (x2 * cos + x1 * sin).astype(o_ref.dtype)


def _retention_kernel(lg_ref, qf_ref, kf_ref, vf_ref, qb_ref, kb_ref, vb_ref, s0f_ref, s0b_ref,
                      of_ref, ob_ref, sf_ref, sb_ref, stf, stb, *, T):
    h, c = pl.program_id(1), pl.program_id(2)

    @pl.when(c == 0)
    def _():
        stf[...] = s0f_ref[0, 0]
        stb[...] = s0b_ref[0, 0]

    ii = lax.broadcasted_iota(I32, (T, T), 0).astype(F32)
    jj = lax.broadcasted_iota(I32, (T, T), 1).astype(F32)
    row = lax.broadcasted_iota(I32, (T, 1), 0).astype(F32)

    def one(q_ref, k_ref, v_ref, st, o_ref, lg, forward):
        q, k, v = q_ref[0], k_ref[0], v_ref[0]
        d = ii - jj if forward else jj - ii
        mask = d >= 0 if forward else d > 0
        dmask = jnp.where(mask, jnp.exp(jnp.where(mask, d, 0.0) * lg), 0.0)
        xi = jnp.exp(((row + 1.0) if forward else (T - row)) * lg)
        zeta = jnp.exp(((T - 1.0 - row) if forward else row) * lg)
        inner = lax.dot_general(q, k, (((1,), (1,)), ((), ())), preferred_element_type=F32) * dmask
        o = _dot(inner.astype(BF16), v) + _dot(q, st[...].astype(BF16)) * xi
        o_ref[0] = o.astype(o_ref.dtype)
        kz = (k.astype(F32) * zeta).astype(BF16)
        decay = jnp.exp(jnp.full((1, 1), T, F32) * lg)
        st[...] = st[...] * decay + lax.dot_general(kz, v, (((0,), (0,)), ((), ())),
                                                    preferred_element_type=F32)

    one(qf_ref, kf_ref, vf_ref, stf, of_ref, lg_ref[0, h], True)
    one(qb_ref, kb_ref, vb_ref, stb, ob_ref, lg_ref[1, h], False)

    @pl.when(c == pl.num_programs(2) - 1)
    def _():
        sf_ref[0, 0] = stf[...]
        sb_ref[0, 0] = stb[...]


def retention_core(log_g, p, s0f, s0b):
    B, n, _ = p.shape
    H, dk, dv = s0f.shape[1], s0f.shape[2], s0f.shape[3]
    T = min(256, n)
    nc = n // T
    kq, kk, kv = 0, (H * dk) // dk, (2 * H * dk) // dv
    fwd = lambda off: (lambda b, h, c: (b, c, off + h))
    bwd = lambda off: (lambda b, h, c: (b, nc - 1 - c, off + h))
    st_spec = pl.BlockSpec((1, 1, dk, dv), lambda b, h, c: (b, h, 0, 0))
    out_sd = jax.ShapeDtypeStruct((B, n, H * dv), BF16)
    st_sd = jax.ShapeDtypeStruct((B, H, dk, dv), F32)
    return pl.pallas_call(
        functools.partial(_retention_kernel, T=T),
        grid=(B, H, nc),
        in_specs=[pl.BlockSpec(memory_space=pltpu.SMEM),
                  pl.BlockSpec((1, T, dk), fwd(kq)), pl.BlockSpec((1, T, dk), fwd(kk)),
                  pl.BlockSpec((1, T, dv), fwd(kv)),
                  pl.BlockSpec((1, T, dk), bwd(kq)), pl.BlockSpec((1, T, dk), bwd(kk)),
                  pl.BlockSpec((1, T, dv), bwd(kv)),
                  st_spec, st_spec],
        out_specs=[pl.BlockSpec((1, T, dv), fwd(0)), pl.BlockSpec((1, T, dv), bwd(0)), st_spec, st_spec],
        out_shape=[out_sd, out_sd, st_sd, st_sd],
        scratch_shapes=[pltpu.VMEM((dk, dv), F32), pltpu.VMEM((dk, dv), F32)],
        compiler_params=_cparams("parallel", "parallel", "arbitrary"),
        name="retention",
    )(log_g, p, p, p, p, p, p, s0f, s0b)


def _ret_finish_kernel(of_ref, ob_ref, g_ref, gn_ref, z_ref, *, dv):
    for s in range(of_ref.shape[-1] // dv):
        sl = slice(s * dv, (s + 1) * dv)
        y = of_ref[0, :, sl].astype(F32) + ob_ref[0, :, sl].astype(F32)
        ms = jnp.mean(y * y, axis=-1, keepdims=True)
        y = y * lax.rsqrt(ms + EPS) * gn_ref[...]
        g = g_ref[0, :, sl].astype(F32)
        z_ref[0, :, sl] = (g * jax.nn.sigmoid(g) * y).astype(z_ref.dtype)


def retention_finish(of, ob, p, gn_g):
    B, n, N = of.shape
    dv = gn_g.shape[0]
    tm = min(512, n)
    blk = lambda off: pl.BlockSpec((1, tm, N), lambda b, i: (b, i, off))
    return pl.pallas_call(
        functools.partial(_ret_finish_kernel, dv=dv),
        grid=(B, n // tm),
        in_specs=[blk(0), blk(0), blk(p.shape[-1] // N - 1), pl.BlockSpec((1, dv), lambda b, i: (0, 0))],
        out_specs=blk(0),
        out_shape=jax.ShapeDtypeStruct((B, n, N), BF16),
        compiler_params=_cparams("parallel", "parallel"),
        name="retention_finish",
    )(of, ob, p, gn_g.astype(F32).reshape(1, dv))


def retention_layer(x, ctx, mv, norm_g, w_in, log_alpha, gn_g, w_out, need_ctx):
    B, L, D = x.shape
    nc = ctx.shape[1]
    dk, dv = RET_QK_DIM, RET_V_DIM
    H = D // dk
    w_in_bf = w_in.astype(BF16)
    w_out_bf = w_out.astype(BF16)
    zb = jnp.zeros((w_in.shape[1],), F32)
    log_g = jnp.log1p(-jnp.exp(log_alpha.astype(F32)))
    f = dk // 2
    inv = 1.0 / (ROPE_BASE ** jnp.linspace(0.0, 1.0, f, dtype=F32))
    ang = jnp.arange(L, dtype=F32)[:, None] * inv
    tn = 512

    def project(h_in, sh, sc, cos, sin):
        tm = min(512, h_in.shape[1])
        specs = (pl.BlockSpec((tm, f), lambda b, i, j: (i, 0)), pl.BlockSpec((tm, f), lambda b, i, j: (i, 0)))
        epi = functools.partial(_ret_epilogue, tn=tn, n_q_tiles=H * dk // tn)
        return mod_proj(h_in, norm_g, sh, sc, w_in_bf, zb, epilogue=epi, extra=(cos, sin),
                        extra_specs=specs, tn=tn, name="ret_project")

    p_c = project(ctx, mv["sh_c"], mv["sc_c"], jnp.ones((nc, f), F32), jnp.zeros((nc, f), F32))
    p_l = project(x, mv["sh_l"], mv["sc_l"], jnp.cos(ang), jnp.sin(ang))
    zero = jnp.zeros((B, H, dk, dv), F32)
    of_c, ob_c, s_f, s_b = retention_core(log_g, p_c, zero, zero)
    of_l, ob_l, _, _ = retention_core(log_g, p_l, s_f, s_b)
    zd = jnp.zeros((D,), F32)
    x = res_matmul(retention_finish(of_l, ob_l, p_l, gn_g), w_out_bf, zd, x, mv["g_l"])
    if need_ctx:
        ctx = res_matmul(retention_finish(of_c, ob_c, p_c, gn_g), w_out_bf, zd, ctx, mv["g_c"])
    return x, ctx


def _hyena_filter_taps(n, w1, b1, w2, b2, freq, w3):
    t = jnp.linspace(0.0, 1.0, n, dtype=F32)[:, None]
    w = 2.0 * math.pi * jnp.arange(n, dtype=F32)[:, None] / n
    f = jnp.linspace(1e-4, HY_BANDS - 1, HY_BANDS, dtype=F32)[None]
    z = jnp.concatenate([t, jnp.cos(f * w), -jnp.sin(f * w)], axis=-1)
    hp = lax.Precision.HIGHEST
    a = jnp.sin(freq[0].astype(F32) * (jnp.dot(z, w1.astype(F32), precision=hp) + b1.astype(F32)))
    a = jnp.sin(freq[1].astype(F32) * (jnp.dot(a, w2.astype(F32), precision=hp) + b2.astype(F32)))
    C = w3.shape[-1] // (2 * HY_ORDER)
    h = jnp.dot(a, w3.astype(F32), precision=hp).reshape(n, HY_ORDER, 2, C)
    max_decay = math.log(HY_TARGET) / HY_FAST_DECAY_PCT
    min_decay = math.log(HY_TARGET) / HY_SLOW_DECAY_PCT
    deltas = jnp.abs(jnp.linspace(min_decay, max_decay, C, dtype=F32))
    return h * jnp.exp(-t * deltas)[:, None, None, :]


def _filter_spectrum_small(h):
    n, _, _, C = h.shape
    buf = jnp.concatenate([h[:, :, 0], jnp.zeros((1, HY_ORDER, C), F32),
                           jnp.flip(h[1:, :, 1], axis=0)], axis=0)
    buf = buf / jnp.sum(jnp.abs(buf), axis=0, keepdims=True)
    hf = jnp.fft.rfft(buf, axis=0)
    return jnp.concatenate([hf, jnp.conj(jnp.flip(hf[1:n], axis=0))], axis=0)


def _fft_a_real_kernel(z_ref, m_ref, o_ref):
    o_ref[...] = _dot(m_ref[...], z_ref[...]).astype(o_ref.dtype)


def _fft_c_filter_kernel(af_ref, ab_ref, gf_ref, inv_ref, o_ref):
    n2, C = af_ref.shape[2], af_ref.shape[3]
    xf = _dot(gf_ref[0], af_ref[:, 0].reshape(2 * n2, C))
    xb = _dot(gf_ref[0], ab_ref[:, 0].reshape(2 * n2, C))
    inv = inv_ref[...]
    o_ref[0, 0] = (xf[:n2] + xb[:n2]) * inv
    o_ref[0, 1] = (xf[n2:] - xb[n2:]) * inv


def _filter_spectrum_fft(h, tabs):
    n, _, _, C = h.shape
    N1, S1, N2 = tabs["N1"], tabs["S1"], tabs["N2"]
    cols = N2 * C
    tc = 2 * C
    fwd = h[:, :, 0]
    bwd = h[:, :, 1].at[0].set(0.0)
    inv_norm = 1.0 / (jnp.sum(jnp.abs(fwd), axis=0) + jnp.sum(jnp.abs(bwd), axis=0))
    m_re = tabs["m_a"][:, :S1]
    out = []
    for o in range(HY_ORDER):
        parts = []
        for taps in (fwd[:, o], bwd[:, o]):
            a = pl.pallas_call(
                _fft_a_real_kernel,
                grid=(cols // tc,),
                in_specs=[pl.BlockSpec((S1, tc), lambda j: (0, j)),
                          pl.BlockSpec((2 * N1, S1), lambda j: (0, 0))],
                out_specs=pl.BlockSpec((2 * N1, tc), lambda j: (0, j)),
                out_shape=jax.ShapeDtypeStruct((2 * N1, cols), BF16),
                compiler_params=_cparams("parallel"),
                name="hy_filter_fft_a",
            )(taps.astype(BF16).reshape(S1, cols), m_re)
            parts.append(a.reshape(2, N1, N2, C))
        ab = pl.BlockSpec((2, 1, N2, C), lambda i: (0, i, 0, 0))
        out.append(pl.pallas_call(
            _fft_c_filter_kernel,
            grid=(N1,),
            in_specs=[ab, ab, pl.BlockSpec((1, 2 * N2, 2 * N2), lambda i: (i, 0, 0)),
                      pl.BlockSpec((1, C), lambda i: (0, 0))],
            out_specs=pl.BlockSpec((1, 2, N2, C), lambda i: (i, 0, 0, 0)),
            out_shape=jax.ShapeDtypeStruct((N1, 2, N2, C), F32),
            compiler_params=_cparams("parallel"),
            name="hy_filter_fft_c",
        )(parts[0], parts[1], tabs["g_f"], inv_norm[o].reshape(1, C)))
    return out


def _short_conv_kernel(u_ref, prev_ref, next_ref, w_ref, b_ref, o_ref):
    i, ni = pl.program_id(1), pl.num_programs(1)
    u = u_ref[0]
    tm = u.shape[0]
    row = lax.broadcasted_iota(I32, (tm, 1), 0)
    prev_row = jnp.where(i > 0, prev_ref[0, 7:8, :], 0.0)
    next_row = jnp.where(i < ni - 1, next_ref[0, 0:1, :], 0.0)
    before = jnp.where(row == 0, prev_row, pltpu.roll(u, 1, 0))
    after = jnp.where(row == tm - 1, next_row, pltpu.roll(u, tm - 1, 0))
    w = w_ref[...]
    o_ref[0, 0] = (before * w[0:1] + u * w[1:2] + after * w[2:3] + b_ref[...]).astype(o_ref.dtype)


def short_conv_split(u, conv_w, conv_b):
    B, n, C3 = u.shape
    C = C3 // 3
    tm = min(512, n)
    r8 = tm // 8
    return pl.pallas_call(
        _short_conv_kernel,
        grid=(B, n // tm, 3),
        in_specs=[pl.BlockSpec((1, tm, C), lambda b, i, j: (b, i, j)),
                  pl.BlockSpec((1, 8, C), lambda b, i, j: (b, jnp.maximum(i * r8 - 1, 0), j)),
                  pl.BlockSpec((1, 8, C), lambda b, i, j: (b, jnp.minimum((i + 1) * r8, n // 8 - 1), j)),
                  pl.BlockSpec((8, C), lambda b, i, j: (0, j)),
                  pl.BlockSpec((1, C), lambda b, i, j: (0, j))],
        out_specs=pl.BlockSpec((1, 1, tm, C), lambda b, i, j: (j, b, i, 0)),
        out_shape=jax.ShapeDtypeStruct((3, B, n, C), BF16),
        compiler_params=_cparams("parallel", "parallel", "parallel"),
        name="hy_short_conv",
    )(u, u, u, jnp.pad(conv_w.astype(F32), ((0, 5), (0, 0))), conv_b.astype(F32).reshape(1, C3))


FFT_N1 = 128


def _stack_complex(re, im):
    return jnp.concatenate([jnp.concatenate([re, -im], axis=-1),
                            jnp.concatenate([im, re], axis=-1)], axis=-2)


def _phase(k, period):
    a = (2.0 * math.pi / period) * (k % period).astype(F32)
    return jnp.cos(a), jnp.sin(a)


def _fft_a_kernel(z_ref, m_ref, o_ref):
    z = z_ref[...]
    o_ref[...] = _dot(m_ref[...], z.reshape(2 * z.shape[1], z.shape[2])).astype(o_ref.dtype)


def _fft_c_kernel(a_ref, gf_ref, gi_ref, h_ref, o_ref):
    n2, C = a_ref.shape[2], a_ref.shape[3]
    x = _dot(gf_ref[0], a_ref[:, 0].reshape(2 * n2, C))
    xr, xi = x[:n2], x[n2:]
    hr, hi = h_ref[0, 0], h_ref[0, 1]
    y = jnp.concatenate([xr * hr - xi * hi, xr * hi + xi * hr], axis=0).astype(BF16)
    o_ref[:, 0] = _dot(gi_ref[0], y).astype(o_ref.dtype).reshape(2, n2, C)


def _fft_b_kernel(b_ref, m_ref, u_ref, xg_ref, bias_ref, o_ref):
    s1, tc = u_ref.shape[1], u_ref.shape[2]
    y = _dot(m_ref[...], b_ref[...]).reshape(2, s1, tc)
    o_ref[...] = (xg_ref[...].astype(F32) * (y + u_ref[...].astype(F32) * bias_ref[...])).astype(o_ref.dtype)


def _dense_conv_kernel(u_ref, mf_ref, mi_ref, h_ref, xg_ref, bias_ref, o_ref):
    n, tc = u_ref.shape[1], u_ref.shape[2]
    u = u_ref[...]
    x = _dot(mf_ref[...], u.reshape(2 * n, tc))
    N = x.shape[0] // 2
    xr, xi = x[:N], x[N:]
    hr, hi = h_ref[0], h_ref[1]
    y = jnp.concatenate([xr * hr - xi * hi, xr * hi + xi * hr], axis=0).astype(BF16)
    out = _dot(mi_ref[...], y).reshape(2, n, tc)
    o_ref[...] = (xg_ref[...].astype(F32) * (out + u.astype(F32) * bias_ref[...])).astype(o_ref.dtype)


def _fft_tables(n):
    N = 2 * n
    N1, S1 = FFT_N1, FFT_N1 // 2
    N2 = N // N1
    f1 = jnp.arange(N1, dtype=I32)
    c, s = _phase(f1[:, None] * jnp.arange(S1, dtype=I32)[None, :], N1)
    m_a = _stack_complex(c, -s).astype(BF16)
    m_b = _stack_complex(c.T / N, s.T / N).astype(BF16)
    f2 = jnp.arange(N2, dtype=I32)
    c, s = _phase(f2[None, None, :] * (f1[:, None, None] + N1 * f2[None, :, None]), N)
    g_f = _stack_complex(c, -s).astype(BF16)
    g_i = _stack_complex(jnp.swapaxes(c, 1, 2), jnp.swapaxes(s, 1, 2)).astype(BF16)
    return dict(N1=N1, S1=S1, N2=N2, m_a=m_a, m_b=m_b, g_f=g_f, g_i=g_i)


def long_conv_gate(u, xg, spec, bias, tabs):
    B, n, C = u.shape
    assert B == 2
    N = 2 * n
    bias3 = bias.astype(F32).reshape(1, 1, C)
    if tabs is None:
        hr, hi = jnp.real(spec).astype(F32), jnp.imag(spec).astype(F32)
        f, t = jnp.arange(N, dtype=I32)[:, None], jnp.arange(n, dtype=I32)[None, :]
        c, s = _phase(f * t, N)
        mf = _stack_complex(c, -s).astype(BF16)
        mi = _stack_complex(c.T / N, s.T / N).astype(BF16)
        tc = 256
        blk = pl.BlockSpec((2, n, tc), lambda j: (0, 0, j))
        return pl.pallas_call(
            _dense_conv_kernel,
            grid=(C // tc,),
            in_specs=[blk, pl.BlockSpec((2 * N, 2 * n), lambda j: (0, 0)),
                      pl.BlockSpec((2 * n, 2 * N), lambda j: (0, 0)),
                      pl.BlockSpec((2, N, tc), lambda j: (0, 0, j)), blk,
                      pl.BlockSpec((1, 1, tc), lambda j: (0, 0, j))],
            out_specs=blk,
            out_shape=jax.ShapeDtypeStruct((2, n, C), BF16),
            compiler_params=_cparams("parallel"),
            name="hy_dense_conv",
        )(u, mf, mi, jnp.stack([hr, hi]), xg, bias3)

    N1, S1, N2 = tabs["N1"], tabs["S1"], tabs["N2"]
    m_a, m_b, g_f, g_i, hperm = tabs["m_a"], tabs["m_b"], tabs["g_f"], tabs["g_i"], spec
    cols = N2 * C
    tc = 2 * C
    zb = pl.BlockSpec((2, S1, tc), lambda j: (0, 0, j))
    a = pl.pallas_call(
        _fft_a_kernel,
        grid=(cols // tc,),
        in_specs=[zb, pl.BlockSpec((2 * N1, 2 * S1), lambda j: (0, 0))],
        out_specs=pl.BlockSpec((2 * N1, tc), lambda j: (0, j)),
        out_shape=jax.ShapeDtypeStruct((2 * N1, cols), BF16),
        compiler_params=_cparams("parallel"),
        name="hy_fft_a",
    )(u.reshape(2, S1, cols), m_a)
    ab = pl.BlockSpec((2, 1, N2, C), lambda i: (0, i, 0, 0))
    gb = pl.BlockSpec((1, 2 * N2, 2 * N2), lambda i: (i, 0, 0))
    bm = pl.pallas_call(
        _fft_c_kernel,
        grid=(N1,),
        in_specs=[ab, gb, gb, pl.BlockSpec((1, 2, N2, C), lambda i: (i, 0, 0, 0))],
        out_specs=ab,
        out_shape=jax.ShapeDtypeStruct((2, N1, N2, C), BF16),
        compiler_params=_cparams("parallel"),
        name="hy_fft_c",
    )(a.reshape(2, N1, N2, C), g_f, g_i, hperm)
    out = pl.pallas_call(
        _fft_b_kernel,
        grid=(cols // tc,),
        in_specs=[pl.BlockSpec((2 * N1, tc), lambda j: (0, j)),
                  pl.BlockSpec((2 * S1, 2 * N1), lambda j: (0, 0)), zb, zb,
                  pl.BlockSpec((1, 1, tc), lambda j: (0, 0, 0))],
        out_specs=zb,
        out_shape=jax.ShapeDtypeStruct((2, S1, cols), BF16),
        compiler_params=_cparams("parallel"),
        name="hy_fft_b",
    )(bm.reshape(2 * N1, cols), m_b, u.reshape(2, S1, cols), xg.reshape(2, S1, cols),
      jnp.tile(bias3, (1, 1, tc // C)))
    return out.reshape(2, n, C)


def hyena_layer(x, ctx, mv, norm_g, hy_p, need_ctx):
    (w_in, b_in, conv_w, conv_b, w1, b1, w2, b2, freq, w3, fbias, w_out, b_out) = hy_p
    w_in_bf = w_in.astype(BF16)
    w_out_bf = w_out.astype(BF16)

    def mix(h_in, sh, sc, res, gate):
        n = h_in.shape[1]
        u = mod_proj(h_in, norm_g, sh, sc, w_in_bf, b_in.astype(F32), out_dtype=F32, name="hy_project")
        v, x1, x2 = short_conv_split(u, conv_w, conv_b)
        taps = _hyena_filter_taps(n, w1, b1, w2, b2, freq, w3)
        if n <= 512:
            tabs = None
            hfull = _filter_spectrum_small(taps)
            spec = [hfull[:, o] for o in range(HY_ORDER)]
        else:
            tabs = _fft_tables(n)
            spec = _filter_spectrum_fft(taps, tabs)
        z = long_conv_gate(v, x1, spec[0], fbias[0], tabs)
        z = long_conv_gate(z, x2, spec[1], fbias[1], tabs)
        return res_matmul(z, w_out_bf, b_out.astype(F32), res, gate)

    x = mix(x, mv["sh_l"], mv["sc_l"], x, mv["g_l"])
    if need_ctx:
        ctx = mix(ctx, mv["sh_c"], mv["sc_c"], ctx, mv["g_c"])
    return x, ctx


def kernel(x, c, ctx, c_ctx, norm_mix_g, norm_ffn_g, mod_w, mod_b,
           da_w_in, da_q_norm, da_k_norm, da_lambda, da_sub_norm, da_w_out,
           ret_w_in, ret_log_alpha, ret_group_norm, ret_w_out,
           hy_w_in, hy_b_in, hy_conv_w, hy_conv_b, hy_ffn_w1, hy_ffn_b1, hy_ffn_w2, hy_ffn_b2,
           hy_sin_freq, hy_ffn_w3, hy_filter_bias, hy_w_out, hy_b_out,
           router_w, exp_w1, exp_w3, exp_w2):
    B = x.shape[0]
    depth = mod_w.shape[0]
    mods = _mod_all(c, c_ctx, mod_w, mod_b, B)
    for i in range(depth):
        last = i == depth - 1
        m, s = i % N_MIXERS, i // N_MIXERS
        mv = mods[i]
        if m == 0:
            lambda_init = 0.8 - 0.6 * math.exp(-0.3 * i)
            x, ctx = diff_attention_layer(x, ctx, mv, norm_mix_g[i], da_w_in[s], da_q_norm[s], da_k_norm[s],
                                          da_lambda[s], da_sub_norm[s], da_w_out[s], lambda_init, not last)
        elif m == 1:
            x, ctx = retention_layer(x, ctx, mv, norm_mix_g[i], ret_w_in[s], ret_log_alpha[s],
                                     ret_group_norm[s], ret_w_out[s], not last)
        else:
            hy_p = (hy_w_in[s], hy_b_in[s], hy_conv_w[s], hy_conv_b[s], hy_ffn_w1[s], hy_ffn_b1[s],
                    hy_ffn_w2[s], hy_ffn_b2[s], hy_sin_freq[s], hy_ffn_w3[s], hy_filter_bias[s],
                    hy_w_out[s], hy_b_out[s])
            x, ctx = hyena_layer(x, ctx, mv, norm_mix_g[i], hy_p, not last)
        w1, w3, w2 = exp_w1[i].astype(BF16), exp_w3[i].astype(BF16), exp_w2[i].astype(BF16)
        x = moe_layer(x, norm_ffn_g[i], mv["sh2_l"], mv["sc2_l"], mv["g2_l"], router_w[i], w1, w3, w2)
        if not last:
            ctx = moe_layer(ctx, norm_ffn_g[i], mv["sh2_c"], mv["sc2_c"], mv["g2_c"], router_w[i], w1, w3, w2)
    return x
```

```python
import functools
import math

import jax
import jax.numpy as jnp
import numpy as np
from jax import lax
from jax.experimental import pallas as pl
from jax.experimental.pallas import tpu as pltpu

F32 = jnp.float32
BF16 = jnp.bfloat16
I32 = jnp.int32

EPS = 1e-6
N_MOD = 6
ROPE_BASE = 10000.0
GRID_W = 64
N_MIXERS = 3

DA_HEAD_DIM = 64
DA_V_DIM = 128
RET_QK_DIM = 256
RET_V_DIM = 512
RET_CHUNK = 128

HY_ORDER = 2
HY_EMB = 33
HY_BANDS = (HY_EMB - 1) // 2
HY_TARGET = 1e-2
HY_FAST_DECAY_PCT = 0.3
HY_SLOW_DECAY_PCT = 1.5

N_EXPERTS = 16
EC_CAPACITY_FACTOR = 2

LANES = 128
VMEM_LIMIT_BYTES = 56 * 1024 * 1024


def _cparams(*sem):
    return pltpu.CompilerParams(dimension_semantics=sem, vmem_limit_bytes=VMEM_LIMIT_BYTES)


def _split_bf16(a):
    hi = a.astype(BF16)
    lo = (a - hi.astype(F32)).astype(BF16)
    return hi, lo


def _dot(a, b):
    return jnp.dot(a, b, preferred_element_type=F32)


def _dot3(a_hi, a_lo, b_hi, b_lo):
    return _dot(a_hi, b_hi) + _dot(a_lo, b_hi) + _dot(a_hi, b_lo)


def _modulate(x, g, sh, sc):
    ms = jnp.mean(x * x, axis=-1, keepdims=True)
    y = x * lax.rsqrt(ms + EPS) * g
    return y * (1.0 + sc) + sh


def _modvec_kernel(a_ref, w_ref, b_ref, o_ref):
    a = a_ref[...]
    a_hi, a_lo = _split_bf16(a * jax.nn.sigmoid(a))
    w_hi, w_lo = _split_bf16(w_ref[0])
    o_ref[0] = _dot3(a_hi, a_lo, w_hi, w_lo) + b_ref[0]


def mod_vectors(a, mod_w, mod_b):
    depth, d, n = mod_w.shape
    tn = 1024
    return pl.pallas_call(
        _modvec_kernel,
        grid=(depth, n // tn),
        in_specs=[pl.BlockSpec((8, d), lambda i, j: (0, 0)),
                  pl.BlockSpec((1, d, tn), lambda i, j: (i, 0, j)),
                  pl.BlockSpec((1, 1, tn), lambda i, j: (i, 0, j))],
        out_specs=pl.BlockSpec((1, 8, tn), lambda i, j: (i, 0, j)),
        out_shape=jax.ShapeDtypeStruct((depth, 8, n), F32),
        compiler_params=_cparams("parallel", "parallel"),
        name="mod_vectors",
    )(a, mod_w, mod_b.reshape(depth, 1, n))


def _mod_all(c, c_ctx, mod_w, mod_b, batch):
    d = c.shape[-1]
    a = jnp.zeros((8, d), F32).at[:batch].set(c).at[batch].set(c_ctx)
    mv = mod_vectors(a, mod_w, mod_b)
    names = ("sh", "sc", "g", "sh2", "sc2", "g2")
    layers = []
    for i in range(mod_w.shape[0]):
        parts = mv[i].reshape(8, N_MOD, d)
        layer = {}
        for k, nm in enumerate(names):
            layer[nm + "_l"] = parts[:batch, k][:, None, :]
            layer[nm + "_c"] = jnp.broadcast_to(parts[batch, k][None, None, :], (batch, 1, d))
        layers.append(layer)
    return layers


def _proj_kernel(x_ref, g_ref, sh_ref, sc_ref, w_ref, b_ref, *rest, epilogue, n_extra):
    extra, o_ref, h_scr = rest[:n_extra], rest[n_extra], rest[n_extra + 1]
    j = pl.program_id(2)

    @pl.when(j == 0)
    def _():
        h_scr[...] = _modulate(x_ref[0], g_ref[...], sh_ref[0], sc_ref[0]).astype(BF16)

    acc = _dot(h_scr[...], w_ref[...]) + b_ref[...]
    epilogue(acc, j, o_ref, *extra)


def _plain_epilogue(acc, j, o_ref):
    o_ref[0] = acc.astype(o_ref.dtype)


def mod_proj(x, g, sh, sc, w, bias, *, epilogue=_plain_epilogue, extra=(), extra_specs=(),
             out_dtype=BF16, tn=512, name="mod_proj"):
    B, n, K = x.shape
    N = w.shape[1]
    tm = min(512, n)
    tn = min(tn, N)
    kern = functools.partial(_proj_kernel, epilogue=epilogue, n_extra=len(extra))
    return pl.pallas_call(
        kern,
        grid=(B, n // tm, N // tn),
        in_specs=[pl.BlockSpec((1, tm, K), lambda b, i, j: (b, i, 0)),
                  pl.BlockSpec((1, K), lambda b, i, j: (0, 0)),
                  pl.BlockSpec((1, 1, K), lambda b, i, j: (b, 0, 0)),
                  pl.BlockSpec((1, 1, K), lambda b, i, j: (b, 0, 0)),
                  pl.BlockSpec((K, tn), lambda b, i, j: (0, j)),
                  pl.BlockSpec((1, tn), lambda b, i, j: (0, j)),
                  *extra_specs],
        out_specs=pl.BlockSpec((1, tm, tn), lambda b, i, j: (b, i, j)),
        out_shape=jax.ShapeDtypeStruct((B, n, N), out_dtype),
        scratch_shapes=[pltpu.VMEM((tm, K), BF16)],
        compiler_params=_cparams("parallel", "parallel", "arbitrary"),
        name=name,
    )(x, g.reshape(1, K), sh, sc, w, bias.reshape(1, N), *extra)


def _resmm_kernel(a_ref, w_ref, b_ref, r_ref, gt_ref, o_ref):
    acc = _dot(a_ref[0], w_ref[...]) + b_ref[...]
    o_ref[0] = r_ref[0] + gt_ref[0] * acc


def res_matmul(a, w, bias, res, gate):
    B, n, K = a.shape
    N = w.shape[1]
    tm = min(512, n)
    tn = min(512, N)
    return pl.pallas_call(
        _resmm_kernel,
        grid=(B, n // tm, N // tn),
        in_specs=[pl.BlockSpec((1, tm, K), lambda b, i, j: (b, i, 0)),
                  pl.BlockSpec((K, tn), lambda b, i, j: (0, j)),
                  pl.BlockSpec((1, tn), lambda b, i, j: (0, j)),
                  pl.BlockSpec((1, tm, tn), lambda b, i, j: (b, i, j)),
                  pl.BlockSpec((1, 1, tn), lambda b, i, j: (b, 0, j))],
        out_specs=pl.BlockSpec((1, tm, tn), lambda b, i, j: (b, i, j)),
        out_shape=jax.ShapeDtypeStruct((B, n, N), F32),
        compiler_params=_cparams("parallel", "parallel", "parallel"),
        name="res_matmul",
    )(a, w, bias.reshape(1, N), res, gate)


def _da_epilogue(acc, j, o_ref, cos_ref, sin_ref, qg_ref, kg_ref, gm_ref, *, tn, n_qk_tiles):
    @pl.when(j >= n_qk_tiles)
    def _():
        o_ref[0] = acc.astype(o_ref.dtype)

    @pl.when(j < n_qk_tiles)
    def _():
        gain = jnp.where(j < n_qk_tiles // 2, qg_ref[...], kg_ref[...])
        cos, sin = cos_ref[...], sin_ref[...]
        lane = lax.broadcasted_iota(I32, (1, LANES), 1)
        first_half = (lane % 32) < 16
        for s in range(tn // LANES):
            y = acc[:, s * LANES:(s + 1) * LANES]
            hi, lo = _split_bf16(y * y)
            gsum = _dot(hi, gm_ref[...]) + _dot(lo, gm_ref[...])
            yn = y * lax.rsqrt(gsum * (1.0 / DA_HEAD_DIM) + EPS) * gain
            partner = jnp.where(first_half, pltpu.roll(yn, LANES - 16, 1), pltpu.roll(yn, 16, 1))
            o_ref[0, :, s * LANES:(s + 1) * LANES] = (yn * cos + partner * sin).astype(o_ref.dtype)


def _attn_kernel(lam_ref, q_ref, kc_ref, vc_ref, *rest, has_lat, tk, n_chunks, out_scale):
    if has_lat:
        kl_ref, vl_ref, sg_ref, o_ref = rest
    else:
        sg_ref, o_ref = rest
    q = q_ref[0]
    tq = q.shape[0]
    lane = lax.broadcasted_iota(I32, (1, LANES), 1)
    zero = jnp.zeros_like(q)
    q2 = jnp.concatenate([jnp.where(lane < DA_HEAD_DIM, q, zero),
                          jnp.where(lane >= DA_HEAD_DIM, q, zero)], axis=0)

    def scores(k):
        return lax.dot_general(q2, k, (((1,), (1,)), ((), ())), preferred_element_type=F32)

    def absorb(s, v, carry):
        m, l, acc = carry
        m_new = jnp.maximum(m, jnp.max(s, axis=-1, keepdims=True))
        alpha = jnp.exp2(m - m_new)
        p = jnp.exp2(s - m_new)
        l_new = alpha * l + jnp.sum(p, axis=-1, keepdims=True)
        acc_new = alpha * acc + _dot(p.astype(BF16), v)
        return m_new, l_new, acc_new

    carry = (jnp.full((2 * tq, 1), -1e30, F32), jnp.zeros((2 * tq, 1), F32),
             jnp.zeros((2 * tq, LANES), F32))
    carry = absorb(scores(kc_ref[0]), vc_ref[0], carry)
    if has_lat:
        def body(c, carry):
            off = pl.multiple_of(c * tk, tk)
            return absorb(scores(kl_ref[0, pl.ds(off, tk), :]), vl_ref[0, pl.ds(off, tk), :], carry)
        carry = lax.fori_loop(0, n_chunks, body, carry, unroll=8 if n_chunks % 8 == 0 else 1)
    _, l, acc = carry
    o = acc / l
    o = o[:tq] - lam_ref[0] * o[tq:]
    ms = jnp.mean(o * o, axis=-1, keepdims=True)
    o_ref[0] = (o * lax.rsqrt(ms + EPS) * sg_ref[...] * out_scale).astype(o_ref.dtype)


def diff_attention_core(lam, qkv_q, qkv_c, qkv_l, sub_g, out_scale):
    B, nq, D3 = qkv_q.shape
    D = D3 // 3
    H = D // DA_V_DIM
    nc = qkv_c.shape[1]
    tq = min(512, nq)
    has_lat = qkv_l is not None
    in_specs = [pl.BlockSpec(memory_space=pltpu.SMEM),
                pl.BlockSpec((1, tq, LANES), lambda b, h, i: (b, i, h)),
                pl.BlockSpec((1, nc, LANES), lambda b, h, i: (b, 0, H + h)),
                pl.BlockSpec((1, nc, LANES), lambda b, h, i: (b, 0, 2 * H + h))]
    args = [lam, qkv_q, qkv_c, qkv_c]
    tk, n_chunks = 0, 0
    if has_lat:
        nl = qkv_l.shape[1]
        tk = min(1024, nl)
        n_chunks = nl // tk
        in_specs += [pl.BlockSpec((1, nl, LANES), lambda b, h, i: (b, 0, H + h)),
                     pl.BlockSpec((1, nl, LANES), lambda b, h, i: (b, 0, 2 * H + h))]
        args += [qkv_l, qkv_l]
    in_specs.append(pl.BlockSpec((1, LANES), lambda b, h, i: (0, 0)))
    args.append(sub_g.reshape(1, LANES))
    kern = functools.partial(_attn_kernel, has_lat=has_lat, tk=tk, n_chunks=n_chunks,
                             out_scale=out_scale)
    return pl.pallas_call(
        kern,
        grid=(B, H, nq // tq),
        in_specs=in_specs,
        out_specs=pl.BlockSpec((1, tq, LANES), lambda b, h, i: (b, i, h)),
        out_shape=jax.ShapeDtypeStruct((B, nq, D), BF16),
        compiler_params=_cparams("parallel", "parallel", "arbitrary"),
        name="diff_attention",
    )(*args)


def _axial_rope_tables(n):
    f = DA_HEAD_DIM // 4
    inv = 1.0 / (ROPE_BASE ** (jnp.arange(f, dtype=F32) / f))
    rows = n // GRID_W
    row = jnp.repeat(jnp.arange(rows), GRID_W).astype(F32)
    col = jnp.tile(jnp.arange(GRID_W), rows).astype(F32)
    ang = jnp.stack([row[:, None] * inv, col[:, None] * inv], axis=1)
    cos, sin = jnp.cos(ang), jnp.sin(ang)
    cos64 = jnp.concatenate([cos, cos], axis=-1).reshape(n, DA_HEAD_DIM)
    sin64 = jnp.concatenate([-sin, sin], axis=-1).reshape(n, DA_HEAD_DIM)
    return jnp.tile(cos64, (1, 2)), jnp.tile(sin64, (1, 2))


def da_project(h_in, g, sh, sc, w_in_bf, q_g, k_g, cos, sin):
    B, n, D = h_in.shape
    tn = 512
    tm = min(512, n)
    gm = (np.arange(LANES)[:, None] // DA_HEAD_DIM == np.arange(LANES)[None, :] // DA_HEAD_DIM)
    gm = jnp.asarray(gm, BF16)
    qg = (jnp.tile(q_g.astype(F32), 2) * (DA_HEAD_DIM ** -0.5 * math.log2(math.e))).reshape(1, LANES)
    kg = jnp.tile(k_g.astype(F32), 2).reshape(1, LANES)
    extra = (cos, sin, qg, kg, gm)
    extra_specs = (pl.BlockSpec((tm, LANES), lambda b, i, j: (i, 0)),
                   pl.BlockSpec((tm, LANES), lambda b, i, j: (i, 0)),
                   pl.BlockSpec((1, LANES), lambda b, i, j: (0, 0)),
                   pl.BlockSpec((1, LANES), lambda b, i, j: (0, 0)),
                   pl.BlockSpec((LANES, LANES), lambda b, i, j: (0, 0)))
    epi = functools.partial(_da_epilogue, tn=tn, n_qk_tiles=2 * D // tn)
    return mod_proj(h_in, g, sh, sc, w_in_bf, jnp.zeros((3 * D,), F32), epilogue=epi, extra=extra,
                    extra_specs=extra_specs, tn=tn, name="da_project")


def diff_attention_layer(x, ctx, mv, norm_g, w_in, q_g, k_g, lam, sub_g, w_out, lambda_init, need_ctx):
    B, L, D = x.shape
    nc = ctx.shape[1]
    w_in_bf = w_in.astype(BF16)
    w_out_bf = w_out.astype(BF16)
    cos_l, sin_l = _axial_rope_tables(L)
    cos_c, sin_c = jnp.ones((nc, LANES), F32), jnp.zeros((nc, LANES), F32)
    qkv_l = da_project(x, norm_g, mv["sh_l"], mv["sc_l"], w_in_bf, q_g, k_g, cos_l, sin_l)
    qkv_c = da_project(ctx, norm_g, mv["sh_c"], mv["sc_c"], w_in_bf, q_g, k_g, cos_c, sin_c)
    lamf = lam.astype(F32)
    lam_val = (jnp.exp(jnp.sum(lamf[0] * lamf[1])) - jnp.exp(jnp.sum(lamf[2] * lamf[3]))
               + lambda_init).reshape(1)
    out_scale = 1.0 - lambda_init
    o_l = diff_attention_core(lam_val, qkv_l, qkv_c, qkv_l, sub_g, out_scale)
    zb = jnp.zeros((D,), F32)
    x = res_matmul(o_l, w_out_bf, zb, x, mv["g_l"])
    if need_ctx:
        o_c = diff_attention_core(lam_val, qkv_c, qkv_c, None, sub_g, out_scale)
        ctx = res_matmul(o_c, w_out_bf, zb, ctx, mv["g_c"])
    return x, ctx


def _router_kernel(x_ref, g_ref, sh_ref, sc_ref, whi_ref, wlo_ref, o_ref):
    h = _modulate(x_ref[0], g_ref[...], sh_ref[0], sc_ref[0])
    o_ref[0] = _router_affinity(h, whi_ref[...], wlo_ref[...])


def _router_affinity(h, w_hi, w_lo):
    h_hi, h_lo = _split_bf16(h)
    logits = _dot3(h_hi, h_lo, w_hi, w_lo)
    lane = lax.broadcasted_iota(I32, (1, LANES), 1)
    logits = jnp.where(lane < N_EXPERTS, logits, -1e30)
    z = jnp.exp(logits - jnp.max(logits, axis=-1, keepdims=True))
    return z / jnp.sum(z, axis=-1, keepdims=True)


def router_affinity(x, g, sh, sc, w_hi, w_lo):
    B, n, D = x.shape
    tm = min(512, n)
    return pl.pallas_call(
        _router_kernel,
        grid=(B, n // tm),
        in_specs=[pl.BlockSpec((1, tm, D), lambda b, i: (b, i, 0)),
                  pl.BlockSpec((1, D), lambda b, i: (0, 0)),
                  pl.BlockSpec((1, 1, D), lambda b, i: (b, 0, 0)),
                  pl.BlockSpec((1, 1, D), lambda b, i: (b, 0, 0)),
                  pl.BlockSpec((D, LANES), lambda b, i: (0, 0)),
                  pl.BlockSpec((D, LANES), lambda b, i: (0, 0))],
        out_specs=pl.BlockSpec((1, tm, LANES), lambda b, i: (b, i, 0)),
        out_shape=jax.ShapeDtypeStruct((B, n, LANES), F32),
        compiler_params=_cparams("parallel", "parallel"),
        name="router",
    )(x, g.reshape(1, D), sh, sc, w_hi, w_lo)


def _ffn_kernel(idx_ref, x_hbm, acc_in, g_ref, sh_ref, sc_ref, g2_ref, rhi_ref, rlo_ref,
                w1_ref, w3_ref, w2_ref, o_hbm, xbuf, abuf, obuf, sem_x, sem_a, sem_o, *, n, cap, tm):
    del acc_in
    b, e = pl.program_id(0), pl.program_id(1)
    base = (b * pl.num_programs(1) + e) * cap
    row0 = b * n
    n_tiles = cap // tm

    def start_gather(j, slot):
        def body(k, c):
            row = row0 + idx_ref[base + j * tm + k]
            pltpu.make_async_copy(x_hbm.at[pl.ds(row, 1)], xbuf.at[slot, pl.ds(k, 1)], sem_x.at[slot]).start()
            pltpu.make_async_copy(o_hbm.at[pl.ds(row, 1)], abuf.at[slot, pl.ds(k, 1)], sem_a.at[slot]).start()
            return c
        lax.fori_loop(0, tm, body, 0)

    def wait_gather(slot):
        pltpu.make_async_copy(x_hbm.at[pl.ds(0, tm)], xbuf.at[slot], sem_x.at[slot]).wait()
        pltpu.make_async_copy(o_hbm.at[pl.ds(0, tm)], abuf.at[slot], sem_a.at[slot]).wait()

    def start_scatter(j, slot):
        def body(k, c):
            row = row0 + idx_ref[base + j * tm + k]
            pltpu.make_async_copy(obuf.at[slot, pl.ds(k, 1)], o_hbm.at[pl.ds(row, 1)], sem_o.at[slot]).start()
            return c
        lax.fori_loop(0, tm, body, 0)

    def wait_scatter(slot):
        pltpu.make_async_copy(obuf.at[slot], o_hbm.at[pl.ds(0, tm)], sem_o.at[slot]).wait()

    lane = lax.broadcasted_iota(I32, (1, LANES), 1)
    start_gather(0, 0)
    for j in range(n_tiles):
        slot = j % 2
        if j + 1 < n_tiles:
            start_gather(j + 1, 1 - slot)
        wait_gather(slot)
        h = _modulate(xbuf[slot], g_ref[...], sh_ref[0], sc_ref[0])
        aff = _router_affinity(h, rhi_ref[...], rlo_ref[...])
        gate = jnp.sum(jnp.where(lane == e, aff, 0.0), axis=-1, keepdims=True)
        hb = h.astype(BF16)
        a1 = _dot(hb, w1_ref[0])
        a3 = _dot(hb, w3_ref[0])
        hid = (a1 * jax.nn.sigmoid(a1) * a3).astype(BF16)
        y = _dot(hid, w2_ref[0])
        if j >= 2:
            wait_scatter(slot)
        obuf[slot] = abuf[slot] + g2_ref[0] * (gate * y)
        start_scatter(j, slot)
    for j in range(max(n_tiles - 2, 0), n_tiles):
        wait_scatter(j % 2)


def expert_ffn(idx, x, g, sh, sc, g2, r_hi, r_lo, w1, w3, w2):
    B, n, D = x.shape
    E, cap = idx.shape[1], idx.shape[2]
    F = w1.shape[-1]
    tm = min(256, cap)
    x2 = x.reshape(B * n, D)
    kern = functools.partial(_ffn_kernel, n=n, cap=cap, tm=tm)
    vec = lambda: pl.BlockSpec((1, 1, D), lambda b, e, idx: (b, 0, 0))
    buf = lambda: pltpu.VMEM((2, tm, D), F32)
    grid_spec = pltpu.PrefetchScalarGridSpec(
        num_scalar_prefetch=1,
        grid=(B, E),
        in_specs=[pl.BlockSpec(memory_space=pl.ANY),
                  pl.BlockSpec(memory_space=pl.ANY),
                  pl.BlockSpec((1, D), lambda b, e, idx: (0, 0)),
                  vec(), vec(), vec(),
                  pl.BlockSpec((D, LANES), lambda b, e, idx: (0, 0)),
                  pl.BlockSpec((D, LANES), lambda b, e, idx: (0, 0)),
                  pl.BlockSpec((1, D, F), lambda b, e, idx: (e, 0, 0)),
                  pl.BlockSpec((1, D, F), lambda b, e, idx: (e, 0, 0)),
                  pl.BlockSpec((1, F, D), lambda b, e, idx: (e, 0, 0))],
        out_specs=pl.BlockSpec(memory_space=pl.ANY),
        scratch_shapes=[buf(), buf(), buf(), pltpu.SemaphoreType.DMA((2,)),
                        pltpu.SemaphoreType.DMA((2,)), pltpu.SemaphoreType.DMA((2,))])
    out = pl.pallas_call(
        kern,
        grid_spec=grid_spec,
        out_shape=jax.ShapeDtypeStruct((B * n, D), F32),
        input_output_aliases={2: 0},
        compiler_params=_cparams("arbitrary", "arbitrary"),
        name="expert_ffn",
    )(idx.reshape(-1), x2, x2, g.reshape(1, D), sh, sc, g2, r_hi, r_lo, w1, w3, w2)
    return out.reshape(B, n, D)


TOPK_ROWS = 128


def _topk_kernel(a_ref, idx_ref, *, k, capp):
    a = a_ref[0]
    n_exp = a.shape[0]
    bits = lax.bitcast_convert_type(a, I32)
    row_i = lax.broadcasted_iota(I32, (1, TOPK_ROWS, LANES), 1)
    lane_i = lax.broadcasted_iota(I32, (1, TOPK_ROWS, LANES), 2)
    tok = row_i * LANES + lane_i

    def count(ind):
        return jnp.sum(jnp.sum(ind, axis=2, keepdims=True), axis=1, keepdims=True)

    thr = jnp.zeros((n_exp, 1, 1), I32)
    for bit in range(30, -1, -1):
        cand = thr | (1 << bit)
        thr = jnp.where(count(jnp.where(bits >= cand, 1, 0)) >= k, cand, thr)
    gt = jnp.where(bits > thr, 1, 0)
    eq = jnp.where(bits == thr, 1, 0)
    need = k - count(gt)
    tcut = jnp.zeros((n_exp, 1, 1), I32)
    for bit in range((TOPK_ROWS * LANES).bit_length() - 2, -1, -1):
        cand = tcut | (1 << bit)
        tcut = jnp.where(count(jnp.where(tok < cand, eq, 0)) < need, cand, tcut)
    sel = (gt + jnp.where(tok <= tcut, eq, 0)).astype(F32)

    tri = jnp.where(lax.broadcasted_iota(I32, (TOPK_ROWS, TOPK_ROWS), 0)
                    >= lax.broadcasted_iota(I32, (TOPK_ROWS, TOPK_ROWS), 1), 1.0, 0.0).astype(BF16)
    slot = lax.broadcasted_iota(I32, (1, capp), 1).astype(F32)
    rows_col = lax.broadcasted_iota(I32, (TOPK_ROWS, 1), 0).astype(F32)
    for e in range(n_exp):
        m = sel[e]
        cnt = jnp.sum(m, axis=1, keepdims=True)
        incl = _dot(tri, jnp.broadcast_to(cnt, (TOPK_ROWS, LANES)).astype(BF16))[:, :1]
        before = incl <= slot
        row_of = jnp.sum(jnp.where(before, 1.0, 0.0), axis=0, keepdims=True)
        base = jnp.sum(jnp.where(before, cnt, 0.0), axis=0, keepdims=True)
        onehot = jnp.where(rows_col == row_of, 1.0, 0.0).astype(BF16)
        picked = _dot(m.T.astype(BF16), onehot)
        cum = _dot(tri, picked.astype(BF16))
        local = jnp.sum(jnp.where(cum <= slot - base, 1.0, 0.0), axis=0, keepdims=True)
        idx_ref[0, e] = (row_of * LANES + local).astype(I32)


def expert_choice_topk(aff, cap):
    B, E, n = aff.shape
    n_pad = TOPK_ROWS * LANES
    assert n <= n_pad and TOPK_ROWS == LANES
    a = jnp.pad(aff, ((0, 0), (0, 0), (0, n_pad - n)), constant_values=-1.0)
    a = a.reshape(B, E, TOPK_ROWS, LANES)
    capp = max(cap, LANES)
    out = pl.pallas_call(
        functools.partial(_topk_kernel, k=cap, capp=capp),
        grid=(B,),
        in_specs=[pl.BlockSpec((1, E, TOPK_ROWS, LANES), lambda b: (b, 0, 0, 0))],
        out_specs=pl.BlockSpec((1, E, 1, capp), lambda b: (b, 0, 0, 0)),
        out_shape=jax.ShapeDtypeStruct((B, E, 1, capp), I32),
        compiler_params=_cparams("parallel"),
        name="expert_topk",
    )(a)
    return out[:, :, 0, :cap]


def moe_layer(x, g, sh, sc, g2, router_w, w1, w3, w2):
    B, n, D = x.shape
    E = router_w.shape[1]
    cap = EC_CAPACITY_FACTOR * n // E
    rw = jnp.pad(router_w.astype(F32), ((0, 0), (0, LANES - E)))
    r_hi, r_lo = _split_bf16(rw)
    aff = router_affinity(x, g, sh, sc, r_hi, r_lo)
    idx = expert_choice_topk(aff[:, :, :E].transpose(0, 2, 1), cap)
    return expert_ffn(idx, x, g, sh, sc, g2, r_hi, r_lo, w1, w3, w2)


def _ret_epilogue(acc, j, o_ref, cos_ref, sin_ref, *, tn, n_q_tiles):
    @pl.when(j >= 2 * n_q_tiles)
    def _():
        o_ref[0] = acc.astype(o_ref.dtype)

    @pl.when(j < 2 * n_q_tiles)
    def _():
        scale = jnp.where(j < n_q_tiles, 1.0, RET_QK_DIM ** -0.5)
        cos, sin = cos_ref[...] * scale, sin_ref[...] * scale
        f = RET_QK_DIM // 2
        for s in range(tn // RET_QK_DIM):
            lo, mid, hi = s * RET_QK_DIM, s * RET_QK_DIM + f, (s + 1) * RET_QK_DIM
            x1, x2 = acc[:, lo:mid], acc[:, mid:hi]
            o_ref[0, :, lo:mid] = (x1 * cos - x2 * sin).astype(o_ref.dtype)
            o_ref[0, :, mid:hi] = (x2 * cos + x1 * sin).astype(o_ref.dtype)


def _retention_kernel(lg_ref, qf_ref, kf_ref, vf_ref, qb_ref, kb_ref, vb_ref, s0f_ref, s0b_ref,
                      of_ref, ob_ref, sf_ref, sb_ref, stf, stb, *, T):
    h, c = pl.program_id(1), pl.program_id(2)

    @pl.when(c == 0)
    def _():
        stf[...] = s0f_ref[0, 0]
        stb[...] = s0b_ref[0, 0]

    ii = lax.broadcasted_iota(I32, (T, T), 0).astype(F32)
    jj = lax.broadcasted_iota(I32, (T, T), 1).astype(F32)
    row = lax.broadcasted_iota(I32, (T, 1), 0).astype(F32)

    def one(q_ref, k_ref, v_ref, st, o_ref, lg, forward):
        q, k, v = q_ref[0], k_ref[0], v_ref[0]
        d = ii - jj if forward else jj - ii
        mask = d >= 0 if forward else d > 0
        dmask = jnp.where(mask, jnp.exp(jnp.where(mask, d, 0.0) * lg), 0.0)
        xi = jnp.exp(((row + 1.0) if forward else (T - row)) * lg)
        zeta = jnp.exp(((T - 1.0 - row) if forward else row) * lg)
        inner = lax.dot_general(q, k, (((1,), (1,)), ((), ())), preferred_element_type=F32) * dmask
        o = _dot(inner.astype(BF16), v) + _dot(q, st[...].astype(BF16)) * xi
        o_ref[0] = o.astype(o_ref.dtype)
        kz = (k.astype(F32) * zeta).astype(BF16)
        decay = jnp.exp(jnp.full((1, 1), T, F32) * lg)
        st[...] = st[...] * decay + lax.dot_general(kz, v, (((0,), (0,)), ((), ())),
                                                    preferred_element_type=F32)

    one(qf_ref, kf_ref, vf_ref, stf, of_ref, lg_ref[0, h], True)
    one(qb_ref, kb_ref, vb_ref, stb, ob_ref, lg_ref[1, h], False)

    @pl.when(c == pl.num_programs(2) - 1)
    def _():
        sf_ref[0, 0] = stf[...]
        sb_ref[0, 0] = stb[...]


def retention_core(log_g, p, s0f, s0b):
    B, n, _ = p.shape
    H, dk, dv = s0f.shape[1], s0f.shape[2], s0f.shape[3]
    T = min(256, n)
    nc = n // T
    kq, kk, kv = 0, (H * dk) // dk, (2 * H * dk) // dv
    fwd = lambda off: (lambda b, h, c: (b, c, off + h))
    bwd = lambda off: (lambda b, h, c: (b, nc - 1 - c, off + h))
    st_spec = pl.BlockSpec((1, 1, dk, dv), lambda b, h, c: (b, h, 0, 0))
    out_sd = jax.ShapeDtypeStruct((B, n, H * dv), BF16)
    st_sd = jax.ShapeDtypeStruct((B, H, dk, dv), F32)
    return pl.pallas_call(
        functools.partial(_retention_kernel, T=T),
        grid=(B, H, nc),
        in_specs=[pl.BlockSpec(memory_space=pltpu.SMEM),
                  pl.BlockSpec((1, T, dk), fwd(kq)), pl.BlockSpec((1, T, dk), fwd(kk)),
                  pl.BlockSpec((1, T, dv), fwd(kv)),
                  pl.BlockSpec((1, T, dk), bwd(kq)), pl.BlockSpec((1, T, dk), bwd(kk)),
                  pl.BlockSpec((1, T, dv), bwd(kv)),
                  st_spec, st_spec],
        out_specs=[pl.BlockSpec((1, T, dv), fwd(0)), pl.BlockSpec((1, T, dv), bwd(0)), st_spec, st_spec],
        out_shape=[out_sd, out_sd, st_sd, st_sd],
        scratch_shapes=[pltpu.VMEM((dk, dv), F32), pltpu.VMEM((dk, dv), F32)],
        compiler_params=_cparams("parallel", "parallel", "arbitrary"),
        name="retention",
    )(log_g, p, p, p, p, p, p, s0f, s0b)


def _ret_finish_kernel(of_ref, ob_ref, g_ref, gn_ref, z_ref, *, dv):
    for s in range(of_ref.shape[-1] // dv):
        sl = slice(s * dv, (s + 1) * dv)
        y = of_ref[0, :, sl].astype(F32) + ob_ref[0, :, sl].astype(F32)
        ms = jnp.mean(y * y, axis=-1, keepdims=True)
        y = y * lax.rsqrt(ms + EPS) * gn_ref[...]
        g = g_ref[0, :, sl].astype(F32)
        z_ref[0, :, sl] = (g * jax.nn.sigmoid(g) * y).astype(z_ref.dtype)


def retention_finish(of, ob, p, gn_g):
    B, n, N = of.shape
    dv = gn_g.shape[0]
    tm = min(512, n)
    blk = lambda off: pl.BlockSpec((1, tm, N), lambda b, i: (b, i, off))
    return pl.pallas_call(
        functools.partial(_ret_finish_kernel, dv=dv),
        grid=(B, n // tm),
        in_specs=[blk(0), blk(0), blk(p.shape[-1] // N - 1), pl.BlockSpec((1, dv), lambda b, i: (0, 0))],
        out_specs=blk(0),
        out_shape=jax.ShapeDtypeStruct((B, n, N), BF16),
        compiler_params=_cparams("parallel", "parallel"),
        name="retention_finish",
    )(of, ob, p, gn_g.astype(F32).reshape(1, dv))


def retention_layer(x, ctx, mv, norm_g, w_in, log_alpha, gn_g, w_out, need_ctx):
    B, L, D = x.shape
    nc = ctx.shape[1]
    dk, dv = RET_QK_DIM, RET_V_DIM
    H = D // dk
    w_in_bf = w_in.astype(BF16)
    w_out_bf = w_out.astype(BF16)
    zb = jnp.zeros((w_in.shape[1],), F32)
    log_g = jnp.log1p(-jnp.exp(log_alpha.astype(F32)))
    f = dk // 2
    inv = 1.0 / (ROPE_BASE ** jnp.linspace(0.0, 1.0, f, dtype=F32))
    ang = jnp.arange(L, dtype=F32)[:, None] * inv
    tn = 512

    def project(h_in, sh, sc, cos, sin):
        tm = min(512, h_in.shape[1])
        specs = (pl.BlockSpec((tm, f), lambda b, i, j: (i, 0)), pl.BlockSpec((tm, f), lambda b, i, j: (i, 0)))
        epi = functools.partial(_ret_epilogue, tn=tn, n_q_tiles=H * dk // tn)
        return mod_proj(h_in, norm_g, sh, sc, w_in_bf, zb, epilogue=epi, extra=(cos, sin),
                        extra_specs=specs, tn=tn, name="ret_project")

    p_c = project(ctx, mv["sh_c"], mv["sc_c"], jnp.ones((nc, f), F32), jnp.zeros((nc, f), F32))
    p_l = project(x, mv["sh_l"], mv["sc_l"], jnp.cos(ang), jnp.sin(ang))
    zero = jnp.zeros((B, H, dk, dv), F32)
    of_c, ob_c, s_f, s_b = retention_core(log_g, p_c, zero, zero)
    of_l, ob_l, _, _ = retention_core(log_g, p_l, s_f, s_b)
    zd = jnp.zeros((D,), F32)
    x = res_matmul(retention_finish(of_l, ob_l, p_l, gn_g), w_out_bf, zd, x, mv["g_l"])
    if need_ctx:
        ctx = res_matmul(retention_finish(of_c, ob_c, p_c, gn_g), w_out_bf, zd, ctx, mv["g_c"])
    return x, ctx


def _hyena_filter_taps(n, w1, b1, w2, b2, freq, w3):
    t = jnp.linspace(0.0, 1.0, n, dtype=F32)[:, None]
    w = 2.0 * math.pi * jnp.arange(n, dtype=F32)[:, None] / n
    f = jnp.linspace(1e-4, HY_BANDS - 1, HY_BANDS, dtype=F32)[None]
    z = jnp.concatenate([t, jnp.cos(f * w), -jnp.sin(f * w)], axis=-1)
    hp = lax.Precision.HIGHEST
    a = jnp.sin(freq[0].astype(F32) * (jnp.dot(z, w1.astype(F32), precision=hp) + b1.astype(F32)))
    a = jnp.sin(freq[1].astype(F32) * (jnp.dot(a, w2.astype(F32), precision=hp) + b2.astype(F32)))
    C = w3.shape[-1] // (2 * HY_ORDER)
    h = jnp.dot(a, w3.astype(F32), precision=hp).reshape(n, HY_ORDER, 2, C)
    max_decay = math.log(HY_TARGET) / HY_FAST_DECAY_PCT
    min_decay = math.log(HY_TARGET) / HY_SLOW_DECAY_PCT
    deltas = jnp.abs(jnp.linspace(min_decay, max_decay, C, dtype=F32))
    return h * jnp.exp(-t * deltas)[:, None, None, :]


def _filter_spectrum_small(h):
    n, _, _, C = h.shape
    buf = jnp.concatenate([h[:, :, 0], jnp.zeros((1, HY_ORDER, C), F32),
                           jnp.flip(h[1:, :, 1], axis=0)], axis=0)
    buf = buf / jnp.sum(jnp.abs(buf), axis=0, keepdims=True)
    hf = jnp.fft.rfft(buf, axis=0)
    return jnp.concatenate([hf, jnp.conj(jnp.flip(hf[1:n], axis=0))], axis=0)


def _fft_a_real_kernel(z_ref, m_ref, o_ref):
    o_ref[...] = _dot(m_ref[...], z_ref[...]).astype(o_ref.dtype)


def _fft_c_filter_kernel(af_ref, ab_ref, gf_ref, inv_ref, o_ref):
    n2, C = af_ref.shape[2], af_ref.shape[3]
    xf = _dot(gf_ref[0], af_ref[:, 0].reshape(2 * n2, C))
    xb = _dot(gf_ref[0], ab_ref[:, 0].reshape(2 * n2, C))
    inv = inv_ref[...]
    o_ref[0, 0] = (xf[:n2] + xb[:n2]) * inv
    o_ref[0, 1] = (xf[n2:] - xb[n2:]) * inv


def _filter_spectrum_fft(h, tabs):
    n, _, _, C = h.shape
    N1, S1, N2 = tabs["N1"], tabs["S1"], tabs["N2"]
    cols = N2 * C
    tc = 2 * C
    fwd = h[:, :, 0]
    bwd = h[:, :, 1].at[0].set(0.0)
    inv_norm = 1.0 / (jnp.sum(jnp.abs(fwd), axis=0) + jnp.sum(jnp.abs(bwd), axis=0))
    m_re = tabs["m_a"][:, :S1]
    out = []
    for o in range(HY_ORDER):
        parts = []
        for taps in (fwd[:, o], bwd[:, o]):
            a = pl.pallas_call(
                _fft_a_real_kernel,
                grid=(cols // tc,),
                in_specs=[pl.BlockSpec((S1, tc), lambda j: (0, j)),
                          pl.BlockSpec((2 * N1, S1), lambda j: (0, 0))],
                out_specs=pl.BlockSpec((2 * N1, tc), lambda j: (0, j)),
                out_shape=jax.ShapeDtypeStruct((2 * N1, cols), BF16),
                compiler_params=_cparams("parallel"),
                name="hy_filter_fft_a",
            )(taps.astype(BF16).reshape(S1, cols), m_re)
            parts.append(a.reshape(2, N1, N2, C))
        ab = pl.BlockSpec((2, 1, N2, C), lambda i: (0, i, 0, 0))
        out.append(pl.pallas_call(
            _fft_c_filter_kernel,
            grid=(N1,),
            in_specs=[ab, ab, pl.BlockSpec((1, 2 * N2, 2 * N2), lambda i: (i, 0, 0)),
                      pl.BlockSpec((1, C), lambda i: (0, 0))],
            out_specs=pl.BlockSpec((1, 2, N2, C), lambda i: (i, 0, 0, 0)),
            out_shape=jax.ShapeDtypeStruct((N1, 2, N2, C), F32),
            compiler_params=_cparams("parallel"),
            name="hy_filter_fft_c",
        )(parts[0], parts[1], tabs["g_f"], inv_norm[o].reshape(1, C)))
    return out


def _short_conv_kernel(u_ref, prev_ref, next_ref, w_ref, b_ref, o_ref):
    i, ni = pl.program_id(1), pl.num_programs(1)
    u = u_ref[0]
    tm = u.shape[0]
    row = lax.broadcasted_iota(I32, (tm, 1), 0)
    prev_row = jnp.where(i > 0, prev_ref[0, 7:8, :], 0.0)
    next_row = jnp.where(i < ni - 1, next_ref[0, 0:1, :], 0.0)
    before = jnp.where(row == 0, prev_row, pltpu.roll(u, 1, 0))
    after = jnp.where(row == tm - 1, next_row, pltpu.roll(u, tm - 1, 0))
    w = w_ref[...]
    o_ref[0, 0] = (before * w[0:1] + u * w[1:2] + after * w[2:3] + b_ref[...]).astype(o_ref.dtype)


def short_conv_split(u, conv_w, conv_b):
    B, n, C3 = u.shape
    C = C3 // 3
    tm = min(512, n)
    r8 = tm // 8
    return pl.pallas_call(
        _short_conv_kernel,
        grid=(B, n // tm, 3),
        in_specs=[pl.BlockSpec((1, tm, C), lambda b, i, j: (b, i, j)),
                  pl.BlockSpec((1, 8, C), lambda b, i, j: (b, jnp.maximum(i * r8 - 1, 0), j)),
                  pl.BlockSpec((1, 8, C), lambda b, i, j: (b, jnp.minimum((i + 1) * r8, n // 8 - 1), j)),
                  pl.BlockSpec((8, C), lambda b, i, j: (0, j)),
                  pl.BlockSpec((1, C), lambda b, i, j: (0, j))],
        out_specs=pl.BlockSpec((1, 1, tm, C), lambda b, i, j: (j, b, i, 0)),
        out_shape=jax.ShapeDtypeStruct((3, B, n, C), BF16),
        compiler_params=_cparams("parallel", "parallel", "parallel"),
        name="hy_short_conv",
    )(u, u, u, jnp.pad(conv_w.astype(F32), ((0, 5), (0, 0))), conv_b.astype(F32).reshape(1, C3))


FFT_N1 = 128


def _stack_complex(re, im):
    return jnp.concatenate([jnp.concatenate([re, -im], axis=-1),
                            jnp.concatenate([im, re], axis=-1)], axis=-2)


def _phase(k, period):
    a = (2.0 * math.pi / period) * (k % period).astype(F32)
    return jnp.cos(a), jnp.sin(a)


def _fft_a_kernel(z_ref, m_ref, o_ref):
    z = z_ref[...]
    o_ref[...] = _dot(m_ref[...], z.reshape(2 * z.shape[1], z.shape[2])).astype(o_ref.dtype)


def _fft_c_kernel(a_ref, gf_ref, gi_ref, h_ref, o_ref):
    n2, C = a_ref.shape[2], a_ref.shape[3]
    x = _dot(gf_ref[0], a_ref[:, 0].reshape(2 * n2, C))
    xr, xi = x[:n2], x[n2:]
    hr, hi = h_ref[0, 0], h_ref[0, 1]
    y = jnp.concatenate([xr * hr - xi * hi, xr * hi + xi * hr], axis=0).astype(BF16)
    o_ref[:, 0] = _dot(gi_ref[0], y).astype(o_ref.dtype).reshape(2, n2, C)


def _fft_b_kernel(b_ref, m_ref, u_ref, xg_ref, bias_ref, o_ref):
    s1, tc = u_ref.shape[1], u_ref.shape[2]
    y = _dot(m_ref[...], b_ref[...]).reshape(2, s1, tc)
    o_ref[...] = (xg_ref[...].astype(F32) * (y + u_ref[...].astype(F32) * bias_ref[...])).astype(o_ref.dtype)


def _dense_conv_kernel(u_ref, mf_ref, mi_ref, h_ref, xg_ref, bias_ref, o_ref):
    n, tc = u_ref.shape[1], u_ref.shape[2]
    u = u_ref[...]
    x = _dot(mf_ref[...], u.reshape(2 * n, tc))
    N = x.shape[0] // 2
    xr, xi = x[:N], x[N:]
    hr, hi = h_ref[0], h_ref[1]
    y = jnp.concatenate([xr * hr - xi * hi, xr * hi + xi * hr], axis=0).astype(BF16)
    out = _dot(mi_ref[...], y).reshape(2, n, tc)
    o_ref[...] = (xg_ref[...].astype(F32) * (out + u.astype(F32) * bias_ref[...])).astype(o_ref.dtype)


def _fft_tables(n):
    N = 2 * n
    N1, S1 = FFT_N1, FFT_N1 // 2
    N2 = N // N1
    f1 = jnp.arange(N1, dtype=I32)
    c, s = _phase(f1[:, None] * jnp.arange(S1, dtype=I32)[None, :], N1)
    m_a = _stack_complex(c, -s).astype(BF16)
    m_b = _stack_complex(c.T / N, s.T / N).astype(BF16)
    f2 = jnp.arange(N2, dtype=I32)
    c, s = _phase(f2[None, None, :] * (f1[:, None, None] + N1 * f2[None, :, None]), N)
    g_f = _stack_complex(c, -s).astype(BF16)
    g_i = _stack_complex(jnp.swapaxes(c, 1, 2), jnp.swapaxes(s, 1, 2)).astype(BF16)
    return dict(N1=N1, S1=S1, N2=N2, m_a=m_a, m_b=m_b, g_f=g_f, g_i=g_i)


def long_conv_gate(u, xg, spec, bias, tabs):
    B, n, C = u.shape
    assert B == 2
    N = 2 * n
    bias3 = bias.astype(F32).reshape(1, 1, C)
    if tabs is None:
        hr, hi = jnp.real(spec).astype(F32), jnp.imag(spec).astype(F32)
        f, t = jnp.arange(N, dtype=I32)[:, None], jnp.arange(n, dtype=I32)[None, :]
        c, s = _phase(f * t, N)
        mf = _stack_complex(c, -s).astype(BF16)
        mi = _stack_complex(c.T / N, s.T / N).astype(BF16)
        tc = 256
        blk = pl.BlockSpec((2, n, tc), lambda j: (0, 0, j))
        return pl.pallas_call(
            _dense_conv_kernel,
            grid=(C // tc,),
            in_specs=[blk, pl.BlockSpec((2 * N, 2 * n), lambda j: (0, 0)),
                      pl.BlockSpec((2 * n, 2 * N), lambda j: (0, 0)),
                      pl.BlockSpec((2, N, tc), lambda j: (0, 0, j)), blk,
                      pl.BlockSpec((1, 1, tc), lambda j: (0, 0, j))],
            out_specs=blk,
            out_shape=jax.ShapeDtypeStruct((2, n, C), BF16),
            compiler_params=_cparams("parallel"),
            name="hy_dense_conv",
        )(u, mf, mi, jnp.stack([hr, hi]), xg, bias3)

    N1, S1, N2 = tabs["N1"], tabs["S1"], tabs["N2"]
    m_a, m_b, g_f, g_i, hperm = tabs["m_a"], tabs["m_b"], tabs["g_f"], tabs["g_i"], spec
    cols = N2 * C
    tc = 2 * C
    zb = pl.BlockSpec((2, S1, tc), lambda j: (0, 0, j))
    a = pl.pallas_call(
        _fft_a_kernel,
        grid=(cols // tc,),
        in_specs=[zb, pl.BlockSpec((2 * N1, 2 * S1), lambda j: (0, 0))],
        out_specs=pl.BlockSpec((2 * N1, tc), lambda j: (0, j)),
        out_shape=jax.ShapeDtypeStruct((2 * N1, cols), BF16),
        compiler_params=_cparams("parallel"),
        name="hy_fft_a",
    )(u.reshape(2, S1, cols), m_a)
    ab = pl.BlockSpec((2, 1, N2, C), lambda i: (0, i, 0, 0))
    gb = pl.BlockSpec((1, 2 * N2, 2 * N2), lambda i: (i, 0, 0))
    bm = pl.pallas_call(
        _fft_c_kernel,
        grid=(N1,),
        in_specs=[ab, gb, gb, pl.BlockSpec((1, 2, N2, C), lambda i: (i, 0, 0, 0))],
        out_specs=ab,
        out_shape=jax.ShapeDtypeStruct((2, N1, N2, C), BF16),
        compiler_params=_cparams("parallel"),
        name="hy_fft_c",
    )(a.reshape(2, N1, N2, C), g_f, g_i, hperm)
    out = pl.pallas_call(
        _fft_b_kernel,
        grid=(cols // tc,),
        in_specs=[pl.BlockSpec((2 * N1, tc), lambda j: (0, j)),
                  pl.BlockSpec((2 * S1, 2 * N1), lambda j: (0, 0)), zb, zb,
                  pl.BlockSpec((1, 1, tc), lambda j: (0, 0, 0))],
        out_specs=zb,
        out_shape=jax.ShapeDtypeStruct((2, S1, cols), BF16),
        compiler_params=_cparams("parallel"),
        name="hy_fft_b",
    )(bm.reshape(2 * N1, cols), m_b, u.reshape(2, S1, cols), xg.reshape(2, S1, cols),
      jnp.tile(bias3, (1, 1, tc // C)))
    return out.reshape(2, n, C)


def hyena_layer(x, ctx, mv, norm_g, hy_p, need_ctx):
    (w_in, b_in, conv_w, conv_b, w1, b1, w2, b2, freq, w3, fbias, w_out, b_out) = hy_p
    w_in_bf = w_in.astype(BF16)
    w_out_bf = w_out.astype(BF16)

    def mix(h_in, sh, sc, res, gate):
        n = h_in.shape[1]
        u = mod_proj(h_in, norm_g, sh, sc, w_in_bf, b_in.astype(F32), out_dtype=F32, name="hy_project")
        v, x1, x2 = short_conv_split(u, conv_w, conv_b)
        taps = _hyena_filter_taps(n, w1, b1, w2, b2, freq, w3)
        if n <= 512:
            tabs = None
            hfull = _filter_spectrum_small(taps)
            spec = [hfull[:, o] for o in range(HY_ORDER)]
        else:
            tabs = _fft_tables(n)
            spec = _filter_spectrum_fft(taps, tabs)
        z = long_conv_gate(v, x1, spec[0], fbias[0], tabs)
        z = long_conv_gate(z, x2, spec[1], fbias[1], tabs)
        return res_matmul(z, w_out_bf, b_out.astype(F32), res, gate)

    x = mix(x, mv["sh_l"], mv["sc_l"], x, mv["g_l"])
    if need_ctx:
        ctx = mix(ctx, mv["sh_c"], mv["sc_c"], ctx, mv["g_c"])
    return x, ctx


def kernel(x, c, ctx, c_ctx, norm_mix_g, norm_ffn_g, mod_w, mod_b,
           da_w_in, da_q_norm, da_k_norm, da_lambda, da_sub_norm, da_w_out,
           ret_w_in, ret_log_alpha, ret_group_norm, ret_w_out,
           hy_w_in, hy_b_in, hy_conv_w, hy_conv_b, hy_ffn_w1, hy_ffn_b1, hy_ffn_w2, hy_ffn_b2,
           hy_sin_freq, hy_ffn_w3, hy_filter_bias, hy_w_out, hy_b_out,
           router_w, exp_w1, exp_w3, exp_w2):
    B = x.shape[0]
    depth = mod_w.shape[0]
    mods = _mod_all(c, c_ctx, mod_w, mod_b, B)
    for i in range(depth):
        last = i == depth - 1
        m, s = i % N_MIXERS, i // N_MIXERS
        mv = mods[i]
        if m == 0:
            lambda_init = 0.8 - 0.6 * math.exp(-0.3 * i)
            x, ctx = diff_attention_layer(x, ctx, mv, norm_mix_g[i], da_w_in[s], da_q_norm[s], da_k_norm[s],
                                          da_lambda[s], da_sub_norm[s], da_w_out[s], lambda_init, not last)
        elif m == 1:
            x, ctx = retention_layer(x, ctx, mv, norm_mix_g[i], ret_w_in[s], ret_log_alpha[s],
                                     ret_group_norm[s], ret_w_out[s], not last)
        else:
            hy_p = (hy_w_in[s], hy_b_in[s], hy_conv_w[s], hy_conv_b[s], hy_ffn_w1[s], hy_ffn_b1[s],
                    hy_ffn_w2[s], hy_ffn_b2[s], hy_sin_freq[s], hy_ffn_w3[s], hy_filter_bias[s],
                    hy_w_out[s], hy_b_out[s])
            x, ctx = hyena_layer(x, ctx, mv, norm_mix_g[i], hy_p, not last)
        w1, w3, w2 = exp_w1[i].astype(BF16), exp_w3[i].astype(BF16), exp_w2[i].astype(BF16)
        x = moe_layer(x, norm_ffn_g[i], mv["sh2_l"], mv["sc2_l"], mv["g2_l"], router_w[i], w1, w3, w2)
        if not last:
            ctx = moe_layer(ctx, norm_ffn_g[i], mv["sh2_c"], mv["sc2_c"], mv["g2_c"], router_w[i], w1, w3, w2)
    return x
```
